```python
import math
import jax, jax.numpy as jnp
from jax import lax
import numpy as np

D_MODEL = 2048
BATCH = 1
SEQ = 8192
DEPTH = 1

HEAD_DIM = 64
A_Q_HEADS = 16
A_KV_HEADS = 4
A_GROUP = A_Q_HEADS // A_KV_HEADS
WINDOW = 128
B_HEADS = 16
BLOCK = 128
D_FF = ((8 * D_MODEL // 3 + 255) // 256) * 256
EPS = 1e-6

A_Q_W = A_Q_HEADS * HEAD_DIM
A_KV_W = A_KV_HEADS * HEAD_DIM
B_W = B_HEADS * HEAD_DIM
IN_SPLITS = (A_Q_W, A_KV_W, A_KV_W, B_W, B_W, B_W, D_MODEL, D_MODEL)
IN_WIDTH = sum(IN_SPLITS)
IN_OFFSETS = tuple(int(o) for o in np.cumsum(IN_SPLITS)[:-1])

kernel_name = "hybrid_swa_sink_stickbreaking_gated"


def rmsnorm(x, g):
    xf = x.astype(jnp.float32)
    inv = lax.rsqrt(jnp.mean(xf * xf, axis=-1, keepdims=True) + EPS)
    return (xf * inv * g.astype(jnp.float32)).astype(x.dtype)


def alibi_slopes(n_heads):
    h = jnp.arange(1, n_heads + 1, dtype=jnp.float32)
    return jnp.exp2(-8.0 * h / n_heads)


def sliding_window_attention(q, k, v, sinks):
    B, S = q.shape[0], q.shape[1]
    nb = S // BLOCK
    qb = q.reshape(B, nb, BLOCK, A_KV_HEADS, A_GROUP, HEAD_DIM)
    kb = k.reshape(B, nb, BLOCK, A_KV_HEADS, HEAD_DIM)
    vb = v.reshape(B, nb, BLOCK, A_KV_HEADS, HEAD_DIM)
    pad_k = jnp.zeros_like(kb[:, :1])
    pad_v = jnp.zeros_like(vb[:, :1])
    kk = jnp.concatenate([jnp.concatenate([pad_k, kb[:, :-1]], axis=1), kb], axis=2)
    vv = jnp.concatenate([jnp.concatenate([pad_v, vb[:, :-1]], axis=1), vb], axis=2)

    scale = 1.0 / math.sqrt(HEAD_DIM)
    s = jnp.einsum('bnqhgd,bnkhd->bnhgqk', qb, kk).astype(jnp.float32) * scale

    qi = jnp.arange(BLOCK)[:, None]
    ki = jnp.arange(2 * BLOCK)[None, :]
    dist = (BLOCK + qi - ki)
    blk = jnp.arange(nb)[:, None, None]
    valid = (dist >= 0) & (dist < WINDOW)
    valid = valid[None] & ((blk > 0) | (ki[None] >= BLOCK))

    slopes = alibi_slopes(A_Q_HEADS).reshape(A_KV_HEADS, A_GROUP)
    s = s - slopes[:, :, None, None] * dist.astype(jnp.float32)[None, None]
    s = jnp.where(valid[None, :, None, None], s, -jnp.inf)

    sink = sinks.astype(jnp.float32).reshape(A_KV_HEADS, A_GROUP)[None, None, :, :, None, None]
    sink = jnp.broadcast_to(sink, s.shape[:-1] + (1,))
    p = jax.nn.softmax(jnp.concatenate([s, sink], axis=-1), axis=-1)[..., :-1]
    o = jnp.einsum('bnhgqk,bnkhd->bnqhgd', p.astype(v.dtype), vv)
    return o.reshape(B, S, A_Q_W)


def stick_breaking_attention(q, k, v):
    B, S = q.shape[0], q.shape[1]
    nb = S // BLOCK
    scale = 1.0 / math.sqrt(HEAD_DIM)
    qb = q.reshape(B, nb, BLOCK, B_HEADS, HEAD_DIM).transpose(1, 0, 2, 3, 4)
    kpos = jnp.arange(S)

    def one_block(args):
        qblk, i = args
        z = jnp.einsum('bqhd,bkhd->bhqk', qblk, k).astype(jnp.float32) * scale
        tpos = i * BLOCK + jnp.arange(BLOCK)
        causal = kpos[None, :] < tpos[:, None]
        log_beta = jax.nn.log_sigmoid(z)
        log_fail = jnp.where(causal, jax.nn.log_sigmoid(-z), 0.0)
        suffix = lax.cumsum(log_fail, axis=3, reverse=True) - log_fail
        a = jnp.where(causal, jnp.exp(log_beta + suffix), 0.0)
        return jnp.einsum('bhqk,bkhd->bqhd', a.astype(v.dtype), v)

    o = lax.map(one_block, (qb, jnp.arange(nb)))
    return o.transpose(1, 0, 2, 3, 4).reshape(B, S, B_W)


def swiglu(h, w_in, w_down):
    gu = h @ w_in
    gate, up = jnp.split(gu, 2, axis=-1)
    return (jax.nn.silu(gate) * up) @ w_down


def setup_inputs(seed: int = 0) -> dict:
    key = jax.random.key(seed)
    ks = jax.random.split(key, 12)
    f32 = jnp.float32

    def w(k, shape, fan_in):
        return jax.random.normal(k, shape, f32) * (fan_in ** -0.5)

    def gain(k, shape):
        return 1.0 + 0.02 * jax.random.normal(k, shape, f32)

    return {
        "x": jax.random.normal(ks[0], (BATCH, SEQ, D_MODEL), f32),
        "norm_mix_g": gain(ks[1], (DEPTH, D_MODEL)),
        "w_in": w(ks[2], (DEPTH, D_MODEL, IN_WIDTH), D_MODEL),
        "sink_logits": 0.5 * jax.random.normal(ks[3], (DEPTH, A_Q_HEADS), f32),
        "w_branch_a": w(ks[4], (DEPTH, A_Q_W, D_MODEL), A_Q_W),
        "w_branch_b": w(ks[5], (DEPTH, B_W, D_MODEL), B_W),
        "w_out": w(ks[6], (DEPTH, D_MODEL, D_MODEL), D_MODEL),
        "norm_ffn_g": gain(ks[7], (DEPTH, D_MODEL)),
        "w_ffn_in": w(ks[8], (DEPTH, D_MODEL, 2 * D_FF), D_MODEL),
        "w_ffn_down": w(ks[9], (DEPTH, D_FF, D_MODEL), D_FF),
        "norm_final_g": gain(ks[10], (D_MODEL,)),
    }


def reference(x, norm_mix_g, w_in, sink_logits, w_branch_a, w_branch_b, w_out,
              norm_ffn_g, w_ffn_in, w_ffn_down, norm_final_g):
    B, S, _ = x.shape
    for layer in range(DEPTH):
        h = rmsnorm(x, norm_mix_g[layer])
        proj = h @ w_in[layer]
        qa, ka, va, qb, kb, vb, ga, gb = jnp.split(proj, IN_OFFSETS, axis=-1)

        ya = sliding_window_attention(
            qa.reshape(B, S, A_Q_HEADS, HEAD_DIM),
            ka.reshape(B, S, A_KV_HEADS, HEAD_DIM),
            va.reshape(B, S, A_KV_HEADS, HEAD_DIM),
            sink_logits[layer]) @ w_branch_a[layer]
        yb = stick_breaking_attention(
            qb.reshape(B, S, B_HEADS, HEAD_DIM),
            kb.reshape(B, S, B_HEADS, HEAD_DIM),
            vb.reshape(B, S, B_HEADS, HEAD_DIM)) @ w_branch_b[layer]

        merged = jax.nn.sigmoid(ga) * ya + jax.nn.sigmoid(gb) * yb
        x = x + merged @ w_out[layer]
        x = x + swiglu(rmsnorm(x, norm_ffn_g[layer]), w_ffn_in[layer], w_ffn_down[layer])
    return rmsnorm(x, norm_final_g)
```

```python
import functools
import math

import numpy as np
import jax
import jax.numpy as jnp
from jax import lax
from jax.experimental import pallas as pl
from jax.experimental.pallas import tpu as pltpu

D_MODEL = 2048
SEQ = 8192
HEAD_DIM = 64
A_Q_HEADS = 16
A_KV_HEADS = 4
A_GROUP = A_Q_HEADS // A_KV_HEADS
WINDOW = 128
B_HEADS = 16
BLOCK = 128
D_FF = 5632
EPS = 1e-6
SCALE = 1.0 / math.sqrt(HEAD_DIM)

A_Q_W = A_Q_HEADS * HEAD_DIM
A_KV_W = A_KV_HEADS * HEAD_DIM
B_W = B_HEADS * HEAD_DIM
IN_WIDTH = A_Q_W + 2 * A_KV_W + 3 * B_W + 2 * D_MODEL

LANES = 128
PAIRS_A = A_Q_HEADS // 2
PAIRS_B = B_HEADS // 2

GA_OFF = 0
GB_OFF = GA_OFF + D_MODEL
QA_OFF = GB_OFF + D_MODEL
KA_OFF = QA_OFF + A_Q_W
VA_OFF = KA_OFF + A_KV_W
QB_OFF = VA_OFF + A_KV_W
KB_OFF = QB_OFF + B_W
VB_OFF = KB_OFF + B_W

F32_EXP_ZERO = -104.0

VMEM_LIMIT = 56 * 1024 * 1024

_NT = (((1,), (1,)), ((), ()))


def _a_head_order():
    evens = [h for h in range(A_Q_HEADS) if (h // A_GROUP) % 2 == 0]
    odds = [h for h in range(A_Q_HEADS) if (h // A_GROUP) % 2 == 1]
    order = []
    for l, r in zip(evens, odds):
        order += [l, r]
    return order


A_ORDER = _a_head_order()


def _in_proj_columns(w):
    o_qa, o_ka = 0, A_Q_W
    o_qb = A_Q_W + 2 * A_KV_W
    o_ga = o_qb + 3 * B_W
    parts = [w[:, o_ga:o_ga + 2 * D_MODEL]]
    parts += [w[:, o_qa + h * HEAD_DIM:o_qa + (h + 1) * HEAD_DIM] for h in A_ORDER]
    parts += [w[:, o_ka:o_ga]]
    return jnp.concatenate(parts, axis=1)


def _branch_a_rows(w):
    return jnp.concatenate([w[h * HEAD_DIM:(h + 1) * HEAD_DIM] for h in A_ORDER], axis=0)


def _in_proj_kernel(x_ref, g_ref, w_ref, o_ref, h_ref, *, row_chunk):
    @pl.when(pl.program_id(1) == 0)
    def _():
        g = g_ref[...]

        def body(c, _):
            r = pl.multiple_of(c * row_chunk, row_chunk)
            x = x_ref[pl.ds(r, row_chunk), :]
            inv = lax.rsqrt(jnp.mean(x * x, axis=-1, keepdims=True) + EPS)
            h_ref[pl.ds(r, row_chunk), :] = (x * inv * g).astype(jnp.bfloat16)
            return 0

        lax.fori_loop(0, x_ref.shape[0] // row_chunk, body, 0)

    o_ref[...] = jnp.dot(h_ref[...], w_ref[...],
                         preferred_element_type=jnp.float32).astype(o_ref.dtype)


def _in_proj(x, g, w, *, tm=1024, tn=512):
    s, d = x.shape
    n = w.shape[1]
    return pl.pallas_call(
        functools.partial(_in_proj_kernel, row_chunk=128),
        grid=(s // tm, n // tn),
        in_specs=[
            pl.BlockSpec((tm, d), lambda i, j: (i, 0)),
            pl.BlockSpec((1, d), lambda i, j: (0, 0)),
            pl.BlockSpec((d, tn), lambda i, j: (0, j)),
        ],
        out_specs=pl.BlockSpec((tm, tn), lambda i, j: (i, j)),
        out_shape=jax.ShapeDtypeStruct((s, n), jnp.bfloat16),
        scratch_shapes=[pltpu.VMEM((tm, d), jnp.bfloat16)],
        compiler_params=pltpu.CompilerParams(
            dimension_semantics=("arbitrary", "arbitrary"),
            vmem_limit_bytes=VMEM_LIMIT),
        name="in_proj",
    )(x, g, w)


def _swa_kernel(sink_ref, q_ref, kp_ref, kc_ref, vp_ref, vc_ref, o_ref):
    i = pl.program_id(0)
    k = jnp.concatenate([kp_ref[...], kc_ref[...]], axis=0)
    v = jnp.concatenate([vp_ref[...], vc_ref[...]], axis=0)
    qi = lax.broadcasted_iota(jnp.int32, (BLOCK, 2 * BLOCK), 0)
    ki = lax.broadcasted_iota(jnp.int32, (BLOCK, 2 * BLOCK), 1)
    dist = BLOCK + qi - ki
    valid = (dist >= 0) & (dist < WINDOW) & ((ki >= BLOCK) | (i > 0))
    distf = dist.astype(jnp.float32)
    lane = lax.broadcasted_iota(jnp.int32, (BLOCK, LANES), 1)
    left = lane < HEAD_DIM

    for b in range(PAIRS_A):
        heads = (A_ORDER[2 * b], A_ORDER[2 * b + 1])
        tile = (heads[0] // A_GROUP) // 2
        kh = k[:, tile * LANES:(tile + 1) * LANES]
        vh = v[:, tile * LANES:(tile + 1) * LANES]
        q = q_ref[:, b * LANES:(b + 1) * LANES] * jnp.bfloat16(SCALE)
        zero = jnp.zeros_like(q)
        lhs = jnp.concatenate([jnp.where(left, q, zero), jnp.where(left, zero, q)], axis=0)
        s2 = lax.dot_general(lhs, kh, _NT, preferred_element_type=jnp.float32)
        ps, inv_den = [], []
        for side, h in enumerate(heads):
            slope = 2.0 ** (-8.0 * (h + 1) / A_Q_HEADS)
            s = s2[side * BLOCK:(side + 1) * BLOCK] - slope * distf
            s = jnp.where(valid, s, -jnp.inf)
            sink = sink_ref[h]
            m = jnp.maximum(jnp.max(s, axis=-1, keepdims=True), sink)
            p = jnp.exp(s - m)
            den = jnp.sum(p, axis=-1, keepdims=True) + jnp.exp(sink - m)
            ps.append(p.astype(jnp.bfloat16))
            inv_den.append(1.0 / den)
        o2 = jnp.dot(jnp.concatenate(ps, axis=0), vh, preferred_element_type=jnp.float32)
        out = jnp.where(left, o2[:BLOCK] * inv_den[0], o2[BLOCK:] * inv_den[1])
        o_ref[:, b * LANES:(b + 1) * LANES] = out.astype(o_ref.dtype)


def _swa(proj, sinks):
    s = proj.shape[0]
    nb = s // BLOCK
    qa_blk = QA_OFF // A_Q_W
    ka_blk = KA_OFF // A_KV_W
    va_blk = VA_OFF // A_KV_W
    prev = lambda i: jnp.maximum(i - 1, 0)
    return pl.pallas_call(
        _swa_kernel,
        grid=(nb,),
        in_specs=[
            pl.BlockSpec(memory_space=pltpu.SMEM),
            pl.BlockSpec((BLOCK, A_Q_W), lambda i: (i, qa_blk)),
            pl.BlockSpec((BLOCK, A_KV_W), lambda i: (prev(i), ka_blk)),
            pl.BlockSpec((BLOCK, A_KV_W), lambda i: (i, ka_blk)),
            pl.BlockSpec((BLOCK, A_KV_W), lambda i: (prev(i), va_blk)),
            pl.BlockSpec((BLOCK, A_KV_W), lambda i: (i, va_blk)),
        ],
        out_specs=pl.BlockSpec((BLOCK, A_Q_W), lambda i: (i, 0)),
        out_shape=jax.ShapeDtypeStruct((s, A_Q_W), jnp.bfloat16),
        compiler_params=pltpu.CompilerParams(
            dimension_semantics=("arbitrary",), vmem_limit_bytes=VMEM_LIMIT),
        name="swa",
    )(sinks, proj, proj, proj, proj, proj)


def _stick_constants():
    def one(n):
        j = np.arange(n)[:, None]
        s = np.arange(n)[None, :]
        u = (j > s).astype(np.float32)
        m = np.concatenate([u, np.ones((n, LANES), np.float32)], axis=1)
        return np.concatenate([m, m], axis=0)
    return (jnp.asarray(one(2 * BLOCK), jnp.bfloat16), jnp.asarray(one(BLOCK), jnp.bfloat16))


def _split_bf16(x):
    hi = x.astype(jnp.bfloat16)
    lo = (x - hi.astype(jnp.float32)).astype(jnp.bfloat16)
    return jnp.concatenate([hi, lo], axis=1)


def _log_terms(z):
    l = jnp.log(1.0 + jnp.exp(-jnp.abs(z)))
    return jnp.minimum(z, 0.0) - l, jnp.minimum(-z, 0.0) - l


def _stick_kernel(q_ref, k_ref, v_ref, c1_ref, c2_ref, o_ref):
    i = pl.program_id(1)
    lane = lax.broadcasted_iota(jnp.int32, (BLOCK, LANES), 1)
    left = lane < HEAD_DIM
    q = q_ref[...] * jnp.bfloat16(SCALE)
    zero = jnp.zeros_like(q)
    lhs = jnp.concatenate([jnp.where(left, q, zero), jnp.where(left, zero, q)], axis=0)

    first = jnp.maximum(i - 1, 0)
    start = pl.multiple_of(first * BLOCK, BLOCK)
    k0 = k_ref[pl.ds(start, 2 * BLOCK), :]
    v0 = v_ref[pl.ds(start, 2 * BLOCK), :]
    z = lax.dot_general(lhs, k0, _NT, preferred_element_type=jnp.float32)
    row = lax.broadcasted_iota(jnp.int32, (2 * BLOCK, 2 * BLOCK), 0)
    col = lax.broadcasted_iota(jnp.int32, (2 * BLOCK, 2 * BLOCK), 1)
    causal = (start + col) < (i * BLOCK + (row & (BLOCK - 1)))
    lb, lf = _log_terms(z)
    lf = jnp.where(causal, lf, 0.0)
    sums = jnp.dot(_split_bf16(lf), c1_ref[...], preferred_element_type=jnp.float32)
    a = jnp.where(causal, jnp.exp(lb + sums[:, :2 * BLOCK]), 0.0)
    acc = jnp.dot(a.astype(jnp.bfloat16), v0, preferred_element_type=jnp.float32)
    tail = sums[:, 2 * BLOCK:]

    def cond(c):
        kb, live, _, _ = c
        return jnp.logical_and(kb >= 0, live > 0)

    def body(c):
        kb, _, acc, tail = c
        st = pl.multiple_of(kb * BLOCK, BLOCK)
        kk = k_ref[pl.ds(st, BLOCK), :]
        vv = v_ref[pl.ds(st, BLOCK), :]
        z = lax.dot_general(lhs, kk, _NT, preferred_element_type=jnp.float32)
        lb, lf = _log_terms(z)
        sums = jnp.dot(_split_bf16(lf), c2_ref[...], preferred_element_type=jnp.float32)
        a = jnp.exp(lb + sums[:, :BLOCK] + tail)
        acc = acc + jnp.dot(a.astype(jnp.bfloat16), vv, preferred_element_type=jnp.float32)
        tail = tail + sums[:, BLOCK:]
        live = (jnp.max(tail) > F32_EXP_ZERO).astype(jnp.int32)
        return kb - 1, live, acc, tail

    live0 = (jnp.max(tail) > F32_EXP_ZERO).astype(jnp.int32)
    _, _, acc, _ = lax.while_loop(cond, body, (first - 1, live0, acc, tail))
    o_ref[...] = jnp.where(left, acc[:BLOCK], acc[BLOCK:]).astype(o_ref.dtype)


def _stick(proj):
    s = proj.shape[0]
    nb = s // BLOCK
    c1, c2 = _stick_constants()
    qb_blk = QB_OFF // LANES
    kb_blk = KB_OFF // LANES
    vb_blk = VB_OFF // LANES
    return pl.pallas_call(
        _stick_kernel,
        grid=(PAIRS_B, nb),
        in_specs=[
            pl.BlockSpec((BLOCK, LANES), lambda b, i: (i, qb_blk + b)),
            pl.BlockSpec((s, LANES), lambda b, i: (0, kb_blk + b)),
            pl.BlockSpec((s, LANES), lambda b, i: (0, vb_blk + b)),
            pl.BlockSpec(c1.shape, lambda b, i: (0, 0)),
            pl.BlockSpec(c2.shape, lambda b, i: (0, 0)),
        ],
        out_specs=pl.BlockSpec((BLOCK, LANES), lambda b, i: (i, b)),
        out_shape=jax.ShapeDtypeStruct((s, B_W), jnp.bfloat16),
        compiler_params=pltpu.CompilerParams(
            dimension_semantics=("arbitrary", "arbitrary"), vmem_limit_bytes=VMEM_LIMIT),
        name="stick",
    )(proj, proj, proj, c1, c2)


def _merge_kernel(oa_ref, ob_ref, ga_ref, gb_ref, x_ref, wa_ref, wb_ref, wo_ref, o_ref,
                  m_ref, *, tn):
    d = o_ref.shape[1]
    for c in range(d // tn):
        sl = slice(c * tn, (c + 1) * tn)
        ya = jnp.dot(oa_ref[...], wa_ref[:, sl], preferred_element_type=jnp.float32)
        yb = jnp.dot(ob_ref[...], wb_ref[:, sl], preferred_element_type=jnp.float32)
        ga = jax.nn.sigmoid(ga_ref[:, sl].astype(jnp.float32))
        gb = jax.nn.sigmoid(gb_ref[:, sl].astype(jnp.float32))
        m_ref[:, sl] = (ga * ya + gb * yb).astype(m_ref.dtype)
    for c in range(d // tn):
        sl = slice(c * tn, (c + 1) * tn)
        o_ref[:, sl] = x_ref[:, sl] + jnp.dot(m_ref[...], wo_ref[:, sl],
                                              preferred_element_type=jnp.float32)


def _merge(oa, ob, proj, x, wa, wb, wo, *, tm=256, tn=512):
    s, d = x.shape
    const = lambda i: (0, 0)
    return pl.pallas_call(
        functools.partial(_merge_kernel, tn=tn),
        grid=(s // tm,),
        in_specs=[
            pl.BlockSpec((tm, A_Q_W), lambda i: (i, 0)),
            pl.BlockSpec((tm, B_W), lambda i: (i, 0)),
            pl.BlockSpec((tm, d), lambda i: (i, GA_OFF // D_MODEL)),
            pl.BlockSpec((tm, d), lambda i: (i, GB_OFF // D_MODEL)),
            pl.BlockSpec((tm, d), lambda i: (i, 0)),
            pl.BlockSpec(wa.shape, const),
            pl.BlockSpec(wb.shape, const),
            pl.BlockSpec(wo.shape, const),
        ],
        out_specs=pl.BlockSpec((tm, d), lambda i: (i, 0)),
        out_shape=jax.ShapeDtypeStruct((s, d), jnp.float32),
        scratch_shapes=[pltpu.VMEM((tm, d), jnp.bfloat16)],
        compiler_params=pltpu.CompilerParams(
            dimension_semantics=("arbitrary",), vmem_limit_bytes=VMEM_LIMIT),
        name="merge",
    )(oa, ob, proj, proj, x, wa, wb, wo)


def _ffn_kernel(x_ref, g_ref, wg_ref, wu_ref, wd_ref, gf_ref, o_ref, h_ref, *, row_chunk):
    f = pl.program_id(1)
    rows = x_ref.shape[0]

    @pl.when(f == 0)
    def _():
        g = g_ref[...]

        def body(c, _):
            r = pl.multiple_of(c * row_chunk, row_chunk)
            x = x_ref[pl.ds(r, row_chunk), :]
            inv = lax.rsqrt(jnp.mean(x * x, axis=-1, keepdims=True) + EPS)
            h_ref[pl.ds(r, row_chunk), :] = (x * inv * g).astype(jnp.bfloat16)
            return 0

        lax.fori_loop(0, rows // row_chunk, body, 0)

    h = h_ref[...]
    gate = jnp.dot(h, wg_ref[...], preferred_element_type=jnp.float32)
    up = jnp.dot(h, wu_ref[...], preferred_element_type=jnp.float32)
    act = (gate * jax.nn.sigmoid(gate) * up).astype(jnp.bfloat16)
    down = jnp.dot(act, wd_ref[...], preferred_element_type=jnp.float32)

    @pl.when(f == 0)
    def _():
        o_ref[...] = x_ref[...] + down

    @pl.when(f > 0)
    def _():
        o_ref[...] += down

    @pl.when(f == pl.num_programs(1) - 1)
    def _():
        gf = gf_ref[...]

        def body(c, _):
            r = pl.multiple_of(c * row_chunk, row_chunk)
            y = o_ref[pl.ds(r, row_chunk), :]
            inv = lax.rsqrt(jnp.mean(y * y, axis=-1, keepdims=True) + EPS)
            o_ref[pl.ds(r, row_chunk), :] = y * inv * gf
            return 0

        lax.fori_loop(0, rows // row_chunk, body, 0)


def _ffn(x1, g, w_in, w_down, gf, *, tm=512, tf=512):
    s, d = x1.shape
    nf = D_FF // tf
    return pl.pallas_call(
        functools.partial(_ffn_kernel, row_chunk=128),
        grid=(s // tm, nf),
        in_specs=[
            pl.BlockSpec((tm, d), lambda i, f: (i, 0)),
            pl.BlockSpec((1, d), lambda i, f: (0, 0)),
            pl.BlockSpec((d, tf), lambda i, f: (0, f)),
            pl.BlockSpec((d, tf), lambda i, f: (0, f + nf)),
            pl.BlockSpec((tf, d), lambda i, f: (f, 0)),
            pl.BlockSpec((1, d), lambda i, f: (0, 0)),
        ],
        out_specs=pl.BlockSpec((tm, d), lambda i, f: (i, 0)),
        out_shape=jax.ShapeDtypeStruct((s, d), jnp.float32),
        scratch_shapes=[pltpu.VMEM((tm, d), jnp.bfloat16)],
        compiler_params=pltpu.CompilerParams(
            dimension_semantics=("arbitrary", "arbitrary"), vmem_limit_bytes=VMEM_LIMIT),
        name="ffn",
    )(x1, g, w_in, w_in, w_down, gf)


def kernel(x, norm_mix_g, w_in, sink_logits, w_branch_a, w_branch_b, w_out,
           norm_ffn_g, w_ffn_in, w_ffn_down, norm_final_g):
    b, s, d = x.shape
    assert (b, s, d) == (1, SEQ, D_MODEL) and w_in.shape[0] == 1
    bf16 = jnp.bfloat16
    x2 = x.reshape(s, d)
    proj = _in_proj(x2, norm_mix_g[0].reshape(1, d), _in_proj_columns(w_in[0]).astype(bf16))
    oa = _swa(proj, sink_logits[0])
    ob = _stick(proj)
    x1 = _merge(oa, ob, proj, x2, _branch_a_rows(w_branch_a[0]).astype(bf16),
                w_branch_b[0].astype(bf16), w_out[0].astype(bf16))
    out = _ffn(x1, norm_ffn_g[0].reshape(1, d), w_ffn_in[0].astype(bf16),
               w_ffn_down[0].astype(bf16), norm_final_g.reshape(1, d))
    return out.reshape(b, s, d)
```

```python
import functools
import math

import numpy as np
import jax
import jax.numpy as jnp
from jax import lax
from jax.experimental import pallas as pl
from jax.experimental.pallas import tpu as pltpu

D_MODEL = 2048
SEQ = 8192
HEAD_DIM = 64
A_Q_HEADS = 16
A_KV_HEADS = 4
A_GROUP = A_Q_HEADS // A_KV_HEADS
WINDOW = 128
B_HEADS = 16
BLOCK = 128
D_FF = 5632
EPS = 1e-6
SCALE = 1.0 / math.sqrt(HEAD_DIM)

A_Q_W = A_Q_HEADS * HEAD_DIM
A_KV_W = A_KV_HEADS * HEAD_DIM
B_W = B_HEADS * HEAD_DIM
IN_WIDTH = A_Q_W + 2 * A_KV_W + 3 * B_W + 2 * D_MODEL

LANES = 128
PAIRS_B = B_HEADS // 2

QA_OFF = 0
KA_OFF = QA_OFF + A_Q_W
VA_OFF = KA_OFF + A_KV_W
QB_OFF = VA_OFF + A_KV_W
KB_OFF = QB_OFF + B_W
VB_OFF = KB_OFF + B_W
GA_OFF = VB_OFF + B_W
GB_OFF = GA_OFF + D_MODEL

F32_EXP_ZERO = -104.0

VMEM_LIMIT = 56 * 1024 * 1024

_NT = (((1,), (1,)), ((), ()))


def _col_tiles(w, tn):
    k, n = w.shape
    return w.reshape(k, n // tn, tn).transpose(1, 0, 2)


def _rmsnorm_rows(src_ref, dst_ref, g, row_chunk):
    def body(c, _):
        r = pl.multiple_of(c * row_chunk, row_chunk)
        x = src_ref[pl.ds(r, row_chunk), :]
        inv = lax.rsqrt(jnp.mean(x * x, axis=-1, keepdims=True) + EPS)
        dst_ref[pl.ds(r, row_chunk), :] = (x * inv * g).astype(dst_ref.dtype)
        return 0

    lax.fori_loop(0, src_ref.shape[0] // row_chunk, body, 0)


def _in_proj_kernel(x_ref, g_ref, w_ref, o_ref, h_ref, *, row_chunk):
    @pl.when(pl.program_id(1) == 0)
    def _():
        _rmsnorm_rows(x_ref, h_ref, g_ref[...], row_chunk)

    o_ref[...] = jnp.dot(h_ref[...], w_ref[0],
                         preferred_element_type=jnp.float32).astype(o_ref.dtype)


def _in_proj(x, g, w_tiles, *, tm=1024):
    s, d = x.shape
    nt, _, tn = w_tiles.shape
    return pl.pallas_call(
        functools.partial(_in_proj_kernel, row_chunk=128),
        grid=(s // tm, nt),
        in_specs=[
            pl.BlockSpec((tm, d), lambda i, j: (i, 0)),
            pl.BlockSpec((1, d), lambda i, j: (0, 0)),
            pl.BlockSpec((1, d, tn), lambda i, j: (j, 0, 0)),
        ],
        out_specs=pl.BlockSpec((tm, tn), lambda i, j: (i, j)),
        out_shape=jax.ShapeDtypeStruct((s, nt * tn), jnp.bfloat16),
        scratch_shapes=[pltpu.VMEM((tm, d), jnp.bfloat16)],
        compiler_params=pltpu.CompilerParams(
            dimension_semantics=("arbitrary", "arbitrary"),
            vmem_limit_bytes=VMEM_LIMIT),
        name="in_proj",
    )(x, g, w_tiles)


def _half_swap_matrix(n):
    c = np.arange(n)
    p = np.zeros((n, n), np.float32)
    p[c, c ^ HEAD_DIM] = 1.0
    return jnp.asarray(p, jnp.bfloat16)


def _swa_kernel(sink_ref, q_ref, kp_ref, kc_ref, vp_ref, vc_ref, p_ref, o_ref):
    i = pl.program_id(0)
    k = jnp.concatenate([kp_ref[...], kc_ref[...]], axis=0)
    v = jnp.concatenate([vp_ref[...], vc_ref[...]], axis=0)
    k_sw = jnp.dot(k, p_ref[...], preferred_element_type=jnp.float32).astype(k.dtype)
    v_sw = jnp.dot(v, p_ref[...], preferred_element_type=jnp.float32).astype(v.dtype)

    qi = lax.broadcasted_iota(jnp.int32, (BLOCK, 2 * BLOCK), 0)
    ki = lax.broadcasted_iota(jnp.int32, (BLOCK, 2 * BLOCK), 1)
    dist = BLOCK + qi - ki
    valid = (dist >= 0) & (dist < WINDOW) & ((ki >= BLOCK) | (i > 0))
    distf = dist.astype(jnp.float32)
    lane = lax.broadcasted_iota(jnp.int32, (BLOCK, LANES), 1)
    left = lane < HEAD_DIM

    for j in range(A_KV_HEADS):
        tile = slice((j // 2) * LANES, (j // 2 + 1) * LANES)
        qa = q_ref[:, (2 * j) * LANES:(2 * j + 1) * LANES] * jnp.bfloat16(SCALE)
        qb = q_ref[:, (2 * j + 1) * LANES:(2 * j + 2) * LANES] * jnp.bfloat16(SCALE)
        zero = jnp.zeros_like(qa)
        outs = []
        for side in range(2):
            keep = left if side == 0 else jnp.logical_not(left)
            aligned = (j % 2) == side
            kh = (k if aligned else k_sw)[:, tile]
            vh = (v if aligned else v_sw)[:, tile]
            lhs = jnp.concatenate([jnp.where(keep, qa, zero), jnp.where(keep, qb, zero)], axis=0)
            s2 = lax.dot_general(lhs, kh, _NT, preferred_element_type=jnp.float32)
            ps, inv_den = [], []
            for t in range(2):
                h = A_GROUP * j + 2 * t + side
                slope = 2.0 ** (-8.0 * (h + 1) / A_Q_HEADS)
                s = s2[t * BLOCK:(t + 1) * BLOCK] - slope * distf
                s = jnp.where(valid, s, -jnp.inf)
                sink = sink_ref[h]
                m = jnp.maximum(jnp.max(s, axis=-1, keepdims=True), sink)
                p = jnp.exp(s - m)
                den = jnp.sum(p, axis=-1, keepdims=True) + jnp.exp(sink - m)
                ps.append(p.astype(jnp.bfloat16))
                inv_den.append(1.0 / den)
            o2 = jnp.dot(jnp.concatenate(ps, axis=0), vh, preferred_element_type=jnp.float32)
            outs.append((o2[:BLOCK] * inv_den[0], o2[BLOCK:] * inv_den[1]))
        for t in range(2):
            out = jnp.where(left, outs[0][t], outs[1][t])
            o_ref[:, (2 * j + t) * LANES:(2 * j + t + 1) * LANES] = out.astype(o_ref.dtype)


def _swa(proj, sinks):
    s = proj.shape[0]
    nb = s // BLOCK
    qa_blk = QA_OFF // A_Q_W
    ka_blk = KA_OFF // A_KV_W
    va_blk = VA_OFF // A_KV_W
    prev = lambda i: jnp.maximum(i - 1, 0)
    swap = _half_swap_matrix(A_KV_W)
    return pl.pallas_call(
        _swa_kernel,
        grid=(nb,),
        in_specs=[
            pl.BlockSpec(memory_space=pltpu.SMEM),
            pl.BlockSpec((BLOCK, A_Q_W), lambda i: (i, qa_blk)),
            pl.BlockSpec((BLOCK, A_KV_W), lambda i: (prev(i), ka_blk)),
            pl.BlockSpec((BLOCK, A_KV_W), lambda i: (i, ka_blk)),
            pl.BlockSpec((BLOCK, A_KV_W), lambda i: (prev(i), va_blk)),
            pl.BlockSpec((BLOCK, A_KV_W), lambda i: (i, va_blk)),
            pl.BlockSpec(swap.shape, lambda i: (0, 0)),
        ],
        out_specs=pl.BlockSpec((BLOCK, A_Q_W), lambda i: (i, 0)),
        out_shape=jax.ShapeDtypeStruct((s, A_Q_W), jnp.bfloat16),
        compiler_params=pltpu.CompilerParams(
            dimension_semantics=("arbitrary",), vmem_limit_bytes=VMEM_LIMIT),
        name="swa",
    )(sinks, proj, proj, proj, proj, proj, swap)


STICK_FIRST_BLOCKS = 3


def _stick_constant():
    j = np.arange(BLOCK)[:, None]
    s = np.arange(BLOCK)[None, :]
    m = np.concatenate([(j > s).astype(np.float32), np.ones((BLOCK, LANES), np.float32)], axis=1)
    return jnp.asarray(np.concatenate([m, m], axis=0), jnp.bfloat16)


def _split_bf16(x):
    hi = x.astype(jnp.bfloat16)
    lo = (x - hi.astype(jnp.float32)).astype(jnp.bfloat16)
    return jnp.concatenate([hi, lo], axis=1)


def _log_beta(z):
    return jnp.minimum(z, 0.0) - jnp.log(1.0 + jnp.exp(-jnp.abs(z)))


def _stick_first(lhs, k_ref, v_ref, c, blk, mask_all):
    nk = STICK_FIRST_BLOCKS
    first = jnp.maximum(blk - (nk - 1), 0)
    start = pl.multiple_of(first * BLOCK, BLOCK)
    k0 = k_ref[pl.ds(start, nk * BLOCK), :]
    v0 = v_ref[pl.ds(start, nk * BLOCK), :]
    z = lax.dot_general(lhs, k0, _NT, preferred_element_type=jnp.float32)
    row = lax.broadcasted_iota(jnp.int32, (2 * BLOCK, BLOCK), 0) & (BLOCK - 1)
    col = lax.broadcasted_iota(jnp.int32, (2 * BLOCK, BLOCK), 1)
    lbs, sums, masks = [], [], []
    for n in range(nk):
        zn = z[:, n * BLOCK:(n + 1) * BLOCK]
        lb = _log_beta(zn)
        lf = lb - zn
        if mask_all:
            m = (first + n - blk) * BLOCK + col < row
        elif n == nk - 1:
            m = col < row
        else:
            m = None
        if m is not None:
            lf = jnp.where(m, lf, 0.0)
        lbs.append(lb)
        masks.append(m)
        sums.append(jnp.dot(_split_bf16(lf), c, preferred_element_type=jnp.float32))
    tail = jnp.zeros((2 * BLOCK, LANES), jnp.float32)
    a = [None] * nk
    for n in reversed(range(nk)):
        an = jnp.exp(lbs[n] + sums[n][:, :BLOCK] + tail)
        if masks[n] is not None:
            an = jnp.where(masks[n], an, 0.0)
        a[n] = an.astype(jnp.bfloat16)
        tail = tail + sums[n][:, BLOCK:]
    acc = jnp.dot(jnp.concatenate(a, axis=1), v0, preferred_element_type=jnp.float32)
    return acc, tail, first


def _stick_rest(lhs, k_ref, v_ref, c, acc, tail, first):
    def live(t):
        return (jnp.max(t) > F32_EXP_ZERO).astype(jnp.int32)

    def cond(carry):
        kb, alive, _, _ = carry
        return jnp.logical_and(kb >= 0, alive > 0)

    def body(carry):
        kb, _, acc, tail = carry
        st = pl.multiple_of(kb * BLOCK, BLOCK)
        z = lax.dot_general(lhs, k_ref[pl.ds(st, BLOCK), :], _NT,
                            preferred_element_type=jnp.float32)
        lb = _log_beta(z)
        sums = jnp.dot(_split_bf16(lb - z), c, preferred_element_type=jnp.float32)
        a = jnp.exp(lb + sums[:, :BLOCK] + tail)
        acc = acc + jnp.dot(a.astype(jnp.bfloat16), v_ref[pl.ds(st, BLOCK), :],
                            preferred_element_type=jnp.float32)
        tail = tail + sums[:, BLOCK:]
        return kb - 1, live(tail), acc, tail

    _, _, acc, _ = lax.while_loop(cond, body, (first - 1, live(tail), acc, tail))
    return acc


def _stick_kernel(q_ref, k_ref, v_ref, c_ref, o_ref, *, q_blocks):
    it = pl.program_id(1)
    lane = lax.broadcasted_iota(jnp.int32, (BLOCK, LANES), 1)
    left = lane < HEAD_DIM
    c = c_ref[...]

    def run(mask_all):
        firsts = []
        for g in range(q_blocks):
            q = q_ref[g * BLOCK:(g + 1) * BLOCK, :] * jnp.bfloat16(SCALE)
            zero = jnp.zeros_like(q)
            lhs = jnp.concatenate([jnp.where(left, q, zero), jnp.where(left, zero, q)], axis=0)
            firsts.append((lhs,) + _stick_first(lhs, k_ref, v_ref, c, it * q_blocks + g, mask_all))
        for g, (lhs, acc, tail, first) in enumerate(firsts):
            acc = _stick_rest(lhs, k_ref, v_ref, c, acc, tail, first)
            o_ref[g * BLOCK:(g + 1) * BLOCK, :] = jnp.where(
                left, acc[:BLOCK], acc[BLOCK:]).astype(o_ref.dtype)

    assert q_blocks >= STICK_FIRST_BLOCKS - 1
    pl.when(it == 0)(lambda: run(True))
    pl.when(it > 0)(lambda: run(False))


def _stick(proj, *, q_blocks=4):
    s = proj.shape[0]
    tq = q_blocks * BLOCK
    c = _stick_constant()
    qb_blk = QB_OFF // LANES
    kb_blk = KB_OFF // LANES
    vb_blk = VB_OFF // LANES
    return pl.pallas_call(
        functools.partial(_stick_kernel, q_blocks=q_blocks),
        grid=(PAIRS_B, s // tq),
        in_specs=[
            pl.BlockSpec((tq, LANES), lambda b, i: (i, qb_blk + b)),
            pl.BlockSpec((s, LANES), lambda b, i: (0, kb_blk + b)),
            pl.BlockSpec((s, LANES), lambda b, i: (0, vb_blk + b)),
            pl.BlockSpec(c.shape, lambda b, i: (0, 0)),
        ],
        out_specs=pl.BlockSpec((tq, LANES), lambda b, i: (i, b)),
        out_shape=jax.ShapeDtypeStruct((s, B_W), jnp.bfloat16),
        compiler_params=pltpu.CompilerParams(
            dimension_semantics=("arbitrary", "arbitrary"), vmem_limit_bytes=VMEM_LIMIT),
        name="stick",
    )(proj, proj, proj, c)


def _merge_kernel(*refs, n_chunks):
    oa_ref, ob_ref = refs[0], refs[1]
    ga_refs = refs[2:2 + n_chunks]
    gb_refs = refs[2 + n_chunks:2 + 2 * n_chunks]
    x_ref, wa_ref, wb_ref, wo_ref, o_ref, m_ref = refs[2 + 2 * n_chunks:]
    tn = o_ref.shape[1] // n_chunks
    for c in range(n_chunks):
        sl = slice(c * tn, (c + 1) * tn)
        ya = jnp.dot(oa_ref[...], wa_ref[:, sl], preferred_element_type=jnp.float32)
        yb = jnp.dot(ob_ref[...], wb_ref[:, sl], preferred_element_type=jnp.float32)
        ga = jax.nn.sigmoid(ga_refs[c][...].astype(jnp.float32))
        gb = jax.nn.sigmoid(gb_refs[c][...].astype(jnp.float32))
        m_ref[:, sl] = (ga * ya + gb * yb).astype(m_ref.dtype)
    for c in range(n_chunks):
        sl = slice(c * tn, (c + 1) * tn)
        o_ref[:, sl] = x_ref[:, sl] + jnp.dot(m_ref[...], wo_ref[:, sl],
                                              preferred_element_type=jnp.float32)


def _merge(oa, ob, proj, x, wa, wb, wo, *, tm=256, tn=512):
    s, d = x.shape
    n_chunks = d // tn
    const = lambda i: (0, 0)
    gate_specs = [pl.BlockSpec((tm, tn), functools.partial(lambda i, blk: (i, blk), blk=off // tn + c))
                  for off in (GA_OFF, GB_OFF) for c in range(n_chunks)]
    return pl.pallas_call(
        functools.partial(_merge_kernel, n_chunks=n_chunks),
        grid=(s // tm,),
        in_specs=[
            pl.BlockSpec((tm, A_Q_W), lambda i: (i, 0)),
            pl.BlockSpec((tm, B_W), lambda i: (i, 0)),
            *gate_specs,
            pl.BlockSpec((tm, d), lambda i: (i, 0)),
            pl.BlockSpec(wa.shape, const),
            pl.BlockSpec(wb.shape, const),
            pl.BlockSpec(wo.shape, const),
        ],
        out_specs=pl.BlockSpec((tm, d), lambda i: (i, 0)),
        out_shape=jax.ShapeDtypeStruct((s, d), jnp.float32),
        scratch_shapes=[pltpu.VMEM((tm, d), jnp.bfloat16)],
        compiler_params=pltpu.CompilerParams(
            dimension_semantics=("arbitrary",), vmem_limit_bytes=VMEM_LIMIT),
        name="merge",
    )(oa, ob, *([proj] * (2 * n_chunks)), x, wa, wb, wo)


def _ffn_kernel(x_ref, g_ref, wg_ref, wu_ref, wd_ref, gf_ref, o_ref, h_ref, *, row_chunk):
    f = pl.program_id(1)

    @pl.when(f == 0)
    def _():
        _rmsnorm_rows(x_ref, h_ref, g_ref[...], row_chunk)
        o_ref[...] = x_ref[...]

    h = h_ref[...]
    gate = jnp.dot(h, wg_ref[0], preferred_element_type=jnp.float32)
    up = jnp.dot(h, wu_ref[0], preferred_element_type=jnp.float32)
    act = (gate * jax.nn.sigmoid(gate) * up).astype(jnp.bfloat16)
    o_ref[...] += jnp.dot(act, wd_ref[...], preferred_element_type=jnp.float32)

    @pl.when(f == pl.num_programs(1) - 1)
    def _():
        _rmsnorm_rows(o_ref, o_ref, gf_ref[...], row_chunk)


def _ffn(x1, g, w_in_tiles, w_down, gf, *, tm=512):
    s, d = x1.shape
    _, _, tf = w_in_tiles.shape
    nf = D_FF // tf
    return pl.pallas_call(
        functools.partial(_ffn_kernel, row_chunk=128),
        grid=(s // tm, nf),
        in_specs=[
            pl.BlockSpec((tm, d), lambda i, f: (i, 0)),
            pl.BlockSpec((1, d), lambda i, f: (0, 0)),
            pl.BlockSpec((1, d, tf), lambda i, f: (f, 0, 0)),
            pl.BlockSpec((1, d, tf), lambda i, f: (f + nf, 0, 0)),
            pl.BlockSpec((tf, d), lambda i, f: (f, 0)),
            pl.BlockSpec((1, d), lambda i, f: (0, 0)),
        ],
        out_specs=pl.BlockSpec((tm, d), lambda i, f: (i, 0)),
        out_shape=jax.ShapeDtypeStruct((s, d), jnp.float32),
        scratch_shapes=[pltpu.VMEM((tm, d), jnp.bfloat16)],
        compiler_params=pltpu.CompilerParams(
            dimension_semantics=("arbitrary", "arbitrary"), vmem_limit_bytes=VMEM_LIMIT),
        name="ffn",
    )(x1, g, w_in_tiles, w_in_tiles, w_down, gf)


def kernel(x, norm_mix_g, w_in, sink_logits, w_branch_a, w_branch_b, w_out,
           norm_ffn_g, w_ffn_in, w_ffn_down, norm_final_g):
    b, s, d = x.shape
    assert (b, s, d) == (1, SEQ, D_MODEL) and w_in.shape[0] == 1
    bf16 = jnp.bfloat16
    x2 = x.reshape(s, d)
    proj = _in_proj(x2, norm_mix_g[0].reshape(1, d), _col_tiles(w_in[0].astype(bf16), 512))
    oa = _swa(proj, sink_logits[0])
    ob = _stick(proj)
    x1 = _merge(oa, ob, proj, x2, w_branch_a[0].astype(bf16), w_branch_b[0].astype(bf16),
                w_out[0].astype(bf16))
    out = _ffn(x1, norm_ffn_g[0].reshape(1, d), _col_tiles(w_ffn_in[0].astype(bf16), 512),
               w_ffn_down[0].astype(bf16), norm_final_g.reshape(1, d))
    return out.reshape(b, s, d)
```

```python
import functools
import math

import numpy as np
import jax
import jax.numpy as jnp
from jax import lax
from jax.experimental import pallas as pl
from jax.experimental.pallas import tpu as pltpu

D_MODEL = 2048
SEQ = 8192
HEAD_DIM = 64
A_Q_HEADS = 16
A_KV_HEADS = 4
A_GROUP = A_Q_HEADS // A_KV_HEADS
WINDOW = 128
B_HEADS = 16
BLOCK = 128
D_FF = 5632
EPS = 1e-6
SCALE = 1.0 / math.sqrt(HEAD_DIM)

A_Q_W = A_Q_HEADS * HEAD_DIM
A_KV_W = A_KV_HEADS * HEAD_DIM
B_W = B_HEADS * HEAD_DIM
IN_WIDTH = A_Q_W + 2 * A_KV_W + 3 * B_W + 2 * D_MODEL

LANES = 128
BF16_ROWS = 16
PAIRS_B = B_HEADS // 2

QA_OFF = 0
KA_OFF = QA_OFF + A_Q_W
VA_OFF = KA_OFF + A_KV_W
QB_OFF = VA_OFF + A_KV_W
KB_OFF = QB_OFF + B_W
VB_OFF = KB_OFF + B_W
GA_OFF = VB_OFF + B_W
GB_OFF = GA_OFF + D_MODEL

F32_EXP_ZERO = -104.0

VMEM_LIMIT = 56 * 1024 * 1024

_NT = (((1,), (1,)), ((), ()))


def _rmsnorm_rows(src_ref, dst_ref, g, row_chunk):
    def body(c, _):
        r = pl.multiple_of(c * row_chunk, row_chunk)
        x = src_ref[pl.ds(r, row_chunk), :]
        inv = lax.rsqrt(jnp.mean(x * x, axis=-1, keepdims=True) + EPS)
        dst_ref[pl.ds(r, row_chunk), :] = (x * inv * g).astype(dst_ref.dtype)
        return 0

    lax.fori_loop(0, src_ref.shape[0] // row_chunk, body, 0)


def _in_proj_kernel(x_ref, g_ref, w_ref, o_ref, h_ref, *, row_chunk):
    @pl.when(pl.program_id(1) == 0)
    def _():
        _rmsnorm_rows(x_ref, h_ref, g_ref[...], row_chunk)

    o_ref[...] = jnp.dot(h_ref[...], w_ref[...].astype(jnp.bfloat16),
                         preferred_element_type=jnp.float32).astype(o_ref.dtype)


def _in_proj(x, g, w, *, tm=2048, tn=512):
    s, d = x.shape
    n = w.shape[1]
    return pl.pallas_call(
        functools.partial(_in_proj_kernel, row_chunk=128),
        grid=(s // tm, n // tn),
        in_specs=[
            pl.BlockSpec((tm, d), lambda i, j: (i, 0), pipeline_mode=pl.Buffered(1)),
            pl.BlockSpec((1, d), lambda i, j: (0, 0)),
            pl.BlockSpec((d, tn), lambda i, j: (0, j)),
        ],
        out_specs=pl.BlockSpec((tm, tn), lambda i, j: (i, j)),
        out_shape=jax.ShapeDtypeStruct((s, n), jnp.bfloat16),
        scratch_shapes=[pltpu.VMEM((tm, d), jnp.bfloat16)],
        compiler_params=pltpu.CompilerParams(
            dimension_semantics=("arbitrary", "arbitrary"),
            vmem_limit_bytes=VMEM_LIMIT),
        name="in_proj",
    )(x, g, w)


def _half_swap_matrix(n):
    c = np.arange(n)
    p = np.zeros((n, n), np.float32)
    p[c, c ^ HEAD_DIM] = 1.0
    return jnp.asarray(p, jnp.bfloat16)


def _swa_kernel(sink_ref, q_ref, kp_ref, kc_ref, vp_ref, vc_ref, p_ref, o_ref):
    i = pl.program_id(0)
    k = jnp.concatenate([kp_ref[...], kc_ref[...]], axis=0)
    v = jnp.concatenate([vp_ref[...], vc_ref[...]], axis=0)
    k_sw = jnp.dot(k, p_ref[...], preferred_element_type=jnp.float32).astype(k.dtype)
    v_sw = jnp.dot(v, p_ref[...], preferred_element_type=jnp.float32).astype(v.dtype)

    qi = lax.broadcasted_iota(jnp.int32, (BLOCK, 2 * BLOCK), 0)
    ki = lax.broadcasted_iota(jnp.int32, (BLOCK, 2 * BLOCK), 1)
    dist = BLOCK + qi - ki
    valid = (dist >= 0) & (dist < WINDOW) & ((ki >= BLOCK) | (i > 0))
    distf = dist.astype(jnp.float32)
    lane = lax.broadcasted_iota(jnp.int32, (BLOCK, LANES), 1)
    left = lane < HEAD_DIM

    for j in range(A_KV_HEADS):
        tile = slice((j // 2) * LANES, (j // 2 + 1) * LANES)
        qa = q_ref[:, (2 * j) * LANES:(2 * j + 1) * LANES] * jnp.bfloat16(SCALE)
        qb = q_ref[:, (2 * j + 1) * LANES:(2 * j + 2) * LANES] * jnp.bfloat16(SCALE)
        zero = jnp.zeros_like(qa)
        outs = []
        for side in range(2):
            keep = left if side == 0 else jnp.logical_not(left)
            aligned = (j % 2) == side
            kh = (k if aligned else k_sw)[:, tile]
            vh = (v if aligned else v_sw)[:, tile]
            lhs = jnp.concatenate([jnp.where(keep, qa, zero), jnp.where(keep, qb, zero)], axis=0)
            s2 = lax.dot_general(lhs, kh, _NT, preferred_element_type=jnp.float32)
            ps, inv_den = [], []
            for t in range(2):
                h = A_GROUP * j + 2 * t + side
                slope = 2.0 ** (-8.0 * (h + 1) / A_Q_HEADS)
                s = s2[t * BLOCK:(t + 1) * BLOCK] - slope * distf
                s = jnp.where(valid, s, -jnp.inf)
                sink = sink_ref[h]
                m = jnp.maximum(jnp.max(s, axis=-1, keepdims=True), sink)
                p = jnp.exp(s - m)
                den = jnp.sum(p, axis=-1, keepdims=True) + jnp.exp(sink - m)
                ps.append(p.astype(jnp.bfloat16))
                inv_den.append(1.0 / den)
            o2 = jnp.dot(jnp.concatenate(ps, axis=0), vh, preferred_element_type=jnp.float32)
            outs.append((o2[:BLOCK] * inv_den[0], o2[BLOCK:] * inv_den[1]))
        for t in range(2):
            out = jnp.where(left, outs[0][t], outs[1][t])
            o_ref[:, (2 * j + t) * LANES:(2 * j + t + 1) * LANES] = out.astype(o_ref.dtype)


def _swa(proj, sinks):
    s = proj.shape[0]
    nb = s // BLOCK
    qa_blk = QA_OFF // A_Q_W
    ka_blk = KA_OFF // A_KV_W
    va_blk = VA_OFF // A_KV_W
    prev = lambda i: jnp.maximum(i - 1, 0)
    swap = _half_swap_matrix(A_KV_W)
    return pl.pallas_call(
        _swa_kernel,
        grid=(nb,),
        in_specs=[
            pl.BlockSpec(memory_space=pltpu.SMEM),
            pl.BlockSpec((BLOCK, A_Q_W), lambda i: (i, qa_blk)),
            pl.BlockSpec((BLOCK, A_KV_W), lambda i: (prev(i), ka_blk)),
            pl.BlockSpec((BLOCK, A_KV_W), lambda i: (i, ka_blk)),
            pl.BlockSpec((BLOCK, A_KV_W), lambda i: (prev(i), va_blk)),
            pl.BlockSpec((BLOCK, A_KV_W), lambda i: (i, va_blk)),
            pl.BlockSpec(swap.shape, lambda i: (0, 0)),
        ],
        out_specs=pl.BlockSpec((BLOCK, A_Q_W), lambda i: (i, 0)),
        out_shape=jax.ShapeDtypeStruct((s, A_Q_W), jnp.bfloat16),
        compiler_params=pltpu.CompilerParams(
            dimension_semantics=("arbitrary",), vmem_limit_bytes=VMEM_LIMIT),
        name="swa",
    )(sinks, proj, proj, proj, proj, proj, swap)


STICK_FIRST_BLOCKS = 3
NEG_LOG2E = -1.0 / math.log(2.0)


def _stick_constant():
    j = np.arange(BLOCK)[:, None]
    s = np.arange(BLOCK)[None, :]
    m = np.concatenate([(j > s).astype(np.float32), np.ones((BLOCK, LANES), np.float32)], axis=1)
    return jnp.asarray(np.concatenate([m, m], axis=0), jnp.bfloat16)


def _split_bf16(x):
    hi = x.astype(jnp.bfloat16)
    lo = (x - hi.astype(jnp.float32)).astype(jnp.bfloat16)
    return jnp.concatenate([hi, lo], axis=1)


def _log_beta(z):
    return jnp.minimum(z, 0.0) - jnp.log(1.0 + jnp.exp2(jnp.abs(z) * NEG_LOG2E))


def _stick_scores(lhs, k_ref, blk):
    first = jnp.maximum(blk - (STICK_FIRST_BLOCKS - 1), 0)
    start = pl.multiple_of(first * BLOCK, BLOCK)
    k0 = k_ref[pl.ds(start, STICK_FIRST_BLOCKS * BLOCK), :]
    return lax.dot_general(lhs, k0, _NT, preferred_element_type=jnp.float32), first


def _stick_sums(z, c, blk, first, mask_all):
    nk = STICK_FIRST_BLOCKS
    row = lax.broadcasted_iota(jnp.int32, (2 * BLOCK, BLOCK), 0) & (BLOCK - 1)
    col = lax.broadcasted_iota(jnp.int32, (2 * BLOCK, BLOCK), 1)
    out = []
    for n in range(nk):
        zn = z[:, n * BLOCK:(n + 1) * BLOCK]
        lb = _log_beta(zn)
        lf = lb - zn
        if mask_all:
            m = (first + n - blk) * BLOCK + col < row
        elif n == nk - 1:
            m = col < row
        else:
            m = None
        if m is not None:
            lf = jnp.where(m, lf, 0.0)
        out.append((lb, m, jnp.dot(_split_bf16(lf), c, preferred_element_type=jnp.float32)))
    return out


def _stick_values(terms, v_ref, first):
    start = pl.multiple_of(first * BLOCK, BLOCK)
    v0 = v_ref[pl.ds(start, STICK_FIRST_BLOCKS * BLOCK), :]
    tail = None
    a = [None] * len(terms)
    for n in reversed(range(len(terms))):
        lb, m, sums = terms[n]
        e = lb + sums[:, :BLOCK]
        if tail is not None:
            e = e + tail
        an = jnp.exp(e)
        if m is not None:
            an = jnp.where(m, an, 0.0)
        a[n] = an.astype(jnp.bfloat16)
        tail = sums[:, BLOCK:] if tail is None else tail + sums[:, BLOCK:]
    acc = jnp.dot(jnp.concatenate(a, axis=1), v0, preferred_element_type=jnp.float32)
    return acc, tail


def _stick_rest(lhs, k_ref, v_ref, c, acc, tail, first):
    def live(t):
        return (jnp.max(t) > F32_EXP_ZERO).astype(jnp.int32)

    def cond(carry):
        kb, alive, _, _ = carry
        return jnp.logical_and(kb >= 0, alive > 0)

    def body(carry):
        kb, _, acc, tail = carry
        st = pl.multiple_of(kb * BLOCK, BLOCK)
        z = lax.dot_general(lhs, k_ref[pl.ds(st, BLOCK), :], _NT,
                            preferred_element_type=jnp.float32)
        lb = _log_beta(z)
        sums = jnp.dot(_split_bf16(lb - z), c, preferred_element_type=jnp.float32)
        a = jnp.exp(lb + sums[:, :BLOCK] + tail)
        acc = acc + jnp.dot(a.astype(jnp.bfloat16), v_ref[pl.ds(st, BLOCK), :],
                            preferred_element_type=jnp.float32)
        tail = tail + sums[:, BLOCK:]
        return kb - 1, live(tail), acc, tail

    _, _, acc, _ = lax.while_loop(cond, body, (first - 1, live(tail), acc, tail))
    return acc


def _stick_kernel(*refs, q_blocks, cast_periods):
    n_cast = len(cast_periods)
    q_ref, k_ref, v_ref, c_ref = refs[:4]
    w_refs = refs[4:4 + n_cast]
    o_ref = refs[4 + n_cast]
    wo_refs = refs[5 + n_cast:]
    it = pl.program_id(1)
    step = pl.program_id(0) * pl.num_programs(1) + it
    for w_ref, wo_ref, period in zip(w_refs, wo_refs, cast_periods):
        @pl.when(step % period == 0)
        def _(w_ref=w_ref, wo_ref=wo_ref):
            wo_ref[...] = w_ref[...].astype(wo_ref.dtype)

    lane = lax.broadcasted_iota(jnp.int32, (BLOCK, LANES), 1)
    left = lane < HEAD_DIM
    c = c_ref[...]

    def run(mask_all):
        blks = [it * q_blocks + g for g in range(q_blocks)]
        lhss = []
        for g in range(q_blocks):
            q = q_ref[g * BLOCK:(g + 1) * BLOCK, :] * jnp.bfloat16(SCALE)
            zero = jnp.zeros_like(q)
            lhss.append(jnp.concatenate([jnp.where(left, q, zero), jnp.where(left, zero, q)], axis=0))
        scores = [_stick_scores(lhss[g], k_ref, blks[g]) for g in range(q_blocks)]
        terms = [_stick_sums(scores[g][0], c, blks[g], scores[g][1], mask_all)
                 for g in range(q_blocks)]
        vals = [_stick_values(terms[g], v_ref, scores[g][1]) for g in range(q_blocks)]
        for g in range(q_blocks):
            acc = _stick_rest(lhss[g], k_ref, v_ref, c, vals[g][0], vals[g][1], scores[g][1])
            o_ref[g * BLOCK:(g + 1) * BLOCK, :] = jnp.where(
                left, acc[:BLOCK], acc[BLOCK:]).astype(o_ref.dtype)

    assert q_blocks >= STICK_FIRST_BLOCKS - 1
    pl.when(it == 0)(lambda: run(True))
    pl.when(it > 0)(lambda: run(False))


def _cast_period(rows, steps):
    for period in range(1, steps + 1):
        if steps % period == 0 and rows % (steps // period) == 0 \
                and (rows // (steps // period)) % BF16_ROWS == 0:
            return period
    raise ValueError(f"cannot split {rows} rows over {steps} steps")


def _stick(proj, weights, *, q_blocks=4):
    s = proj.shape[0]
    tq = q_blocks * BLOCK
    n_q = s // tq
    steps = PAIRS_B * n_q
    c = _stick_constant()
    qb_blk = QB_OFF // LANES
    kb_blk = KB_OFF // LANES
    vb_blk = VB_OFF // LANES
    periods = tuple(_cast_period(w.shape[0], steps) for w in weights)
    w_specs = [
        pl.BlockSpec((w.shape[0] * p // steps, w.shape[1]),
                     functools.partial(lambda b, i, p: ((b * n_q + i) // p, 0), p=p))
        for w, p in zip(weights, periods)]
    outs = pl.pallas_call(
        functools.partial(_stick_kernel, q_blocks=q_blocks, cast_periods=periods),
        grid=(PAIRS_B, n_q),
        in_specs=[
            pl.BlockSpec((tq, LANES), lambda b, i: (i, qb_blk + b)),
            pl.BlockSpec((s, LANES), lambda b, i: (0, kb_blk + b)),
            pl.BlockSpec((s, LANES), lambda b, i: (0, vb_blk + b)),
            pl.BlockSpec(c.shape, lambda b, i: (0, 0)),
            *w_specs,
        ],
        out_specs=[pl.BlockSpec((tq, LANES), lambda b, i: (i, b)), *w_specs],
        out_shape=[jax.ShapeDtypeStruct((s, B_W), jnp.bfloat16)]
        + [jax.ShapeDtypeStruct(w.shape, jnp.bfloat16) for w in weights],
        compiler_params=pltpu.CompilerParams(
            dimension_semantics=("arbitrary", "arbitrary"), vmem_limit_bytes=VMEM_LIMIT),
        name="stick",
    )(proj, proj, proj, c, *weights)
    return outs[0], outs[1:]


def _merge_kernel(*refs, n_chunks):
    oa_ref, ob_ref = refs[0], refs[1]
    ga_refs = refs[2:2 + n_chunks]
    gb_refs = refs[2 + n_chunks:2 + 2 * n_chunks]
    x_ref, wa_ref, wb_ref, wo_ref, o_ref, m_ref = refs[2 + 2 * n_chunks:]
    tn = o_ref.shape[1] // n_chunks
    for c in range(n_chunks):
        sl = slice(c * tn, (c + 1) * tn)
        ya = jnp.dot(oa_ref[...], wa_ref[:, sl], preferred_element_type=jnp.float32)
        yb = jnp.dot(ob_ref[...], wb_ref[:, sl], preferred_element_type=jnp.float32)
        ga = jax.nn.sigmoid(ga_refs[c][...].astype(jnp.float32))
        gb = jax.nn.sigmoid(gb_refs[c][...].astype(jnp.float32))
        m_ref[:, sl] = (ga * ya + gb * yb).astype(m_ref.dtype)
    for c in range(n_chunks):
        sl = slice(c * tn, (c + 1) * tn)
        o_ref[:, sl] = x_ref[:, sl] + jnp.dot(m_ref[...], wo_ref[:, sl],
                                              preferred_element_type=jnp.float32)


def _merge(oa, ob, proj, x, wa, wb, wo, *, tm=256, tn=512):
    s, d = x.shape
    n_chunks = d // tn
    const = lambda i: (0, 0)
    gate_specs = [pl.BlockSpec((tm, tn), functools.partial(lambda i, blk: (i, blk), blk=off // tn + c))
                  for off in (GA_OFF, GB_OFF) for c in range(n_chunks)]
    return pl.pallas_call(
        functools.partial(_merge_kernel, n_chunks=n_chunks),
        grid=(s // tm,),
        in_specs=[
            pl.BlockSpec((tm, A_Q_W), lambda i: (i, 0)),
            pl.BlockSpec((tm, B_W), lambda i: (i, 0)),
            *gate_specs,
            pl.BlockSpec((tm, d), lambda i: (i, 0)),
            pl.BlockSpec(wa.shape, const),
            pl.BlockSpec(wb.shape, const),
            pl.BlockSpec(wo.shape, const),
        ],
        out_specs=pl.BlockSpec((tm, d), lambda i: (i, 0)),
        out_shape=jax.ShapeDtypeStruct((s, d), jnp.float32),
        scratch_shapes=[pltpu.VMEM((tm, d), jnp.bfloat16)],
        compiler_params=pltpu.CompilerParams(
            dimension_semantics=("arbitrary",), vmem_limit_bytes=VMEM_LIMIT),
        name="merge",
    )(oa, ob, *([proj] * (2 * n_chunks)), x, wa, wb, wo)


def _ffn_kernel(x_ref, g_ref, wg_ref, wu_ref, wd_ref, gf_ref, o_ref, h_ref, *, row_chunk):
    f = pl.program_id(1)

    @pl.when(f == 0)
    def _():
        _rmsnorm_rows(x_ref, h_ref, g_ref[...], row_chunk)
        o_ref[...] = x_ref[...]

    h = h_ref[...]
    gate = jnp.dot(h, wg_ref[...], preferred_element_type=jnp.float32)
    up = jnp.dot(h, wu_ref[...], preferred_element_type=jnp.float32)
    act = (gate * jax.nn.sigmoid(gate) * up).astype(jnp.bfloat16)
    o_ref[...] += jnp.dot(act, wd_ref[...], preferred_element_type=jnp.float32)

    @pl.when(f == pl.num_programs(1) - 1)
    def _():
        _rmsnorm_rows(o_ref, o_ref, gf_ref[...], row_chunk)


def _ffn(x1, g, w_in, w_down, gf, *, tm=1024, tf=512):
    s, d = x1.shape
    nf = D_FF // tf
    return pl.pallas_call(
        functools.partial(_ffn_kernel, row_chunk=128),
        grid=(s // tm, nf),
        in_specs=[
            pl.BlockSpec((tm, d), lambda i, f: (i, 0), pipeline_mode=pl.Buffered(1)),
            pl.BlockSpec((1, d), lambda i, f: (0, 0)),
            pl.BlockSpec((d, tf), lambda i, f: (0, f)),
            pl.BlockSpec((d, tf), lambda i, f: (0, f + nf)),
            pl.BlockSpec((tf, d), lambda i, f: (f, 0)),
            pl.BlockSpec((1, d), lambda i, f: (0, 0)),
        ],
        out_specs=pl.BlockSpec((tm, d), lambda i, f: (i, 0)),
        out_shape=jax.ShapeDtypeStruct((s, d), jnp.float32),
        scratch_shapes=[pltpu.VMEM((tm, d), jnp.bfloat16)],
        compiler_params=pltpu.CompilerParams(
            dimension_semantics=("arbitrary", "arbitrary"), vmem_limit_bytes=VMEM_LIMIT),
        name="ffn",
    )(x1, g, w_in, w_in, w_down, gf)


def kernel(x, norm_mix_g, w_in, sink_logits, w_branch_a, w_branch_b, w_out,
           norm_ffn_g, w_ffn_in, w_ffn_down, norm_final_g):
    b, s, d = x.shape
    assert (b, s, d) == (1, SEQ, D_MODEL) and w_in.shape[0] == 1
    x2 = x.reshape(s, d)
    proj = _in_proj(x2, norm_mix_g[0].reshape(1, d), w_in[0])
    oa = _swa(proj, sink_logits[0])
    f32_weights = [w_branch_a[0], w_branch_b[0], w_out[0], w_ffn_in[0], w_ffn_down[0]]
    ob, (wa, wb, wo, wfi, wfd) = _stick(proj, f32_weights)
    x1 = _merge(oa, ob, proj, x2, wa, wb, wo)
    out = _ffn(x1, norm_ffn_g[0].reshape(1, d), wfi, wfd, norm_final_g.reshape(1, d))
    return out.reshape(b, s, d)
```

```python
import functools
import math

import numpy as np
import jax
import jax.numpy as jnp
from jax import lax
from jax.experimental import pallas as pl
from jax.experimental.pallas import tpu as pltpu

D_MODEL = 2048
SEQ = 8192
HEAD_DIM = 64
A_Q_HEADS = 16
A_KV_HEADS = 4
A_GROUP = A_Q_HEADS // A_KV_HEADS
WINDOW = 128
B_HEADS = 16
BLOCK = 128
D_FF = 5632
EPS = 1e-6
SCALE = 1.0 / math.sqrt(HEAD_DIM)

A_Q_W = A_Q_HEADS * HEAD_DIM
A_KV_W = A_KV_HEADS * HEAD_DIM
B_W = B_HEADS * HEAD_DIM
IN_WIDTH = A_Q_W + 2 * A_KV_W + 3 * B_W + 2 * D_MODEL

LANES = 128
BF16_ROWS = 16
PAIRS_B = B_HEADS // 2

QA_OFF = 0
KA_OFF = QA_OFF + A_Q_W
VA_OFF = KA_OFF + A_KV_W
QB_OFF = VA_OFF + A_KV_W
KB_OFF = QB_OFF + B_W
VB_OFF = KB_OFF + B_W
GA_OFF = VB_OFF + B_W
GB_OFF = GA_OFF + D_MODEL

F32_EXP_ZERO = -104.0

VMEM_LIMIT = 56 * 1024 * 1024

_NT = (((1,), (1,)), ((), ()))


def _rmsnorm_rows(src_ref, dst_ref, g, row_chunk):
    def body(c, _):
        r = pl.multiple_of(c * row_chunk, row_chunk)
        x = src_ref[pl.ds(r, row_chunk), :]
        inv = lax.rsqrt(jnp.mean(x * x, axis=-1, keepdims=True) + EPS)
        dst_ref[pl.ds(r, row_chunk), :] = (x * inv * g).astype(dst_ref.dtype)
        return 0

    lax.fori_loop(0, src_ref.shape[0] // row_chunk, body, 0)


def _consume_row_block(x_hbm, xbuf, sem, consume):
    i = pl.program_id(0)
    rows = xbuf.shape[0]

    def copy(blk):
        return pltpu.make_async_copy(x_hbm.at[pl.ds(blk * rows, rows), :], xbuf, sem)

    @pl.when(pl.program_id(1) == 0)
    def _():
        @pl.when(i == 0)
        def _():
            copy(0).start()

        copy(i).wait()
        consume()

        @pl.when(i + 1 < pl.num_programs(0))
        def _():
            copy(i + 1).start()


def _in_proj_kernel(x_hbm, g_ref, w_ref, o_ref, xbuf, sem, h_ref, *, row_chunk):
    _consume_row_block(x_hbm, xbuf, sem,
                       lambda: _rmsnorm_rows(xbuf, h_ref, g_ref[...], row_chunk))
    o_ref[...] = jnp.dot(h_ref[...], w_ref[...].astype(jnp.bfloat16),
                         preferred_element_type=jnp.float32).astype(o_ref.dtype)


def _in_proj(x, g, w, *, tm=2048, tn=512):
    s, d = x.shape
    n = w.shape[1]
    return pl.pallas_call(
        functools.partial(_in_proj_kernel, row_chunk=128),
        grid=(s // tm, n // tn),
        in_specs=[
            pl.BlockSpec(memory_space=pl.ANY),
            pl.BlockSpec((1, d), lambda i, j: (0, 0)),
            pl.BlockSpec((d, tn), lambda i, j: (0, j)),
        ],
        out_specs=pl.BlockSpec((tm, tn), lambda i, j: (i, j)),
        out_shape=jax.ShapeDtypeStruct((s, n), jnp.bfloat16),
        scratch_shapes=[pltpu.VMEM((tm, d), x.dtype), pltpu.SemaphoreType.DMA(()),
                        pltpu.VMEM((tm, d), jnp.bfloat16)],
        compiler_params=pltpu.CompilerParams(
            dimension_semantics=("arbitrary", "arbitrary"),
            vmem_limit_bytes=VMEM_LIMIT),
        name="in_proj",
    )(x, g, w)


def _half_swap_matrix(n):
    c = np.arange(n)
    p = np.zeros((n, n), np.float32)
    p[c, c ^ HEAD_DIM] = 1.0
    return jnp.asarray(p, jnp.bfloat16)


def _swa_kernel(sink_ref, q_ref, kp_ref, kc_ref, vp_ref, vc_ref, p_ref, o_ref):
    i = pl.program_id(0)
    k = jnp.concatenate([kp_ref[...], kc_ref[...]], axis=0)
    v = jnp.concatenate([vp_ref[...], vc_ref[...]], axis=0)
    k_sw = jnp.dot(k, p_ref[...], preferred_element_type=jnp.float32).astype(k.dtype)
    v_sw = jnp.dot(v, p_ref[...], preferred_element_type=jnp.float32).astype(v.dtype)

    qi = lax.broadcasted_iota(jnp.int32, (BLOCK, 2 * BLOCK), 0)
    ki = lax.broadcasted_iota(jnp.int32, (BLOCK, 2 * BLOCK), 1)
    dist = BLOCK + qi - ki
    valid = (dist >= 0) & (dist < WINDOW) & ((ki >= BLOCK) | (i > 0))
    distf = dist.astype(jnp.float32)
    lane = lax.broadcasted_iota(jnp.int32, (BLOCK, LANES), 1)
    left = lane < HEAD_DIM

    for j in range(A_KV_HEADS):
        tile = slice((j // 2) * LANES, (j // 2 + 1) * LANES)
        qa = q_ref[:, (2 * j) * LANES:(2 * j + 1) * LANES] * jnp.bfloat16(SCALE)
        qb = q_ref[:, (2 * j + 1) * LANES:(2 * j + 2) * LANES] * jnp.bfloat16(SCALE)
        zero = jnp.zeros_like(qa)
        outs = []
        for side in range(2):
            keep = left if side == 0 else jnp.logical_not(left)
            aligned = (j % 2) == side
            kh = (k if aligned else k_sw)[:, tile]
            vh = (v if aligned else v_sw)[:, tile]
            lhs = jnp.concatenate([jnp.where(keep, qa, zero), jnp.where(keep, qb, zero)], axis=0)
            s2 = lax.dot_general(lhs, kh, _NT, preferred_element_type=jnp.float32)
            ps, inv_den = [], []
            for t in range(2):
                h = A_GROUP * j + 2 * t + side
                slope = 2.0 ** (-8.0 * (h + 1) / A_Q_HEADS)
                s = s2[t * BLOCK:(t + 1) * BLOCK] - slope * distf
                s = jnp.where(valid, s, -jnp.inf)
                sink = sink_ref[h]
                m = jnp.maximum(jnp.max(s, axis=-1, keepdims=True), sink)
                p = jnp.exp(s - m)
                den = jnp.sum(p, axis=-1, keepdims=True) + jnp.exp(sink - m)
                ps.append(p.astype(jnp.bfloat16))
                inv_den.append(1.0 / den)
            o2 = jnp.dot(jnp.concatenate(ps, axis=0), vh, preferred_element_type=jnp.float32)
            outs.append((o2[:BLOCK] * inv_den[0], o2[BLOCK:] * inv_den[1]))
        for t in range(2):
            out = jnp.where(left, outs[0][t], outs[1][t])
            o_ref[:, (2 * j + t) * LANES:(2 * j + t + 1) * LANES] = out.astype(o_ref.dtype)


def _swa(proj, sinks):
    s = proj.shape[0]
    nb = s // BLOCK
    qa_blk = QA_OFF // A_Q_W
    ka_blk = KA_OFF // A_KV_W
    va_blk = VA_OFF // A_KV_W
    prev = lambda i: jnp.maximum(i - 1, 0)
    swap = _half_swap_matrix(A_KV_W)
    return pl.pallas_call(
        _swa_kernel,
        grid=(nb,),
        in_specs=[
            pl.BlockSpec(memory_space=pltpu.SMEM),
            pl.BlockSpec((BLOCK, A_Q_W), lambda i: (i, qa_blk)),
            pl.BlockSpec((BLOCK, A_KV_W), lambda i: (prev(i), ka_blk)),
            pl.BlockSpec((BLOCK, A_KV_W), lambda i: (i, ka_blk)),
            pl.BlockSpec((BLOCK, A_KV_W), lambda i: (prev(i), va_blk)),
            pl.BlockSpec((BLOCK, A_KV_W), lambda i: (i, va_blk)),
            pl.BlockSpec(swap.shape, lambda i: (0, 0)),
        ],
        out_specs=pl.BlockSpec((BLOCK, A_Q_W), lambda i: (i, 0)),
        out_shape=jax.ShapeDtypeStruct((s, A_Q_W), jnp.bfloat16),
        compiler_params=pltpu.CompilerParams(
            dimension_semantics=("arbitrary",), vmem_limit_bytes=VMEM_LIMIT),
        name="swa",
    )(sinks, proj, proj, proj, proj, proj, swap)


STICK_FIRST_BLOCKS = 3
NEG_LOG2E = -1.0 / math.log(2.0)
MASKED = 1e30


def _stick_constant():
    j = np.arange(BLOCK)[:, None]
    s = np.arange(BLOCK)[None, :]
    m = np.concatenate([(j >= s).astype(np.float32), np.ones((BLOCK, LANES), np.float32)], axis=1)
    return jnp.asarray(np.concatenate([m, m], axis=0), jnp.bfloat16)


def _split_bf16(x):
    hi = x.astype(jnp.bfloat16)
    lo = (x - hi.astype(jnp.float32)).astype(jnp.bfloat16)
    return jnp.concatenate([hi, lo], axis=1)


def _softplus(z):
    return jnp.maximum(z, 0.0) + jnp.log(1.0 + jnp.exp2(jnp.abs(z) * NEG_LOG2E))


def _stick_scores(lhs, k_ref, blk):
    first = jnp.maximum(blk - (STICK_FIRST_BLOCKS - 1), 0)
    start = pl.multiple_of(first * BLOCK, BLOCK)
    k0 = k_ref[pl.ds(start, STICK_FIRST_BLOCKS * BLOCK), :]
    return lax.dot_general(lhs, k0, _NT, preferred_element_type=jnp.float32), first


def _stick_sums(z, c, blk, first, mask_all):
    nk = STICK_FIRST_BLOCKS
    row = lax.broadcasted_iota(jnp.int32, (2 * BLOCK, BLOCK), 0) & (BLOCK - 1)
    col = lax.broadcasted_iota(jnp.int32, (2 * BLOCK, BLOCK), 1)
    out = []
    for n in range(nk):
        zn = z[:, n * BLOCK:(n + 1) * BLOCK]
        if mask_all:
            m = (first + n - blk) * BLOCK + col < row
        elif n == nk - 1:
            m = col < row
        else:
            m = None
        if m is not None:
            zn = jnp.where(m, zn, -MASKED)
        out.append((zn, jnp.dot(_split_bf16(_softplus(zn)), c,
                                preferred_element_type=jnp.float32)))
    return out


def _stick_values(terms, v_ref, first):
    start = pl.multiple_of(first * BLOCK, BLOCK)
    v0 = v_ref[pl.ds(start, STICK_FIRST_BLOCKS * BLOCK), :]
    tail = None
    a = [None] * len(terms)
    for n in reversed(range(len(terms))):
        zn, sums = terms[n]
        e = zn - sums[:, :BLOCK]
        if tail is not None:
            e = e - tail
        a[n] = jnp.exp(e).astype(jnp.bfloat16)
        tail = sums[:, BLOCK:] if tail is None else tail + sums[:, BLOCK:]
    acc = jnp.dot(jnp.concatenate(a, axis=1), v0, preferred_element_type=jnp.float32)
    return acc, tail


def _stick_rest(lhs, k_ref, v_ref, c, acc, tail, first):
    def live(t):
        return (jnp.min(t) < -F32_EXP_ZERO).astype(jnp.int32)

    def cond(carry):
        kb, alive, _, _ = carry
        return jnp.logical_and(kb >= 0, alive > 0)

    def body(carry):
        kb, _, acc, tail = carry
        st = pl.multiple_of(kb * BLOCK, BLOCK)
        z = lax.dot_general(lhs, k_ref[pl.ds(st, BLOCK), :], _NT,
                            preferred_element_type=jnp.float32)
        sums = jnp.dot(_split_bf16(_softplus(z)), c, preferred_element_type=jnp.float32)
        a = jnp.exp(z - sums[:, :BLOCK] - tail)
        acc = acc + jnp.dot(a.astype(jnp.bfloat16), v_ref[pl.ds(st, BLOCK), :],
                            preferred_element_type=jnp.float32)
        tail = tail + sums[:, BLOCK:]
        return kb - 1, live(tail), acc, tail

    _, _, acc, _ = lax.while_loop(cond, body, (first - 1, live(tail), acc, tail))
    return acc


def _stick_kernel(*refs, q_blocks, cast_periods):
    n_cast = len(cast_periods)
    q_ref, k_ref, v_ref, c_ref = refs[:4]
    w_refs = refs[4:4 + n_cast]
    o_ref = refs[4 + n_cast]
    wo_refs = refs[5 + n_cast:]
    it = pl.program_id(1)
    step = pl.program_id(0) * pl.num_programs(1) + it
    for w_ref, wo_ref, period in zip(w_refs, wo_refs, cast_periods):
        @pl.when(step % period == 0)
        def _(w_ref=w_ref, wo_ref=wo_ref):
            wo_ref[...] = w_ref[...].astype(wo_ref.dtype)

    lane = lax.broadcasted_iota(jnp.int32, (BLOCK, LANES), 1)
    left = lane < HEAD_DIM
    c = c_ref[...]

    def run(mask_all):
        blks = [it * q_blocks + g for g in range(q_blocks)]
        lhss = []
        for g in range(q_blocks):
            q = q_ref[g * BLOCK:(g + 1) * BLOCK, :] * jnp.bfloat16(SCALE)
            zero = jnp.zeros_like(q)
            lhss.append(jnp.concatenate([jnp.where(left, q, zero), jnp.where(left, zero, q)], axis=0))
        scores = [_stick_scores(lhss[g], k_ref, blks[g]) for g in range(q_blocks)]
        terms = [_stick_sums(scores[g][0], c, blks[g], scores[g][1], mask_all)
                 for g in range(q_blocks)]
        vals = [_stick_values(terms[g], v_ref, scores[g][1]) for g in range(q_blocks)]
        for g in range(q_blocks):
            acc = _stick_rest(lhss[g], k_ref, v_ref, c, vals[g][0], vals[g][1], scores[g][1])
            o_ref[g * BLOCK:(g + 1) * BLOCK, :] = jnp.where(
                left, acc[:BLOCK], acc[BLOCK:]).astype(o_ref.dtype)

    assert q_blocks >= STICK_FIRST_BLOCKS - 1
    pl.when(it == 0)(lambda: run(True))
    pl.when(it > 0)(lambda: run(False))


def _cast_period(rows, steps):
    for period in range(1, steps + 1):
        if steps % period == 0 and rows % (steps // period) == 0 \
                and (rows // (steps // period)) % BF16_ROWS == 0:
            return period
    raise ValueError(f"cannot split {rows} rows over {steps} steps")


def _stick(proj, weights, *, q_blocks=4):
    s = proj.shape[0]
    tq = q_blocks * BLOCK
    n_q = s // tq
    steps = PAIRS_B * n_q
    c = _stick_constant()
    qb_blk = QB_OFF // LANES
    kb_blk = KB_OFF // LANES
    vb_blk = VB_OFF // LANES
    periods = tuple(_cast_period(w.shape[0], steps) for w in weights)
    w_specs = [
        pl.BlockSpec((w.shape[0] * p // steps, w.shape[1]),
                     functools.partial(lambda b, i, p: ((b * n_q + i) // p, 0), p=p))
        for w, p in zip(weights, periods)]
    outs = pl.pallas_call(
        functools.partial(_stick_kernel, q_blocks=q_blocks, cast_periods=periods),
        grid=(PAIRS_B, n_q),
        in_specs=[
            pl.BlockSpec((tq, LANES), lambda b, i: (i, qb_blk + b)),
            pl.BlockSpec((s, LANES), lambda b, i: (0, kb_blk + b)),
            pl.BlockSpec((s, LANES), lambda b, i: (0, vb_blk + b)),
            pl.BlockSpec(c.shape, lambda b, i: (0, 0)),
            *w_specs,
        ],
        out_specs=[pl.BlockSpec((tq, LANES), lambda b, i: (i, b)), *w_specs],
        out_shape=[jax.ShapeDtypeStruct((s, B_W), jnp.bfloat16)]
        + [jax.ShapeDtypeStruct(w.shape, jnp.bfloat16) for w in weights],
        compiler_params=pltpu.CompilerParams(
            dimension_semantics=("arbitrary", "arbitrary"), vmem_limit_bytes=VMEM_LIMIT),
        name="stick",
    )(proj, proj, proj, c, *weights)
    return outs[0], outs[1:]


def _merge_kernel(*refs, n_chunks):
    oa_ref, ob_ref = refs[0], refs[1]
    ga_refs = refs[2:2 + n_chunks]
    gb_refs = refs[2 + n_chunks:2 + 2 * n_chunks]
    x_ref, wa_ref, wb_ref, wo_ref, o_ref, m_ref = refs[2 + 2 * n_chunks:]
    tn = o_ref.shape[1] // n_chunks
    for c in range(n_chunks):
        sl = slice(c * tn, (c + 1) * tn)
        ya = jnp.dot(oa_ref[...], wa_ref[:, sl], preferred_element_type=jnp.float32)
        yb = jnp.dot(ob_ref[...], wb_ref[:, sl], preferred_element_type=jnp.float32)
        ga = jax.nn.sigmoid(ga_refs[c][...].astype(jnp.float32))
        gb = jax.nn.sigmoid(gb_refs[c][...].astype(jnp.float32))
        m_ref[:, sl] = (ga * ya + gb * yb).astype(m_ref.dtype)
    for c in range(n_chunks):
        sl = slice(c * tn, (c + 1) * tn)
        o_ref[:, sl] = x_ref[:, sl] + jnp.dot(m_ref[...], wo_ref[:, sl],
                                              preferred_element_type=jnp.float32)


def _merge(oa, ob, proj, x, wa, wb, wo, *, tm=256, tn=512):
    s, d = x.shape
    n_chunks = d // tn
    const = lambda i: (0, 0)
    gate_specs = [pl.BlockSpec((tm, tn), functools.partial(lambda i, blk: (i, blk), blk=off // tn + c))
                  for off in (GA_OFF, GB_OFF) for c in range(n_chunks)]
    return pl.pallas_call(
        functools.partial(_merge_kernel, n_chunks=n_chunks),
        grid=(s // tm,),
        in_specs=[
            pl.BlockSpec((tm, A_Q_W), lambda i: (i, 0)),
            pl.BlockSpec((tm, B_W), lambda i: (i, 0)),
            *gate_specs,
            pl.BlockSpec((tm, d), lambda i: (i, 0)),
            pl.BlockSpec(wa.shape, const),
            pl.BlockSpec(wb.shape, const),
            pl.BlockSpec(wo.shape, const),
        ],
        out_specs=pl.BlockSpec((tm, d), lambda i: (i, 0)),
        out_shape=jax.ShapeDtypeStruct((s, d), jnp.float32),
        scratch_shapes=[pltpu.VMEM((tm, d), jnp.bfloat16)],
        compiler_params=pltpu.CompilerParams(
            dimension_semantics=("arbitrary",), vmem_limit_bytes=VMEM_LIMIT),
        name="merge",
    )(oa, ob, *([proj] * (2 * n_chunks)), x, wa, wb, wo)


def _ffn_kernel(x_hbm, g_ref, wg_ref, wu_ref, wd_ref, gf_ref, o_ref, xbuf, sem, h_ref, *,
                row_chunk):
    f = pl.program_id(1)

    def start_row_block():
        _rmsnorm_rows(xbuf, h_ref, g_ref[...], row_chunk)
        o_ref[...] = xbuf[...]

    _consume_row_block(x_hbm, xbuf, sem, start_row_block)

    h = h_ref[...]
    gate = jnp.dot(h, wg_ref[...], preferred_element_type=jnp.float32)
    up = jnp.dot(h, wu_ref[...], preferred_element_type=jnp.float32)
    act = (gate * jax.nn.sigmoid(gate) * up).astype(jnp.bfloat16)
    o_ref[...] += jnp.dot(act, wd_ref[...], preferred_element_type=jnp.float32)

    @pl.when(f == pl.num_programs(1) - 1)
    def _():
        _rmsnorm_rows(o_ref, o_ref, gf_ref[...], row_chunk)


def _ffn(x1, g, w_in, w_down, gf, *, tm=1024, tf=512):
    s, d = x1.shape
    nf = D_FF // tf
    return pl.pallas_call(
        functools.partial(_ffn_kernel, row_chunk=128),
        grid=(s // tm, nf),
        in_specs=[
            pl.BlockSpec(memory_space=pl.ANY),
            pl.BlockSpec((1, d), lambda i, f: (0, 0)),
            pl.BlockSpec((d, tf), lambda i, f: (0, f)),
            pl.BlockSpec((d, tf), lambda i, f: (0, f + nf)),
            pl.BlockSpec((tf, d), lambda i, f: (f, 0)),
            pl.BlockSpec((1, d), lambda i, f: (0, 0)),
        ],
        out_specs=pl.BlockSpec((tm, d), lambda i, f: (i, 0)),
        out_shape=jax.ShapeDtypeStruct((s, d), jnp.float32),
        scratch_shapes=[pltpu.VMEM((tm, d), x1.dtype), pltpu.SemaphoreType.DMA(()),
                        pltpu.VMEM((tm, d), jnp.bfloat16)],
        compiler_params=pltpu.CompilerParams(
            dimension_semantics=("arbitrary", "arbitrary"), vmem_limit_bytes=VMEM_LIMIT),
        name="ffn",
    )(x1, g, w_in, w_in, w_down, gf)


def kernel(x, norm_mix_g, w_in, sink_logits, w_branch_a, w_branch_b, w_out,
           norm_ffn_g, w_ffn_in, w_ffn_down, norm_final_g):
    b, s, d = x.shape
    assert (b, s, d) == (1, SEQ, D_MODEL) and w_in.shape[0] == 1
    x2 = x.reshape(s, d)
    proj = _in_proj(x2, norm_mix_g[0].reshape(1, d), w_in[0])
    oa = _swa(proj, sink_logits[0])
    f32_weights = [w_branch_a[0], w_branch_b[0], w_out[0], w_ffn_in[0], w_ffn_down[0]]
    ob, (wa, wb, wo, wfi, wfd) = _stick(proj, f32_weights)
    x1 = _merge(oa, ob, proj, x2, wa, wb, wo)
    out = _ffn(x1, norm_ffn_g[0].reshape(1, d), wfi, wfd, norm_final_g.reshape(1, d))
    return out.reshape(b, s, d)
```

```python
import functools
import math

import numpy as np
import jax
import jax.numpy as jnp
from jax import lax
from jax.experimental import pallas as pl
from jax.experimental.pallas import tpu as pltpu

D_MODEL = 2048
SEQ = 8192
HEAD_DIM = 64
A_Q_HEADS = 16
A_KV_HEADS = 4
A_GROUP = A_Q_HEADS // A_KV_HEADS
WINDOW = 128
B_HEADS = 16
BLOCK = 128
D_FF = 5632
EPS = 1e-6
SCALE = 1.0 / math.sqrt(HEAD_DIM)

A_Q_W = A_Q_HEADS * HEAD_DIM
A_KV_W = A_KV_HEADS * HEAD_DIM
B_W = B_HEADS * HEAD_DIM
IN_WIDTH = A_Q_W + 2 * A_KV_W + 3 * B_W + 2 * D_MODEL

LANES = 128
BF16_ROWS = 16
PAIRS_B = B_HEADS // 2

QA_OFF = 0
KA_OFF = QA_OFF + A_Q_W
VA_OFF = KA_OFF + A_KV_W
QB_OFF = VA_OFF + A_KV_W
KB_OFF = QB_OFF + B_W
VB_OFF = KB_OFF + B_W
GA_OFF = VB_OFF + B_W
GB_OFF = GA_OFF + D_MODEL

F32_EXP_ZERO = -104.0

VMEM_LIMIT = 56 * 1024 * 1024

_NT = (((1,), (1,)), ((), ()))


def _rmsnorm_rows(src_ref, dst_ref, g, row_chunk):
    def body(c, _):
        r = pl.multiple_of(c * row_chunk, row_chunk)
        x = src_ref[pl.ds(r, row_chunk), :]
        inv = lax.rsqrt(jnp.mean(x * x, axis=-1, keepdims=True) + EPS)
        dst_ref[pl.ds(r, row_chunk), :] = (x * inv * g).astype(dst_ref.dtype)
        return 0

    lax.fori_loop(0, src_ref.shape[0] // row_chunk, body, 0)


def _consume_row_block(x_hbm, xbuf, sem, consume):
    i = pl.program_id(0)
    rows = xbuf.shape[0]

    def copy(blk):
        return pltpu.make_async_copy(x_hbm.at[pl.ds(blk * rows, rows), :], xbuf, sem)

    @pl.when(pl.program_id(1) == 0)
    def _():
        @pl.when(i == 0)
        def _():
            copy(0).start()

        copy(i).wait()
        consume()

        @pl.when(i + 1 < pl.num_programs(0))
        def _():
            copy(i + 1).start()


def _in_proj_kernel(x_hbm, g_ref, w_ref, o_ref, xbuf, sem, h_ref, *, row_chunk):
    _consume_row_block(x_hbm, xbuf, sem,
                       lambda: _rmsnorm_rows(xbuf, h_ref, g_ref[...], row_chunk))
    o_ref[...] = jnp.dot(h_ref[...], w_ref[...].astype(jnp.bfloat16),
                         preferred_element_type=jnp.float32).astype(o_ref.dtype)


def _in_proj(x, g, w, *, tm=2048, tn=512):
    s, d = x.shape
    n = w.shape[1]
    return pl.pallas_call(
        functools.partial(_in_proj_kernel, row_chunk=128),
        grid=(s // tm, n // tn),
        in_specs=[
            pl.BlockSpec(memory_space=pl.ANY),
            pl.BlockSpec((1, d), lambda i, j: (0, 0)),
            pl.BlockSpec((d, tn), lambda i, j: (0, j)),
        ],
        out_specs=pl.BlockSpec((tm, tn), lambda i, j: (i, j)),
        out_shape=jax.ShapeDtypeStruct((s, n), jnp.bfloat16),
        scratch_shapes=[pltpu.VMEM((tm, d), x.dtype), pltpu.SemaphoreType.DMA(()),
                        pltpu.VMEM((tm, d), jnp.bfloat16)],
        compiler_params=pltpu.CompilerParams(
            dimension_semantics=("arbitrary", "arbitrary"),
            vmem_limit_bytes=VMEM_LIMIT),
        name="in_proj",
    )(x, g, w)


def _half_swap_matrix(n):
    c = np.arange(n)
    p = np.zeros((n, n), np.float32)
    p[c, c ^ HEAD_DIM] = 1.0
    return jnp.asarray(p, jnp.bfloat16)


def _swa_kernel(sink_ref, q_ref, kp_ref, kc_ref, vp_ref, vc_ref, p_ref, o_ref):
    i = pl.program_id(0)
    k = jnp.concatenate([kp_ref[...], kc_ref[...]], axis=0)
    v = jnp.concatenate([vp_ref[...], vc_ref[...]], axis=0)
    k_sw = jnp.dot(k, p_ref[...], preferred_element_type=jnp.float32).astype(k.dtype)
    v_sw = jnp.dot(v, p_ref[...], preferred_element_type=jnp.float32).astype(v.dtype)

    qi = lax.broadcasted_iota(jnp.int32, (BLOCK, 2 * BLOCK), 0)
    ki = lax.broadcasted_iota(jnp.int32, (BLOCK, 2 * BLOCK), 1)
    dist = BLOCK + qi - ki
    valid = (dist >= 0) & (dist < WINDOW) & ((ki >= BLOCK) | (i > 0))
    distf = dist.astype(jnp.float32)
    lane = lax.broadcasted_iota(jnp.int32, (BLOCK, LANES), 1)
    left = lane < HEAD_DIM

    for j in range(A_KV_HEADS):
        tile = slice((j // 2) * LANES, (j // 2 + 1) * LANES)
        qa = q_ref[:, (2 * j) * LANES:(2 * j + 1) * LANES] * jnp.bfloat16(SCALE)
        qb = q_ref[:, (2 * j + 1) * LANES:(2 * j + 2) * LANES] * jnp.bfloat16(SCALE)
        zero = jnp.zeros_like(qa)
        outs = []
        for side in range(2):
            keep = left if side == 0 else jnp.logical_not(left)
            aligned = (j % 2) == side
            kh = (k if aligned else k_sw)[:, tile]
            vh = (v if aligned else v_sw)[:, tile]
            lhs = jnp.concatenate([jnp.where(keep, qa, zero), jnp.where(keep, qb, zero)], axis=0)
            s2 = lax.dot_general(lhs, kh, _NT, preferred_element_type=jnp.float32)
            ps, inv_den = [], []
            for t in range(2):
                h = A_GROUP * j + 2 * t + side
                slope = 2.0 ** (-8.0 * (h + 1) / A_Q_HEADS)
                s = s2[t * BLOCK:(t + 1) * BLOCK] - slope * distf
                s = jnp.where(valid, s, -jnp.inf)
                sink = sink_ref[h]
                m = jnp.maximum(jnp.max(s, axis=-1, keepdims=True), sink)
                p = jnp.exp(s - m)
                den = jnp.sum(p, axis=-1, keepdims=True) + jnp.exp(sink - m)
                ps.append(p.astype(jnp.bfloat16))
                inv_den.append(1.0 / den)
            o2 = jnp.dot(jnp.concatenate(ps, axis=0), vh, preferred_element_type=jnp.float32)
            outs.append((o2[:BLOCK] * inv_den[0], o2[BLOCK:] * inv_den[1]))
        for t in range(2):
            out = jnp.where(left, outs[0][t], outs[1][t])
            o_ref[:, (2 * j + t) * LANES:(2 * j + t + 1) * LANES] = out.astype(o_ref.dtype)


def _swa(proj, sinks):
    s = proj.shape[0]
    nb = s // BLOCK
    qa_blk = QA_OFF // A_Q_W
    ka_blk = KA_OFF // A_KV_W
    va_blk = VA_OFF // A_KV_W
    prev = lambda i: jnp.maximum(i - 1, 0)
    swap = _half_swap_matrix(A_KV_W)
    return pl.pallas_call(
        _swa_kernel,
        grid=(nb,),
        in_specs=[
            pl.BlockSpec(memory_space=pltpu.SMEM),
            pl.BlockSpec((BLOCK, A_Q_W), lambda i: (i, qa_blk)),
            pl.BlockSpec((BLOCK, A_KV_W), lambda i: (prev(i), ka_blk)),
            pl.BlockSpec((BLOCK, A_KV_W), lambda i: (i, ka_blk)),
            pl.BlockSpec((BLOCK, A_KV_W), lambda i: (prev(i), va_blk)),
            pl.BlockSpec((BLOCK, A_KV_W), lambda i: (i, va_blk)),
            pl.BlockSpec(swap.shape, lambda i: (0, 0)),
        ],
        out_specs=pl.BlockSpec((BLOCK, A_Q_W), lambda i: (i, 0)),
        out_shape=jax.ShapeDtypeStruct((s, A_Q_W), jnp.bfloat16),
        compiler_params=pltpu.CompilerParams(
            dimension_semantics=("arbitrary",), vmem_limit_bytes=VMEM_LIMIT),
        name="swa",
    )(sinks, proj, proj, proj, proj, proj, swap)


STICK_FIRST_BLOCKS = 3
NEG_LOG2E = -1.0 / math.log(2.0)
MASKED = 1e30


def _stick_constant():
    j = np.arange(BLOCK)[:, None]
    s = np.arange(BLOCK)[None, :]
    m = np.concatenate([(j >= s).astype(np.float32), np.ones((BLOCK, LANES), np.float32)], axis=1)
    return jnp.asarray(np.concatenate([m, m], axis=0), jnp.bfloat16)


def _split_bf16(x):
    hi = x.astype(jnp.bfloat16)
    lo = (x - hi.astype(jnp.float32)).astype(jnp.bfloat16)
    return jnp.concatenate([hi, lo], axis=1)


def _softplus(z):
    return jnp.maximum(z, 0.0) + jnp.log(1.0 + jnp.exp2(jnp.abs(z) * NEG_LOG2E))


def _stick_scores(lhs, k_ref, blk):
    first = jnp.maximum(blk - (STICK_FIRST_BLOCKS - 1), 0)
    start = pl.multiple_of(first * BLOCK, BLOCK)
    k0 = k_ref[pl.ds(start, STICK_FIRST_BLOCKS * BLOCK), :]
    return lax.dot_general(lhs, k0, _NT, preferred_element_type=jnp.float32), first


def _stick_masked(z, blk, first, mask_all):
    nk = STICK_FIRST_BLOCKS
    row = lax.broadcasted_iota(jnp.int32, (2 * BLOCK, BLOCK), 0) & (BLOCK - 1)
    col = lax.broadcasted_iota(jnp.int32, (2 * BLOCK, BLOCK), 1)
    out = []
    for n in range(nk):
        zn = z[:, n * BLOCK:(n + 1) * BLOCK]
        if mask_all:
            m = (first + n - blk) * BLOCK + col < row
        elif n == nk - 1:
            m = col < row
        else:
            m = None
        if m is not None:
            zn = jnp.where(m, zn, -MASKED)
        out.append((zn, _split_bf16(_softplus(zn))))
    return out


def _stick_values(terms, v_ref, first):
    start = pl.multiple_of(first * BLOCK, BLOCK)
    v0 = v_ref[pl.ds(start, STICK_FIRST_BLOCKS * BLOCK), :]
    tail = None
    a = [None] * len(terms)
    for n in reversed(range(len(terms))):
        zn, sums = terms[n]
        e = zn - sums[:, :BLOCK]
        if tail is not None:
            e = e - tail
        a[n] = jnp.exp(e).astype(jnp.bfloat16)
        tail = sums[:, BLOCK:] if tail is None else tail + sums[:, BLOCK:]
    acc = jnp.dot(jnp.concatenate(a, axis=1), v0, preferred_element_type=jnp.float32)
    return acc, tail


def _stick_live(tail):
    return (jnp.min(tail) < -F32_EXP_ZERO).astype(jnp.int32)


def _stick_rest(lhs, k_ref, v_ref, c, acc, tail, first, live0):
    live = _stick_live

    def cond(carry):
        kb, alive, _, _ = carry
        return jnp.logical_and(kb >= 0, alive > 0)

    def body(carry):
        kb, _, acc, tail = carry
        st = pl.multiple_of(kb * BLOCK, BLOCK)
        z = lax.dot_general(lhs, k_ref[pl.ds(st, BLOCK), :], _NT,
                            preferred_element_type=jnp.float32)
        sums = jnp.dot(_split_bf16(_softplus(z)), c, preferred_element_type=jnp.float32)
        a = jnp.exp(z - sums[:, :BLOCK] - tail)
        acc = acc + jnp.dot(a.astype(jnp.bfloat16), v_ref[pl.ds(st, BLOCK), :],
                            preferred_element_type=jnp.float32)
        tail = tail + sums[:, BLOCK:]
        return kb - 1, live(tail), acc, tail

    _, _, acc, _ = lax.while_loop(cond, body, (first - 1, live0, acc, tail))
    return acc


def _stick_kernel(*refs, q_blocks, cast_periods):
    n_cast = len(cast_periods)
    q_ref, k_ref, v_ref, c_ref = refs[:4]
    w_refs = refs[4:4 + n_cast]
    o_ref = refs[4 + n_cast]
    wo_refs = refs[5 + n_cast:]
    it = pl.program_id(1)
    step = pl.program_id(0) * pl.num_programs(1) + it
    for w_ref, wo_ref, period in zip(w_refs, wo_refs, cast_periods):
        @pl.when(step % period == 0)
        def _(w_ref=w_ref, wo_ref=wo_ref):
            wo_ref[...] = w_ref[...].astype(wo_ref.dtype)

    lane = lax.broadcasted_iota(jnp.int32, (BLOCK, LANES), 1)
    left = lane < HEAD_DIM
    c = c_ref[...]

    def run(mask_all):
        blks = [it * q_blocks + g for g in range(q_blocks)]
        lhss = []
        for g in range(q_blocks):
            q = q_ref[g * BLOCK:(g + 1) * BLOCK, :] * jnp.bfloat16(SCALE)
            zero = jnp.zeros_like(q)
            lhss.append(jnp.concatenate([jnp.where(left, q, zero), jnp.where(left, zero, q)], axis=0))
        scores = [_stick_scores(lhss[g], k_ref, blks[g]) for g in range(q_blocks)]
        masked = [_stick_masked(scores[g][0], blks[g], scores[g][1], mask_all)
                  for g in range(q_blocks)]
        nk = STICK_FIRST_BLOCKS
        sums = jnp.dot(jnp.concatenate([sp for m in masked for _, sp in m], axis=0), c,
                       preferred_element_type=jnp.float32)
        rows = 2 * BLOCK
        terms = [[(masked[g][n][0], sums[(g * nk + n) * rows:(g * nk + n + 1) * rows])
                  for n in range(nk)] for g in range(q_blocks)]
        vals = [_stick_values(terms[g], v_ref, scores[g][1]) for g in range(q_blocks)]
        lives = [_stick_live(vals[g][1]) for g in range(q_blocks)]
        for g in range(q_blocks):
            acc = _stick_rest(lhss[g], k_ref, v_ref, c, vals[g][0], vals[g][1], scores[g][1],
                              lives[g])
            o_ref[g * BLOCK:(g + 1) * BLOCK, :] = jnp.where(
                left, acc[:BLOCK], acc[BLOCK:]).astype(o_ref.dtype)

    assert q_blocks >= STICK_FIRST_BLOCKS - 1
    pl.when(it == 0)(lambda: run(True))
    pl.when(it > 0)(lambda: run(False))


def _cast_period(rows, steps):
    for period in range(1, steps + 1):
        if steps % period == 0 and rows % (steps // period) == 0 \
                and (rows // (steps // period)) % BF16_ROWS == 0:
            return period
    raise ValueError(f"cannot split {rows} rows over {steps} steps")


def _stick(proj, weights, *, q_blocks=4):
    s = proj.shape[0]
    tq = q_blocks * BLOCK
    n_q = s // tq
    steps = PAIRS_B * n_q
    c = _stick_constant()
    qb_blk = QB_OFF // LANES
    kb_blk = KB_OFF // LANES
    vb_blk = VB_OFF // LANES
    periods = tuple(_cast_period(w.shape[0], steps) for w in weights)
    w_specs = [
        pl.BlockSpec((w.shape[0] * p // steps, w.shape[1]),
                     functools.partial(lambda b, i, p: ((b * n_q + i) // p, 0), p=p))
        for w, p in zip(weights, periods)]
    outs = pl.pallas_call(
        functools.partial(_stick_kernel, q_blocks=q_blocks, cast_periods=periods),
        grid=(PAIRS_B, n_q),
        in_specs=[
            pl.BlockSpec((tq, LANES), lambda b, i: (i, qb_blk + b)),
            pl.BlockSpec((s, LANES), lambda b, i: (0, kb_blk + b)),
            pl.BlockSpec((s, LANES), lambda b, i: (0, vb_blk + b)),
            pl.BlockSpec(c.shape, lambda b, i: (0, 0)),
            *w_specs,
        ],
        out_specs=[pl.BlockSpec((tq, LANES), lambda b, i: (i, b)), *w_specs],
        out_shape=[jax.ShapeDtypeStruct((s, B_W), jnp.bfloat16)]
        + [jax.ShapeDtypeStruct(w.shape, jnp.bfloat16) for w in weights],
        compiler_params=pltpu.CompilerParams(
            dimension_semantics=("arbitrary", "arbitrary"), vmem_limit_bytes=VMEM_LIMIT),
        name="stick",
    )(proj, proj, proj, c, *weights)
    return outs[0], outs[1:]


def _merge_kernel(*refs, n_chunks):
    oa_ref, ob_ref = refs[0], refs[1]
    ga_refs = refs[2:2 + n_chunks]
    gb_refs = refs[2 + n_chunks:2 + 2 * n_chunks]
    x_ref, wa_ref, wb_ref, wo_ref, o_ref, m_ref = refs[2 + 2 * n_chunks:]
    tn = o_ref.shape[1] // n_chunks
    for c in range(n_chunks):
        sl = slice(c * tn, (c + 1) * tn)
        ya = jnp.dot(oa_ref[...], wa_ref[:, sl], preferred_element_type=jnp.float32)
        yb = jnp.dot(ob_ref[...], wb_ref[:, sl], preferred_element_type=jnp.float32)
        ga = jax.nn.sigmoid(ga_refs[c][...].astype(jnp.float32))
        gb = jax.nn.sigmoid(gb_refs[c][...].astype(jnp.float32))
        m_ref[:, sl] = (ga * ya + gb * yb).astype(m_ref.dtype)
    for c in range(n_chunks):
        sl = slice(c * tn, (c + 1) * tn)
        o_ref[:, sl] = x_ref[:, sl] + jnp.dot(m_ref[...], wo_ref[:, sl],
                                              preferred_element_type=jnp.float32)


def _merge(oa, ob, proj, x, wa, wb, wo, *, tm=256, tn=512):
    s, d = x.shape
    n_chunks = d // tn
    const = lambda i: (0, 0)
    gate_specs = [pl.BlockSpec((tm, tn), functools.partial(lambda i, blk: (i, blk), blk=off // tn + c))
                  for off in (GA_OFF, GB_OFF) for c in range(n_chunks)]
    return pl.pallas_call(
        functools.partial(_merge_kernel, n_chunks=n_chunks),
        grid=(s // tm,),
        in_specs=[
            pl.BlockSpec((tm, A_Q_W), lambda i: (i, 0)),
            pl.BlockSpec((tm, B_W), lambda i: (i, 0)),
            *gate_specs,
            pl.BlockSpec((tm, d), lambda i: (i, 0)),
            pl.BlockSpec(wa.shape, const),
            pl.BlockSpec(wb.shape, const),
            pl.BlockSpec(wo.shape, const),
        ],
        out_specs=pl.BlockSpec((tm, d), lambda i: (i, 0)),
        out_shape=jax.ShapeDtypeStruct((s, d), jnp.float32),
        scratch_shapes=[pltpu.VMEM((tm, d), jnp.bfloat16)],
        compiler_params=pltpu.CompilerParams(
            dimension_semantics=("arbitrary",), vmem_limit_bytes=VMEM_LIMIT),
        name="merge",
    )(oa, ob, *([proj] * (2 * n_chunks)), x, wa, wb, wo)


def _ffn_kernel(x_hbm, g_ref, wg_ref, wu_ref, wd_ref, gf_ref, o_ref, xbuf, sem, h_ref, *,
                row_chunk):
    f = pl.program_id(1)

    def start_row_block():
        _rmsnorm_rows(xbuf, h_ref, g_ref[...], row_chunk)
        o_ref[...] = xbuf[...]

    _consume_row_block(x_hbm, xbuf, sem, start_row_block)

    h = h_ref[...]
    gate = jnp.dot(h, wg_ref[...], preferred_element_type=jnp.float32)
    up = jnp.dot(h, wu_ref[...], preferred_element_type=jnp.float32)
    act = (gate * jax.nn.sigmoid(gate) * up).astype(jnp.bfloat16)
    o_ref[...] += jnp.dot(act, wd_ref[...], preferred_element_type=jnp.float32)

    @pl.when(f == pl.num_programs(1) - 1)
    def _():
        _rmsnorm_rows(o_ref, o_ref, gf_ref[...], row_chunk)


def _ffn(x1, g, w_in, w_down, gf, *, tm=1024, tf=512):
    s, d = x1.shape
    nf = D_FF // tf
    return pl.pallas_call(
        functools.partial(_ffn_kernel, row_chunk=128),
        grid=(s // tm, nf),
        in_specs=[
            pl.BlockSpec(memory_space=pl.ANY),
            pl.BlockSpec((1, d), lambda i, f: (0, 0)),
            pl.BlockSpec((d, tf), lambda i, f: (0, f)),
            pl.BlockSpec((d, tf), lambda i, f: (0, f + nf)),
            pl.BlockSpec((tf, d), lambda i, f: (f, 0)),
            pl.BlockSpec((1, d), lambda i, f: (0, 0)),
        ],
        out_specs=pl.BlockSpec((tm, d), lambda i, f: (i, 0)),
        out_shape=jax.ShapeDtypeStruct((s, d), jnp.float32),
        scratch_shapes=[pltpu.VMEM((tm, d), x1.dtype), pltpu.SemaphoreType.DMA(()),
                        pltpu.VMEM((tm, d), jnp.bfloat16)],
        compiler_params=pltpu.CompilerParams(
            dimension_semantics=("arbitrary", "arbitrary"), vmem_limit_bytes=VMEM_LIMIT),
        name="ffn",
    )(x1, g, w_in, w_in, w_down, gf)


def kernel(x, norm_mix_g, w_in, sink_logits, w_branch_a, w_branch_b, w_out,
           norm_ffn_g, w_ffn_in, w_ffn_down, norm_final_g):
    b, s, d = x.shape
    assert (b, s, d) == (1, SEQ, D_MODEL) and w_in.shape[0] == 1
    x2 = x.reshape(s, d)
    proj = _in_proj(x2, norm_mix_g[0].reshape(1, d), w_in[0])
    oa = _swa(proj, sink_logits[0])
    f32_weights = [w_branch_a[0], w_branch_b[0], w_out[0], w_ffn_in[0], w_ffn_down[0]]
    ob, (wa, wb, wo, wfi, wfd) = _stick(proj, f32_weights)
    x1 = _merge(oa, ob, proj, x2, wa, wb, wo)
    out = _ffn(x1, norm_ffn_g[0].reshape(1, d), wfi, wfd, norm_final_g.reshape(1, d))
    return out.reshape(b, s, d)
```

```python
import functools
import math

import numpy as np
import jax
import jax.numpy as jnp
from jax import lax
from jax.experimental import pallas as pl
from jax.experimental.pallas import tpu as pltpu

D_MODEL = 2048
SEQ = 8192
HEAD_DIM = 64
A_Q_HEADS = 16
A_KV_HEADS = 4
A_GROUP = A_Q_HEADS // A_KV_HEADS
WINDOW = 128
B_HEADS = 16
BLOCK = 128
D_FF = 5632
EPS = 1e-6
SCALE = 1.0 / math.sqrt(HEAD_DIM)

A_Q_W = A_Q_HEADS * HEAD_DIM
A_KV_W = A_KV_HEADS * HEAD_DIM
B_W = B_HEADS * HEAD_DIM
IN_WIDTH = A_Q_W + 2 * A_KV_W + 3 * B_W + 2 * D_MODEL

LANES = 128
BF16_ROWS = 16
PAIRS_B = B_HEADS // 2

QA_OFF = 0
KA_OFF = QA_OFF + A_Q_W
VA_OFF = KA_OFF + A_KV_W
QB_OFF = VA_OFF + A_KV_W
KB_OFF = QB_OFF + B_W
VB_OFF = KB_OFF + B_W
GA_OFF = VB_OFF + B_W
GB_OFF = GA_OFF + D_MODEL

F32_EXP_ZERO = -104.0

VMEM_LIMIT = 56 * 1024 * 1024

_NT = (((1,), (1,)), ((), ()))


def _rmsnorm_rows(src_ref, dst_ref, g, row_chunk):
    def body(c, _):
        r = pl.multiple_of(c * row_chunk, row_chunk)
        x = src_ref[pl.ds(r, row_chunk), :]
        inv = lax.rsqrt(jnp.mean(x * x, axis=-1, keepdims=True) + EPS)
        dst_ref[pl.ds(r, row_chunk), :] = (x * inv * g).astype(dst_ref.dtype)
        return 0

    lax.fori_loop(0, src_ref.shape[0] // row_chunk, body, 0)


def _consume_row_block(x_hbm, xbuf, sem, consume):
    i = pl.program_id(0)
    rows = xbuf.shape[0]

    def copy(blk):
        return pltpu.make_async_copy(x_hbm.at[pl.ds(blk * rows, rows), :], xbuf, sem)

    @pl.when(pl.program_id(1) == 0)
    def _():
        @pl.when(i == 0)
        def _():
            copy(0).start()

        copy(i).wait()
        consume()

        @pl.when(i + 1 < pl.num_programs(0))
        def _():
            copy(i + 1).start()


def _in_proj_kernel(x_hbm, g_ref, w_ref, o_ref, xbuf, sem, h_ref, *, row_chunk):
    _consume_row_block(x_hbm, xbuf, sem,
                       lambda: _rmsnorm_rows(xbuf, h_ref, g_ref[...], row_chunk))
    o_ref[...] = jnp.dot(h_ref[...], w_ref[...].astype(jnp.bfloat16),
                         preferred_element_type=jnp.float32).astype(o_ref.dtype)


def _in_proj(x, g, w, *, tm=2048, tn=512):
    s, d = x.shape
    n = w.shape[1]
    return pl.pallas_call(
        functools.partial(_in_proj_kernel, row_chunk=128),
        grid=(s // tm, n // tn),
        in_specs=[
            pl.BlockSpec(memory_space=pl.ANY),
            pl.BlockSpec((1, d), lambda i, j: (0, 0)),
            pl.BlockSpec((d, tn), lambda i, j: (0, j)),
        ],
        out_specs=pl.BlockSpec((tm, tn), lambda i, j: (i, j)),
        out_shape=jax.ShapeDtypeStruct((s, n), jnp.bfloat16),
        scratch_shapes=[pltpu.VMEM((tm, d), x.dtype), pltpu.SemaphoreType.DMA(()),
                        pltpu.VMEM((tm, d), jnp.bfloat16)],
        compiler_params=pltpu.CompilerParams(
            dimension_semantics=("arbitrary", "arbitrary"),
            vmem_limit_bytes=VMEM_LIMIT),
        name="in_proj",
    )(x, g, w)


def _half_swap_matrix(n):
    c = np.arange(n)
    p = np.zeros((n, n), np.float32)
    p[c, c ^ HEAD_DIM] = 1.0
    return jnp.asarray(p, jnp.bfloat16)


def _swa_kernel(sink_ref, q_ref, kp_ref, kc_ref, vp_ref, vc_ref, p_ref, o_ref):
    i = pl.program_id(0)
    k = jnp.concatenate([kp_ref[...], kc_ref[...]], axis=0)
    v = jnp.concatenate([vp_ref[...], vc_ref[...]], axis=0)
    k_sw = jnp.dot(k, p_ref[...], preferred_element_type=jnp.float32).astype(k.dtype)
    v_sw = jnp.dot(v, p_ref[...], preferred_element_type=jnp.float32).astype(v.dtype)

    qi = lax.broadcasted_iota(jnp.int32, (BLOCK, 2 * BLOCK), 0)
    ki = lax.broadcasted_iota(jnp.int32, (BLOCK, 2 * BLOCK), 1)
    dist = BLOCK + qi - ki
    valid = (dist >= 0) & (dist < WINDOW) & ((ki >= BLOCK) | (i > 0))
    distf = dist.astype(jnp.float32)
    lane = lax.broadcasted_iota(jnp.int32, (BLOCK, LANES), 1)
    left = lane < HEAD_DIM

    for j in range(A_KV_HEADS):
        tile = slice((j // 2) * LANES, (j // 2 + 1) * LANES)
        qa = q_ref[:, (2 * j) * LANES:(2 * j + 1) * LANES] * jnp.bfloat16(SCALE)
        qb = q_ref[:, (2 * j + 1) * LANES:(2 * j + 2) * LANES] * jnp.bfloat16(SCALE)
        zero = jnp.zeros_like(qa)
        outs = []
        for side in range(2):
            keep = left if side == 0 else jnp.logical_not(left)
            aligned = (j % 2) == side
            kh = (k if aligned else k_sw)[:, tile]
            vh = (v if aligned else v_sw)[:, tile]
            lhs = jnp.concatenate([jnp.where(keep, qa, zero), jnp.where(keep, qb, zero)], axis=0)
            s2 = lax.dot_general(lhs, kh, _NT, preferred_element_type=jnp.float32)
            ps, inv_den = [], []
            for t in range(2):
                h = A_GROUP * j + 2 * t + side
                slope = 2.0 ** (-8.0 * (h + 1) / A_Q_HEADS)
                s = s2[t * BLOCK:(t + 1) * BLOCK] - slope * distf
                s = jnp.where(valid, s, -jnp.inf)
                sink = sink_ref[h]
                m = jnp.maximum(jnp.max(s, axis=-1, keepdims=True), sink)
                p = jnp.exp(s - m)
                den = jnp.sum(p, axis=-1, keepdims=True) + jnp.exp(sink - m)
                ps.append(p.astype(jnp.bfloat16))
                inv_den.append(1.0 / den)
            o2 = jnp.dot(jnp.concatenate(ps, axis=0), vh, preferred_element_type=jnp.float32)
            outs.append((o2[:BLOCK] * inv_den[0], o2[BLOCK:] * inv_den[1]))
        for t in range(2):
            out = jnp.where(left, outs[0][t], outs[1][t])
            o_ref[:, (2 * j + t) * LANES:(2 * j + t + 1) * LANES] = out.astype(o_ref.dtype)


def _swa(proj, sinks):
    s = proj.shape[0]
    nb = s // BLOCK
    qa_blk = QA_OFF // A_Q_W
    ka_blk = KA_OFF // A_KV_W
    va_blk = VA_OFF // A_KV_W
    prev = lambda i: jnp.maximum(i - 1, 0)
    swap = _half_swap_matrix(A_KV_W)
    return pl.pallas_call(
        _swa_kernel,
        grid=(nb,),
        in_specs=[
            pl.BlockSpec(memory_space=pltpu.SMEM),
            pl.BlockSpec((BLOCK, A_Q_W), lambda i: (i, qa_blk)),
            pl.BlockSpec((BLOCK, A_KV_W), lambda i: (prev(i), ka_blk)),
            pl.BlockSpec((BLOCK, A_KV_W), lambda i: (i, ka_blk)),
            pl.BlockSpec((BLOCK, A_KV_W), lambda i: (prev(i), va_blk)),
            pl.BlockSpec((BLOCK, A_KV_W), lambda i: (i, va_blk)),
            pl.BlockSpec(swap.shape, lambda i: (0, 0)),
        ],
        out_specs=pl.BlockSpec((BLOCK, A_Q_W), lambda i: (i, 0)),
        out_shape=jax.ShapeDtypeStruct((s, A_Q_W), jnp.bfloat16),
        compiler_params=pltpu.CompilerParams(
            dimension_semantics=("arbitrary",), vmem_limit_bytes=VMEM_LIMIT),
        name="swa",
    )(sinks, proj, proj, proj, proj, proj, swap)


STICK_FIRST_BLOCKS = 3
NEG_LOG2E = -1.0 / math.log(2.0)
MASKED = 1e30


def _stick_constant():
    j = np.arange(BLOCK)[:, None]
    s = np.arange(BLOCK)[None, :]
    m = np.concatenate([(j >= s).astype(np.float32), np.ones((BLOCK, LANES), np.float32)], axis=1)
    return jnp.asarray(np.concatenate([m, m], axis=0), jnp.bfloat16)


def _split_bf16(x):
    hi = x.astype(jnp.bfloat16)
    lo = (x - hi.astype(jnp.float32)).astype(jnp.bfloat16)
    return jnp.concatenate([hi, lo], axis=1)


def _softplus(z):
    return jnp.maximum(z, 0.0) + jnp.log(1.0 + jnp.exp2(jnp.abs(z) * NEG_LOG2E))


def _stick_scores_clamped(lhs, k_ref, blk):
    nk = STICK_FIRST_BLOCKS
    first = jnp.maximum(blk - (nk - 1), 0)
    start = pl.multiple_of(first * BLOCK, BLOCK)
    z = lax.dot_general(lhs, k_ref[pl.ds(start, nk * BLOCK), :], _NT,
                        preferred_element_type=jnp.float32)
    return [z[:, n * BLOCK:(n + 1) * BLOCK] for n in range(nk)], first


def _stick_masked(z, blk, first, mask_all):
    nk = STICK_FIRST_BLOCKS
    row = lax.broadcasted_iota(jnp.int32, (2 * BLOCK, BLOCK), 0) & (BLOCK - 1)
    col = lax.broadcasted_iota(jnp.int32, (2 * BLOCK, BLOCK), 1)
    out = []
    for n in range(nk):
        zn = z[n]
        if mask_all:
            m = (first + n - blk) * BLOCK + col < row
        elif n == nk - 1:
            m = col < row
        else:
            m = None
        if m is not None:
            zn = jnp.where(m, zn, -MASKED)
        out.append((zn, _split_bf16(_softplus(zn))))
    return out


def _stick_weights(terms):
    tail = None
    a = [None] * len(terms)
    for n in reversed(range(len(terms))):
        zn, sums = terms[n]
        e = zn - sums[:, :BLOCK]
        if tail is not None:
            e = e - tail
        a[n] = jnp.exp(e).astype(jnp.bfloat16)
        tail = sums[:, BLOCK:] if tail is None else tail + sums[:, BLOCK:]
    return a, tail


def _stick_values_clamped(a, v_ref, first):
    start = pl.multiple_of(first * BLOCK, BLOCK)
    v0 = v_ref[pl.ds(start, STICK_FIRST_BLOCKS * BLOCK), :]
    return jnp.dot(jnp.concatenate(a, axis=1), v0, preferred_element_type=jnp.float32)


def _stick_live(tail):
    return (jnp.min(tail) < -F32_EXP_ZERO).astype(jnp.int32)


def _stick_rest(lhs, k_ref, v_ref, c, acc, tail, first, live0):
    live = _stick_live

    def cond(carry):
        kb, alive, _, _ = carry
        return jnp.logical_and(kb >= 0, alive > 0)

    def body(carry):
        kb, _, acc, tail = carry
        st = pl.multiple_of(kb * BLOCK, BLOCK)
        z = lax.dot_general(lhs, k_ref[pl.ds(st, BLOCK), :], _NT,
                            preferred_element_type=jnp.float32)
        sums = jnp.dot(_split_bf16(_softplus(z)), c, preferred_element_type=jnp.float32)
        a = jnp.exp(z - sums[:, :BLOCK] - tail)
        acc = acc + jnp.dot(a.astype(jnp.bfloat16), v_ref[pl.ds(st, BLOCK), :],
                            preferred_element_type=jnp.float32)
        tail = tail + sums[:, BLOCK:]
        return kb - 1, live(tail), acc, tail

    _, _, acc, _ = lax.while_loop(cond, body, (first - 1, live0, acc, tail))
    return acc


def _stick_kernel(*refs, q_blocks, n_cast):
    q_ref, k_ref, v_ref, c_ref = refs[:4]
    w_refs = refs[4:4 + n_cast]
    o_ref = refs[4 + n_cast]
    wo_refs = refs[5 + n_cast:]
    it = pl.program_id(1)
    lane = lax.broadcasted_iota(jnp.int32, (BLOCK, LANES), 1)
    left = lane < HEAD_DIM
    c = c_ref[...]

    def run(first_step):
        for w_ref, wo_ref in zip(w_refs, wo_refs):
            wo_ref[...] = w_ref[...].astype(wo_ref.dtype)
        blks = [it * q_blocks + g for g in range(q_blocks)]
        lhss = []
        for g in range(q_blocks):
            q = q_ref[g * BLOCK:(g + 1) * BLOCK, :] * jnp.bfloat16(SCALE)
            zero = jnp.zeros_like(q)
            lhss.append(jnp.concatenate([jnp.where(left, q, zero), jnp.where(left, zero, q)], axis=0))
        nk = STICK_FIRST_BLOCKS
        clamped = [_stick_scores_clamped(lhss[g], k_ref, blks[g]) for g in range(q_blocks)]
        z = [zs for zs, _ in clamped]
        firsts = [first for _, first in clamped]
        masked = [_stick_masked(z[g], blks[g], firsts[g], first_step and g < nk - 1)
                  for g in range(q_blocks)]
        sums = jnp.dot(jnp.concatenate([sp for m in masked for _, sp in m], axis=0), c,
                       preferred_element_type=jnp.float32)
        rows = 2 * BLOCK
        terms = [[(masked[g][n][0], sums[(g * nk + n) * rows:(g * nk + n + 1) * rows])
                  for n in range(nk)] for g in range(q_blocks)]
        weights = [_stick_weights(terms[g]) for g in range(q_blocks)]
        accs = [_stick_values_clamped(weights[g][0], v_ref, firsts[g]) for g in range(q_blocks)]
        lives = [_stick_live(tail) for _, tail in weights]
        for g in range(q_blocks):
            acc = _stick_rest(lhss[g], k_ref, v_ref, c, accs[g], weights[g][1], firsts[g], lives[g])
            o_ref[g * BLOCK:(g + 1) * BLOCK, :] = jnp.where(
                left, acc[:BLOCK], acc[BLOCK:]).astype(o_ref.dtype)

    assert q_blocks >= STICK_FIRST_BLOCKS - 1
    pl.when(it == 0)(lambda: run(True))
    pl.when(it > 0)(lambda: run(False))


def _cast_block_spec(shape, steps, n_inner):
    rows, cols = shape
    for n_cb in range(1, steps + 1):
        n_rb = steps // n_cb
        if steps % n_cb == 0 and rows % n_rb == 0 and cols % n_cb == 0 \
                and (rows // n_rb) % BF16_ROWS == 0 and (cols // n_cb) % LANES == 0:
            return pl.BlockSpec(
                (rows // n_rb, cols // n_cb),
                lambda b, i: ((b * n_inner + i) // n_cb, (b * n_inner + i) % n_cb))
    raise ValueError(f"cannot tile {shape} into {steps} blocks")


def _stick(proj, weights, *, q_blocks=8):
    s = proj.shape[0]
    tq = q_blocks * BLOCK
    n_q = s // tq
    steps = PAIRS_B * n_q
    c = _stick_constant()
    qb_blk = QB_OFF // LANES
    kb_blk = KB_OFF // LANES
    vb_blk = VB_OFF // LANES
    w_specs = [_cast_block_spec(w.shape, steps, n_q) for w in weights]
    outs = pl.pallas_call(
        functools.partial(_stick_kernel, q_blocks=q_blocks, n_cast=len(weights)),
        grid=(PAIRS_B, n_q),
        in_specs=[
            pl.BlockSpec((tq, LANES), lambda b, i: (i, qb_blk + b)),
            pl.BlockSpec((s, LANES), lambda b, i: (0, kb_blk + b)),
            pl.BlockSpec((s, LANES), lambda b, i: (0, vb_blk + b)),
            pl.BlockSpec(c.shape, lambda b, i: (0, 0)),
            *w_specs,
        ],
        out_specs=[pl.BlockSpec((tq, LANES), lambda b, i: (i, b)), *w_specs],
        out_shape=[jax.ShapeDtypeStruct((s, B_W), jnp.bfloat16)]
        + [jax.ShapeDtypeStruct(w.shape, jnp.bfloat16) for w in weights],
        compiler_params=pltpu.CompilerParams(
            dimension_semantics=("arbitrary", "arbitrary"), vmem_limit_bytes=VMEM_LIMIT),
        name="stick",
    )(proj, proj, proj, c, *weights)
    return outs[0], outs[1:]


def _merge_kernel(*refs, n_chunks):
    oa_ref, ob_ref = refs[0], refs[1]
    ga_refs = refs[2:2 + n_chunks]
    gb_refs = refs[2 + n_chunks:2 + 2 * n_chunks]
    x_ref, wa_ref, wb_ref, wo_ref, o_ref, m_ref = refs[2 + 2 * n_chunks:]
    tn = o_ref.shape[1] // n_chunks
    for c in range(n_chunks):
        sl = slice(c * tn, (c + 1) * tn)
        ya = jnp.dot(oa_ref[...], wa_ref[:, sl], preferred_element_type=jnp.float32)
        yb = jnp.dot(ob_ref[...], wb_ref[:, sl], preferred_element_type=jnp.float32)
        ga = jax.nn.sigmoid(ga_refs[c][...].astype(jnp.float32))
        gb = jax.nn.sigmoid(gb_refs[c][...].astype(jnp.float32))
        m_ref[:, sl] = (ga * ya + gb * yb).astype(m_ref.dtype)
    for c in range(n_chunks):
        sl = slice(c * tn, (c + 1) * tn)
        o_ref[:, sl] = x_ref[:, sl] + jnp.dot(m_ref[...], wo_ref[:, sl],
                                              preferred_element_type=jnp.float32)


def _merge(oa, ob, proj, x, wa, wb, wo, *, tm=512, tn=512):
    s, d = x.shape
    n_chunks = d // tn
    const = lambda i: (0, 0)
    gate_specs = [pl.BlockSpec((tm, tn), functools.partial(lambda i, blk: (i, blk), blk=off // tn + c))
                  for off in (GA_OFF, GB_OFF) for c in range(n_chunks)]
    return pl.pallas_call(
        functools.partial(_merge_kernel, n_chunks=n_chunks),
        grid=(s // tm,),
        in_specs=[
            pl.BlockSpec((tm, A_Q_W), lambda i: (i, 0)),
            pl.BlockSpec((tm, B_W), lambda i: (i, 0)),
            *gate_specs,
            pl.BlockSpec((tm, d), lambda i: (i, 0)),
            pl.BlockSpec(wa.shape, const),
            pl.BlockSpec(wb.shape, const),
            pl.BlockSpec(wo.shape, const),
        ],
        out_specs=pl.BlockSpec((tm, d), lambda i: (i, 0)),
        out_shape=jax.ShapeDtypeStruct((s, d), jnp.float32),
        scratch_shapes=[pltpu.VMEM((tm, d), jnp.bfloat16)],
        compiler_params=pltpu.CompilerParams(
            dimension_semantics=("arbitrary",), vmem_limit_bytes=VMEM_LIMIT),
        name="merge",
    )(oa, ob, *([proj] * (2 * n_chunks)), x, wa, wb, wo)


def _ffn_kernel(x_hbm, g_ref, wg_ref, wu_ref, wd_ref, gf_ref, o_ref, xbuf, sem, h_ref, *,
                row_chunk):
    f = pl.program_id(1)

    def start_row_block():
        _rmsnorm_rows(xbuf, h_ref, g_ref[...], row_chunk)
        o_ref[...] = xbuf[...]

    _consume_row_block(x_hbm, xbuf, sem, start_row_block)

    h = h_ref[...]
    gate = jnp.dot(h, wg_ref[...], preferred_element_type=jnp.float32)
    up = jnp.dot(h, wu_ref[...], preferred_element_type=jnp.float32)
    act = (gate * jax.nn.sigmoid(gate) * up).astype(jnp.bfloat16)
    o_ref[...] += jnp.dot(act, wd_ref[...], preferred_element_type=jnp.float32)

    @pl.when(f == pl.num_programs(1) - 1)
    def _():
        _rmsnorm_rows(o_ref, o_ref, gf_ref[...], row_chunk)


def _ffn(x1, g, w_in, w_down, gf, *, tm=1024, tf=512):
    s, d = x1.shape
    nf = D_FF // tf
    return pl.pallas_call(
        functools.partial(_ffn_kernel, row_chunk=128),
        grid=(s // tm, nf),
        in_specs=[
            pl.BlockSpec(memory_space=pl.ANY),
            pl.BlockSpec((1, d), lambda i, f: (0, 0)),
            pl.BlockSpec((d, tf), lambda i, f: (0, f)),
            pl.BlockSpec((d, tf), lambda i, f: (0, f + nf)),
            pl.BlockSpec((tf, d), lambda i, f: (f, 0)),
            pl.BlockSpec((1, d), lambda i, f: (0, 0)),
        ],
        out_specs=pl.BlockSpec((tm, d), lambda i, f: (i, 0)),
        out_shape=jax.ShapeDtypeStruct((s, d), jnp.float32),
        scratch_shapes=[pltpu.VMEM((tm, d), x1.dtype), pltpu.SemaphoreType.DMA(()),
                        pltpu.VMEM((tm, d), jnp.bfloat16)],
        compiler_params=pltpu.CompilerParams(
            dimension_semantics=("arbitrary", "arbitrary"), vmem_limit_bytes=VMEM_LIMIT),
        name="ffn",
    )(x1, g, w_in, w_in, w_down, gf)


def kernel(x, norm_mix_g, w_in, sink_logits, w_branch_a, w_branch_b, w_out,
           norm_ffn_g, w_ffn_in, w_ffn_down, norm_final_g):
    b, s, d = x.shape
    assert (b, s, d) == (1, SEQ, D_MODEL) and w_in.shape[0] == 1
    x2 = x.reshape(s, d)
    proj = _in_proj(x2, norm_mix_g[0].reshape(1, d), w_in[0])
    oa = _swa(proj, sink_logits[0])
    f32_weights = [w_branch_a[0], w_branch_b[0], w_out[0], w_ffn_in[0], w_ffn_down[0]]
    ob, (wa, wb, wo, wfi, wfd) = _stick(proj, f32_weights)
    x1 = _merge(oa, ob, proj, x2, wa, wb, wo)
    out = _ffn(x1, norm_ffn_g[0].reshape(1, d), wfi, wfd, norm_final_g.reshape(1, d))
    return out.reshape(b, s, d)
```

```python
import functools
import math

import numpy as np
import jax
import jax.numpy as jnp
from jax import lax
from jax.experimental import pallas as pl
from jax.experimental.pallas import tpu as pltpu

D_MODEL = 2048
SEQ = 8192
HEAD_DIM = 64
A_Q_HEADS = 16
A_KV_HEADS = 4
A_GROUP = A_Q_HEADS // A_KV_HEADS
WINDOW = 128
B_HEADS = 16
BLOCK = 128
D_FF = 5632
EPS = 1e-6
SCALE = 1.0 / math.sqrt(HEAD_DIM)

A_Q_W = A_Q_HEADS * HEAD_DIM
A_KV_W = A_KV_HEADS * HEAD_DIM
B_W = B_HEADS * HEAD_DIM
IN_WIDTH = A_Q_W + 2 * A_KV_W + 3 * B_W + 2 * D_MODEL

LANES = 128
BF16_ROWS = 16
PAIRS_B = B_HEADS // 2

QA_OFF = 0
KA_OFF = QA_OFF + A_Q_W
VA_OFF = KA_OFF + A_KV_W
QB_OFF = VA_OFF + A_KV_W
KB_OFF = QB_OFF + B_W
VB_OFF = KB_OFF + B_W
GA_OFF = VB_OFF + B_W
GB_OFF = GA_OFF + D_MODEL

F32_EXP_ZERO = -104.0

VMEM_LIMIT = 56 * 1024 * 1024

_NT = (((1,), (1,)), ((), ()))


def _rmsnorm_rows(src_ref, dst_ref, g, row_chunk):
    def body(c, _):
        r = pl.multiple_of(c * row_chunk, row_chunk)
        x = src_ref[pl.ds(r, row_chunk), :]
        inv = lax.rsqrt(jnp.mean(x * x, axis=-1, keepdims=True) + EPS)
        dst_ref[pl.ds(r, row_chunk), :] = (x * inv * g).astype(dst_ref.dtype)
        return 0

    lax.fori_loop(0, src_ref.shape[0] // row_chunk, body, 0)


def _consume_row_block(x_hbm, xbuf, sem, consume):
    i = pl.program_id(0)
    rows = xbuf.shape[0]

    def copy(blk):
        return pltpu.make_async_copy(x_hbm.at[pl.ds(blk * rows, rows), :], xbuf, sem)

    @pl.when(pl.program_id(1) == 0)
    def _():
        @pl.when(i == 0)
        def _():
            copy(0).start()

        copy(i).wait()
        consume()

        @pl.when(i + 1 < pl.num_programs(0))
        def _():
            copy(i + 1).start()


def _in_proj_kernel(x_hbm, g_ref, w_ref, o_ref, xbuf, sem, h_ref, *, row_chunk):
    _consume_row_block(x_hbm, xbuf, sem,
                       lambda: _rmsnorm_rows(xbuf, h_ref, g_ref[...], row_chunk))
    o_ref[...] = jnp.dot(h_ref[...], w_ref[...].astype(jnp.bfloat16),
                         preferred_element_type=jnp.float32).astype(o_ref.dtype)


def _in_proj(x, g, w, *, tm=2048, tn=512):
    s, d = x.shape
    n = w.shape[1]
    return pl.pallas_call(
        functools.partial(_in_proj_kernel, row_chunk=128),
        grid=(s // tm, n // tn),
        in_specs=[
            pl.BlockSpec(memory_space=pl.ANY),
            pl.BlockSpec((1, d), lambda i, j: (0, 0)),
            pl.BlockSpec((d, tn), lambda i, j: (0, j)),
        ],
        out_specs=pl.BlockSpec((tm, tn), lambda i, j: (i, j)),
        out_shape=jax.ShapeDtypeStruct((s, n), jnp.bfloat16),
        scratch_shapes=[pltpu.VMEM((tm, d), x.dtype), pltpu.SemaphoreType.DMA(()),
                        pltpu.VMEM((tm, d), jnp.bfloat16)],
        compiler_params=pltpu.CompilerParams(
            dimension_semantics=("arbitrary", "arbitrary"),
            vmem_limit_bytes=VMEM_LIMIT),
        name="in_proj",
    )(x, g, w)


def _half_swap_matrix(n):
    c = np.arange(n)
    p = np.zeros((n, n), np.float32)
    p[c, c ^ HEAD_DIM] = 1.0
    return jnp.asarray(p, jnp.bfloat16)


def _swa_kernel(sink_ref, q_ref, kp_ref, kc_ref, vp_ref, vc_ref, p_ref, o_ref):
    i = pl.program_id(0)
    k = jnp.concatenate([kp_ref[...], kc_ref[...]], axis=0)
    v = jnp.concatenate([vp_ref[...], vc_ref[...]], axis=0)
    k_sw = jnp.dot(k, p_ref[...], preferred_element_type=jnp.float32).astype(k.dtype)
    v_sw = jnp.dot(v, p_ref[...], preferred_element_type=jnp.float32).astype(v.dtype)

    qi = lax.broadcasted_iota(jnp.int32, (BLOCK, 2 * BLOCK), 0)
    ki = lax.broadcasted_iota(jnp.int32, (BLOCK, 2 * BLOCK), 1)
    dist = BLOCK + qi - ki
    valid = (dist >= 0) & (dist < WINDOW) & ((ki >= BLOCK) | (i > 0))
    distf = dist.astype(jnp.float32)
    lane = lax.broadcasted_iota(jnp.int32, (BLOCK, LANES), 1)
    left = lane < HEAD_DIM

    for j in range(A_KV_HEADS):
        tile = slice((j // 2) * LANES, (j // 2 + 1) * LANES)
        qa = q_ref[:, (2 * j) * LANES:(2 * j + 1) * LANES] * jnp.bfloat16(SCALE)
        qb = q_ref[:, (2 * j + 1) * LANES:(2 * j + 2) * LANES] * jnp.bfloat16(SCALE)
        zero = jnp.zeros_like(qa)
        outs = []
        for side in range(2):
            keep = left if side == 0 else jnp.logical_not(left)
            aligned = (j % 2) == side
            kh = (k if aligned else k_sw)[:, tile]
            vh = (v if aligned else v_sw)[:, tile]
            lhs = jnp.concatenate([jnp.where(keep, qa, zero), jnp.where(keep, qb, zero)], axis=0)
            s2 = lax.dot_general(lhs, kh, _NT, preferred_element_type=jnp.float32)
            ps, inv_den = [], []
            for t in range(2):
                h = A_GROUP * j + 2 * t + side
                slope = 2.0 ** (-8.0 * (h + 1) / A_Q_HEADS)
                s = s2[t * BLOCK:(t + 1) * BLOCK] - slope * distf
                s = jnp.where(valid, s, -jnp.inf)
                sink = sink_ref[h]
                m = jnp.maximum(jnp.max(s, axis=-1, keepdims=True), sink)
                p = jnp.exp(s - m)
                den = jnp.sum(p, axis=-1, keepdims=True) + jnp.exp(sink - m)
                ps.append(p.astype(jnp.bfloat16))
                inv_den.append(1.0 / den)
            o2 = jnp.dot(jnp.concatenate(ps, axis=0), vh, preferred_element_type=jnp.float32)
            outs.append((o2[:BLOCK] * inv_den[0], o2[BLOCK:] * inv_den[1]))
        for t in range(2):
            out = jnp.where(left, outs[0][t], outs[1][t])
            o_ref[:, (2 * j + t) * LANES:(2 * j + t + 1) * LANES] = out.astype(o_ref.dtype)


def _swa(proj, sinks):
    s = proj.shape[0]
    nb = s // BLOCK
    qa_blk = QA_OFF // A_Q_W
    ka_blk = KA_OFF // A_KV_W
    va_blk = VA_OFF // A_KV_W
    prev = lambda i: jnp.maximum(i - 1, 0)
    swap = _half_swap_matrix(A_KV_W)
    return pl.pallas_call(
        _swa_kernel,
        grid=(nb,),
        in_specs=[
            pl.BlockSpec(memory_space=pltpu.SMEM),
            pl.BlockSpec((BLOCK, A_Q_W), lambda i: (i, qa_blk)),
            pl.BlockSpec((BLOCK, A_KV_W), lambda i: (prev(i), ka_blk)),
            pl.BlockSpec((BLOCK, A_KV_W), lambda i: (i, ka_blk)),
            pl.BlockSpec((BLOCK, A_KV_W), lambda i: (prev(i), va_blk)),
            pl.BlockSpec((BLOCK, A_KV_W), lambda i: (i, va_blk)),
            pl.BlockSpec(swap.shape, lambda i: (0, 0)),
        ],
        out_specs=pl.BlockSpec((BLOCK, A_Q_W), lambda i: (i, 0)),
        out_shape=jax.ShapeDtypeStruct((s, A_Q_W), jnp.bfloat16),
        compiler_params=pltpu.CompilerParams(
            dimension_semantics=("arbitrary",), vmem_limit_bytes=VMEM_LIMIT),
        name="swa",
    )(sinks, proj, proj, proj, proj, proj, swap)


STICK_FIRST_BLOCKS = 3
NEG_LOG2E = -1.0 / math.log(2.0)
MASKED = 1e30


def _stick_constant():
    j = np.arange(BLOCK)[:, None]
    s = np.arange(BLOCK)[None, :]
    m = np.concatenate([(j >= s).astype(np.float32), np.ones((BLOCK, LANES), np.float32)], axis=1)
    return jnp.asarray(m, jnp.bfloat16)


def _softplus(z):
    return jnp.maximum(z, 0.0) + jnp.log(1.0 + jnp.exp2(jnp.abs(z) * NEG_LOG2E))


def _stick_scores_clamped(lhs, k_ref, blk):
    nk = STICK_FIRST_BLOCKS
    first = jnp.maximum(blk - (nk - 1), 0)
    start = pl.multiple_of(first * BLOCK, BLOCK)
    z = lax.dot_general(lhs, k_ref[pl.ds(start, nk * BLOCK), :], _NT,
                        preferred_element_type=jnp.float32)
    return [z[:, n * BLOCK:(n + 1) * BLOCK] for n in range(nk)], first


def _stick_masked(z, blk, first, mask_all):
    nk = STICK_FIRST_BLOCKS
    row = lax.broadcasted_iota(jnp.int32, (2 * BLOCK, BLOCK), 0) & (BLOCK - 1)
    col = lax.broadcasted_iota(jnp.int32, (2 * BLOCK, BLOCK), 1)
    out = []
    for n in range(nk):
        zn = z[n]
        if mask_all:
            m = (first + n - blk) * BLOCK + col < row
        elif n == nk - 1:
            m = col < row
        else:
            m = None
        if m is not None:
            zn = jnp.where(m, zn, -MASKED)
        out.append((zn, _softplus(zn).astype(jnp.bfloat16)))
    return out


def _stick_weights(terms):
    tail = None
    a = [None] * len(terms)
    for n in reversed(range(len(terms))):
        zn, sums = terms[n]
        e = zn - sums[:, :BLOCK]
        if tail is not None:
            e = e - tail
        a[n] = jnp.exp(e).astype(jnp.bfloat16)
        tail = sums[:, BLOCK:] if tail is None else tail + sums[:, BLOCK:]
    return a, tail


def _stick_values_clamped(a, v_ref, first):
    start = pl.multiple_of(first * BLOCK, BLOCK)
    v0 = v_ref[pl.ds(start, STICK_FIRST_BLOCKS * BLOCK), :]
    return jnp.dot(jnp.concatenate(a, axis=1), v0, preferred_element_type=jnp.float32)


def _stick_live(tail):
    return (jnp.min(tail) < -F32_EXP_ZERO).astype(jnp.int32)


def _stick_rest(lhs, k_ref, v_ref, c, acc, tail, first, live0):
    live = _stick_live

    def cond(carry):
        kb, alive, _, _ = carry
        return jnp.logical_and(kb >= 0, alive > 0)

    def body(carry):
        kb, _, acc, tail = carry
        st = pl.multiple_of(kb * BLOCK, BLOCK)
        z = lax.dot_general(lhs, k_ref[pl.ds(st, BLOCK), :], _NT,
                            preferred_element_type=jnp.float32)
        sums = jnp.dot(_softplus(z).astype(jnp.bfloat16), c, preferred_element_type=jnp.float32)
        a = jnp.exp(z - sums[:, :BLOCK] - tail)
        acc = acc + jnp.dot(a.astype(jnp.bfloat16), v_ref[pl.ds(st, BLOCK), :],
                            preferred_element_type=jnp.float32)
        tail = tail + sums[:, BLOCK:]
        return kb - 1, live(tail), acc, tail

    _, _, acc, _ = lax.while_loop(cond, body, (first - 1, live0, acc, tail))
    return acc


def _stick_kernel(*refs, q_blocks, n_cast):
    q_ref, k_ref, v_ref, c_ref = refs[:4]
    w_refs = refs[4:4 + n_cast]
    o_ref = refs[4 + n_cast]
    wo_refs = refs[5 + n_cast:]
    it = pl.program_id(1)
    lane = lax.broadcasted_iota(jnp.int32, (BLOCK, LANES), 1)
    left = lane < HEAD_DIM
    c = c_ref[...]

    def run(first_step):
        for w_ref, wo_ref in zip(w_refs, wo_refs):
            wo_ref[...] = w_ref[...].astype(wo_ref.dtype)
        blks = [it * q_blocks + g for g in range(q_blocks)]
        lhss = []
        for g in range(q_blocks):
            q = q_ref[g * BLOCK:(g + 1) * BLOCK, :] * jnp.bfloat16(SCALE)
            zero = jnp.zeros_like(q)
            lhss.append(jnp.concatenate([jnp.where(left, q, zero), jnp.where(left, zero, q)], axis=0))
        nk = STICK_FIRST_BLOCKS
        clamped = [_stick_scores_clamped(lhss[g], k_ref, blks[g]) for g in range(q_blocks)]
        z = [zs for zs, _ in clamped]
        firsts = [first for _, first in clamped]
        masked = [_stick_masked(z[g], blks[g], firsts[g], first_step and g < nk - 1)
                  for g in range(q_blocks)]
        sums = jnp.dot(jnp.concatenate([sp for m in masked for _, sp in m], axis=0), c,
                       preferred_element_type=jnp.float32)
        rows = 2 * BLOCK
        terms = [[(masked[g][n][0], sums[(g * nk + n) * rows:(g * nk + n + 1) * rows])
                  for n in range(nk)] for g in range(q_blocks)]
        weights = [_stick_weights(terms[g]) for g in range(q_blocks)]
        accs = [_stick_values_clamped(weights[g][0], v_ref, firsts[g]) for g in range(q_blocks)]
        lives = [_stick_live(tail) for _, tail in weights]
        for g in range(q_blocks):
            acc = _stick_rest(lhss[g], k_ref, v_ref, c, accs[g], weights[g][1], firsts[g], lives[g])
            o_ref[g * BLOCK:(g + 1) * BLOCK, :] = jnp.where(
                left, acc[:BLOCK], acc[BLOCK:]).astype(o_ref.dtype)

    assert q_blocks >= STICK_FIRST_BLOCKS - 1
    pl.when(it == 0)(lambda: run(True))
    pl.when(it > 0)(lambda: run(False))


def _cast_block_spec(shape, steps, n_inner):
    rows, cols = shape
    for n_cb in range(1, steps + 1):
        n_rb = steps // n_cb
        if steps % n_cb == 0 and rows % n_rb == 0 and cols % n_cb == 0 \
                and (rows // n_rb) % BF16_ROWS == 0 and (cols // n_cb) % LANES == 0:
            return pl.BlockSpec(
                (rows // n_rb, cols // n_cb),
                lambda b, i: ((b * n_inner + i) // n_cb, (b * n_inner + i) % n_cb))
    raise ValueError(f"cannot tile {shape} into {steps} blocks")


def _stick(proj, weights, *, q_blocks=8):
    s = proj.shape[0]
    tq = q_blocks * BLOCK
    n_q = s // tq
    steps = PAIRS_B * n_q
    c = _stick_constant()
    qb_blk = QB_OFF // LANES
    kb_blk = KB_OFF // LANES
    vb_blk = VB_OFF // LANES
    w_specs = [_cast_block_spec(w.shape, steps, n_q) for w in weights]
    outs = pl.pallas_call(
        functools.partial(_stick_kernel, q_blocks=q_blocks, n_cast=len(weights)),
        grid=(PAIRS_B, n_q),
        in_specs=[
            pl.BlockSpec((tq, LANES), lambda b, i: (i, qb_blk + b)),
            pl.BlockSpec((s, LANES), lambda b, i: (0, kb_blk + b)),
            pl.BlockSpec((s, LANES), lambda b, i: (0, vb_blk + b)),
            pl.BlockSpec(c.shape, lambda b, i: (0, 0)),
            *w_specs,
        ],
        out_specs=[pl.BlockSpec((tq, LANES), lambda b, i: (i, b)), *w_specs],
        out_shape=[jax.ShapeDtypeStruct((s, B_W), jnp.bfloat16)]
        + [jax.ShapeDtypeStruct(w.shape, jnp.bfloat16) for w in weights],
        compiler_params=pltpu.CompilerParams(
            dimension_semantics=("arbitrary", "arbitrary"), vmem_limit_bytes=VMEM_LIMIT),
        name="stick",
    )(proj, proj, proj, c, *weights)
    return outs[0], outs[1:]


def _merge_kernel(*refs, n_chunks):
    oa_ref, ob_ref = refs[0], refs[1]
    ga_refs = refs[2:2 + n_chunks]
    gb_refs = refs[2 + n_chunks:2 + 2 * n_chunks]
    x_ref, wa_ref, wb_ref, wo_ref, o_ref, m_ref = refs[2 + 2 * n_chunks:]
    tn = o_ref.shape[1] // n_chunks
    for c in range(n_chunks):
        sl = slice(c * tn, (c + 1) * tn)
        ya = jnp.dot(oa_ref[...], wa_ref[:, sl], preferred_element_type=jnp.float32)
        yb = jnp.dot(ob_ref[...], wb_ref[:, sl], preferred_element_type=jnp.float32)
        ga = jax.nn.sigmoid(ga_refs[c][...].astype(jnp.float32))
        gb = jax.nn.sigmoid(gb_refs[c][...].astype(jnp.float32))
        m_ref[:, sl] = (ga * ya + gb * yb).astype(m_ref.dtype)
    for c in range(n_chunks):
        sl = slice(c * tn, (c + 1) * tn)
        o_ref[:, sl] = x_ref[:, sl] + jnp.dot(m_ref[...], wo_ref[:, sl],
                                              preferred_element_type=jnp.float32)


def _merge(oa, ob, proj, x, wa, wb, wo, *, tm=512, tn=512):
    s, d = x.shape
    n_chunks = d // tn
    const = lambda i: (0, 0)
    gate_specs = [pl.BlockSpec((tm, tn), functools.partial(lambda i, blk: (i, blk), blk=off // tn + c))
                  for off in (GA_OFF, GB_OFF) for c in range(n_chunks)]
    return pl.pallas_call(
        functools.partial(_merge_kernel, n_chunks=n_chunks),
        grid=(s // tm,),
        in_specs=[
            pl.BlockSpec((tm, A_Q_W), lambda i: (i, 0)),
            pl.BlockSpec((tm, B_W), lambda i: (i, 0)),
            *gate_specs,
            pl.BlockSpec((tm, d), lambda i: (i, 0)),
            pl.BlockSpec(wa.shape, const),
            pl.BlockSpec(wb.shape, const),
            pl.BlockSpec(wo.shape, const),
        ],
        out_specs=pl.BlockSpec((tm, d), lambda i: (i, 0)),
        out_shape=jax.ShapeDtypeStruct((s, d), jnp.float32),
        scratch_shapes=[pltpu.VMEM((tm, d), jnp.bfloat16)],
        compiler_params=pltpu.CompilerParams(
            dimension_semantics=("arbitrary",), vmem_limit_bytes=VMEM_LIMIT),
        name="merge",
    )(oa, ob, *([proj] * (2 * n_chunks)), x, wa, wb, wo)


def _ffn_kernel(x_hbm, g_ref, wg_ref, wu_ref, wd_ref, gf_ref, o_ref, xbuf, sem, h_ref, *,
                row_chunk):
    f = pl.program_id(1)

    def start_row_block():
        _rmsnorm_rows(xbuf, h_ref, g_ref[...], row_chunk)
        o_ref[...] = xbuf[...]

    _consume_row_block(x_hbm, xbuf, sem, start_row_block)

    h = h_ref[...]
    gate = jnp.dot(h, wg_ref[...], preferred_element_type=jnp.float32)
    up = jnp.dot(h, wu_ref[...], preferred_element_type=jnp.float32)
    act = (gate * jax.nn.sigmoid(gate) * up).astype(jnp.bfloat16)
    o_ref[...] += jnp.dot(act, wd_ref[...], preferred_element_type=jnp.float32)

    @pl.when(f == pl.num_programs(1) - 1)
    def _():
        _rmsnorm_rows(o_ref, o_ref, gf_ref[...], row_chunk)


def _ffn(x1, g, w_in, w_down, gf, *, tm=1024, tf=512):
    s, d = x1.shape
    nf = D_FF // tf
    return pl.pallas_call(
        functools.partial(_ffn_kernel, row_chunk=128),
        grid=(s // tm, nf),
        in_specs=[
            pl.BlockSpec(memory_space=pl.ANY),
            pl.BlockSpec((1, d), lambda i, f: (0, 0)),
            pl.BlockSpec((d, tf), lambda i, f: (0, f)),
            pl.BlockSpec((d, tf), lambda i, f: (0, f + nf)),
            pl.BlockSpec((tf, d), lambda i, f: (f, 0)),
            pl.BlockSpec((1, d), lambda i, f: (0, 0)),
        ],
        out_specs=pl.BlockSpec((tm, d), lambda i, f: (i, 0)),
        out_shape=jax.ShapeDtypeStruct((s, d), jnp.float32),
        scratch_shapes=[pltpu.VMEM((tm, d), x1.dtype), pltpu.SemaphoreType.DMA(()),
                        pltpu.VMEM((tm, d), jnp.bfloat16)],
        compiler_params=pltpu.CompilerParams(
            dimension_semantics=("arbitrary", "arbitrary"), vmem_limit_bytes=VMEM_LIMIT),
        name="ffn",
    )(x1, g, w_in, w_in, w_down, gf)


def kernel(x, norm_mix_g, w_in, sink_logits, w_branch_a, w_branch_b, w_out,
           norm_ffn_g, w_ffn_in, w_ffn_down, norm_final_g):
    b, s, d = x.shape
    assert (b, s, d) == (1, SEQ, D_MODEL) and w_in.shape[0] == 1
    x2 = x.reshape(s, d)
    proj = _in_proj(x2, norm_mix_g[0].reshape(1, d), w_in[0])
    oa = _swa(proj, sink_logits[0])
    f32_weights = [w_branch_a[0], w_branch_b[0], w_out[0], w_ffn_in[0], w_ffn_down[0]]
    ob, (wa, wb, wo, wfi, wfd) = _stick(proj, f32_weights)
    x1 = _merge(oa, ob, proj, x2, wa, wb, wo)
    out = _ffn(x1, norm_ffn_g[0].reshape(1, d), wfi, wfd, norm_final_g.reshape(1, d))
    return out.reshape(b, s, d)
```

```python
import functools
import math

import numpy as np
import jax
import jax.numpy as jnp
from jax import lax
from jax.experimental import pallas as pl
from jax.experimental.pallas import tpu as pltpu

D_MODEL = 2048
SEQ = 8192
HEAD_DIM = 64
A_Q_HEADS = 16
A_KV_HEADS = 4
A_GROUP = A_Q_HEADS // A_KV_HEADS
WINDOW = 128
B_HEADS = 16
BLOCK = 128
D_FF = 5632
EPS = 1e-6
SCALE = 1.0 / math.sqrt(HEAD_DIM)

A_Q_W = A_Q_HEADS * HEAD_DIM
A_KV_W = A_KV_HEADS * HEAD_DIM
B_W = B_HEADS * HEAD_DIM
IN_WIDTH = A_Q_W + 2 * A_KV_W + 3 * B_W + 2 * D_MODEL

LANES = 128
BF16_ROWS = 16
PAIRS_B = B_HEADS // 2

QA_OFF = 0
KA_OFF = QA_OFF + A_Q_W
VA_OFF = KA_OFF + A_KV_W
QB_OFF = VA_OFF + A_KV_W
KB_OFF = QB_OFF + B_W
VB_OFF = KB_OFF + B_W
GA_OFF = VB_OFF + B_W
GB_OFF = GA_OFF + D_MODEL

F32_EXP_ZERO = -104.0

VMEM_LIMIT = 56 * 1024 * 1024

_NT = (((1,), (1,)), ((), ()))


def _rmsnorm_rows(src_ref, dst_ref, g, row_chunk):
    def body(c, _):
        r = pl.multiple_of(c * row_chunk, row_chunk)
        x = src_ref[pl.ds(r, row_chunk), :]
        inv = lax.rsqrt(jnp.mean(x * x, axis=-1, keepdims=True) + EPS)
        dst_ref[pl.ds(r, row_chunk), :] = (x * inv * g).astype(dst_ref.dtype)
        return 0

    lax.fori_loop(0, src_ref.shape[0] // row_chunk, body, 0)


def _consume_row_block(x_hbm, xbuf, sem, consume):
    i = pl.program_id(0)
    rows = xbuf.shape[0]

    def copy(blk):
        return pltpu.make_async_copy(x_hbm.at[pl.ds(blk * rows, rows), :], xbuf, sem)

    @pl.when(pl.program_id(1) == 0)
    def _():
        @pl.when(i == 0)
        def _():
            copy(0).start()

        copy(i).wait()
        consume()

        @pl.when(i + 1 < pl.num_programs(0))
        def _():
            copy(i + 1).start()


def _in_proj_kernel(x_hbm, g_ref, w_ref, o_ref, xbuf, sem, h_ref, *, row_chunk):
    _consume_row_block(x_hbm, xbuf, sem,
                       lambda: _rmsnorm_rows(xbuf, h_ref, g_ref[...], row_chunk))
    o_ref[...] = jnp.dot(h_ref[...], w_ref[...].astype(jnp.bfloat16),
                         preferred_element_type=jnp.float32).astype(o_ref.dtype)


def _in_proj(x, g, w, *, tm=2048, tn=512):
    s, d = x.shape
    n = w.shape[1]
    return pl.pallas_call(
        functools.partial(_in_proj_kernel, row_chunk=128),
        grid=(s // tm, n // tn),
        in_specs=[
            pl.BlockSpec(memory_space=pl.ANY),
            pl.BlockSpec((1, d), lambda i, j: (0, 0)),
            pl.BlockSpec((d, tn), lambda i, j: (0, j)),
        ],
        out_specs=pl.BlockSpec((tm, tn), lambda i, j: (i, j)),
        out_shape=jax.ShapeDtypeStruct((s, n), jnp.bfloat16),
        scratch_shapes=[pltpu.VMEM((tm, d), x.dtype), pltpu.SemaphoreType.DMA(()),
                        pltpu.VMEM((tm, d), jnp.bfloat16)],
        compiler_params=pltpu.CompilerParams(
            dimension_semantics=("arbitrary", "arbitrary"),
            vmem_limit_bytes=VMEM_LIMIT),
        name="in_proj",
    )(x, g, w)


def _half_swap_matrix(n):
    c = np.arange(n)
    p = np.zeros((n, n), np.float32)
    p[c, c ^ HEAD_DIM] = 1.0
    return jnp.asarray(p, jnp.bfloat16)


def _swa_scores(q_ref, kp_ref, kc_ref, vp_ref, vc_ref, p_ref):
    k = jnp.concatenate([kp_ref[...], kc_ref[...]], axis=0)
    v = jnp.concatenate([vp_ref[...], vc_ref[...]], axis=0)
    k_sw = jnp.dot(k, p_ref[...], preferred_element_type=jnp.float32).astype(k.dtype)
    v_sw = jnp.dot(v, p_ref[...], preferred_element_type=jnp.float32).astype(v.dtype)
    lane = lax.broadcasted_iota(jnp.int32, (BLOCK, LANES), 1)
    left = lane < HEAD_DIM
    out = []
    for j in range(A_KV_HEADS):
        tile = slice((j // 2) * LANES, (j // 2 + 1) * LANES)
        qa = q_ref[:, (2 * j) * LANES:(2 * j + 1) * LANES] * jnp.bfloat16(SCALE)
        qb = q_ref[:, (2 * j + 1) * LANES:(2 * j + 2) * LANES] * jnp.bfloat16(SCALE)
        zero = jnp.zeros_like(qa)
        for side in range(2):
            keep = left if side == 0 else jnp.logical_not(left)
            aligned = (j % 2) == side
            kh = (k if aligned else k_sw)[:, tile]
            vh = (v if aligned else v_sw)[:, tile]
            lhs = jnp.concatenate([jnp.where(keep, qa, zero), jnp.where(keep, qb, zero)], axis=0)
            out.append((lax.dot_general(lhs, kh, _NT, preferred_element_type=jnp.float32), vh))
    return out


def _swa_finish(i, sink_ref, scores, o_ref):
    qi = lax.broadcasted_iota(jnp.int32, (BLOCK, 2 * BLOCK), 0)
    ki = lax.broadcasted_iota(jnp.int32, (BLOCK, 2 * BLOCK), 1)
    dist = BLOCK + qi - ki
    valid = (dist >= 0) & (dist < WINDOW) & ((ki >= BLOCK) | (i > 0))
    distf = dist.astype(jnp.float32)
    lane = lax.broadcasted_iota(jnp.int32, (BLOCK, LANES), 1)
    left = lane < HEAD_DIM
    for j in range(A_KV_HEADS):
        outs = []
        for side in range(2):
            s2, vh = scores[2 * j + side]
            ps, inv_den = [], []
            for t in range(2):
                h = A_GROUP * j + 2 * t + side
                slope = 2.0 ** (-8.0 * (h + 1) / A_Q_HEADS)
                s = s2[t * BLOCK:(t + 1) * BLOCK] - slope * distf
                s = jnp.where(valid, s, -jnp.inf)
                sink = sink_ref[h]
                m = jnp.maximum(jnp.max(s, axis=-1, keepdims=True), sink)
                p = jnp.exp(s - m)
                den = jnp.sum(p, axis=-1, keepdims=True) + jnp.exp(sink - m)
                ps.append(p.astype(jnp.bfloat16))
                inv_den.append(1.0 / den)
            o2 = jnp.dot(jnp.concatenate(ps, axis=0), vh, preferred_element_type=jnp.float32)
            outs.append((o2[:BLOCK] * inv_den[0], o2[BLOCK:] * inv_den[1]))
        for t in range(2):
            out = jnp.where(left, outs[0][t], outs[1][t])
            o_ref[:, (2 * j + t) * LANES:(2 * j + t + 1) * LANES] = out.astype(o_ref.dtype)


N_SWA_IN = 7


def _swa_in_specs(step_of):
    qa_blk = QA_OFF // A_Q_W
    ka_blk = KA_OFF // A_KV_W
    va_blk = VA_OFF // A_KV_W
    cur = lambda *g: step_of(*g)
    prev = lambda *g: jnp.maximum(step_of(*g) - 1, 0)
    return [
        pl.BlockSpec(memory_space=pltpu.SMEM),
        pl.BlockSpec((BLOCK, A_Q_W), lambda *g: (cur(*g), qa_blk)),
        pl.BlockSpec((BLOCK, A_KV_W), lambda *g: (prev(*g), ka_blk)),
        pl.BlockSpec((BLOCK, A_KV_W), lambda *g: (cur(*g), ka_blk)),
        pl.BlockSpec((BLOCK, A_KV_W), lambda *g: (prev(*g), va_blk)),
        pl.BlockSpec((BLOCK, A_KV_W), lambda *g: (cur(*g), va_blk)),
        pl.BlockSpec((A_KV_W, A_KV_W), lambda *g: (0, 0)),
    ]


STICK_FIRST_BLOCKS = 3
NEG_LOG2E = -1.0 / math.log(2.0)
MASKED = 1e30


def _stick_constant():
    j = np.arange(BLOCK)[:, None]
    s = np.arange(BLOCK)[None, :]
    m = np.concatenate([(j >= s).astype(np.float32), np.ones((BLOCK, LANES), np.float32)], axis=1)
    return jnp.asarray(m, jnp.bfloat16)


def _softplus(z):
    return jnp.maximum(z, 0.0) + jnp.log(1.0 + jnp.exp2(jnp.abs(z) * NEG_LOG2E))


def _stick_scores_clamped(lhs, k_ref, blk):
    nk = STICK_FIRST_BLOCKS
    first = jnp.maximum(blk - (nk - 1), 0)
    start = pl.multiple_of(first * BLOCK, BLOCK)
    z = lax.dot_general(lhs, k_ref[pl.ds(start, nk * BLOCK), :], _NT,
                        preferred_element_type=jnp.float32)
    return [z[:, n * BLOCK:(n + 1) * BLOCK] for n in range(nk)], first


def _stick_masked(z, blk, first, mask_all):
    nk = STICK_FIRST_BLOCKS
    row = lax.broadcasted_iota(jnp.int32, (2 * BLOCK, BLOCK), 0) & (BLOCK - 1)
    col = lax.broadcasted_iota(jnp.int32, (2 * BLOCK, BLOCK), 1)
    out = []
    for n in range(nk):
        zn = z[n]
        if mask_all:
            m = (first + n - blk) * BLOCK + col < row
        elif n == nk - 1:
            m = col < row
        else:
            m = None
        if m is not None:
            zn = jnp.where(m, zn, -MASKED)
        out.append((zn, _softplus(zn).astype(jnp.bfloat16)))
    return out


def _stick_weights(terms):
    tail = None
    a = [None] * len(terms)
    for n in reversed(range(len(terms))):
        zn, sums = terms[n]
        e = zn - sums[:, :BLOCK]
        if tail is not None:
            e = e - tail
        a[n] = jnp.exp(e).astype(jnp.bfloat16)
        tail = sums[:, BLOCK:] if tail is None else tail + sums[:, BLOCK:]
    return a, tail


def _stick_values_clamped(a, v_ref, first):
    start = pl.multiple_of(first * BLOCK, BLOCK)
    v0 = v_ref[pl.ds(start, STICK_FIRST_BLOCKS * BLOCK), :]
    return jnp.dot(jnp.concatenate(a, axis=1), v0, preferred_element_type=jnp.float32)


def _stick_live(tail):
    return (jnp.min(tail) < -F32_EXP_ZERO).astype(jnp.int32)


def _stick_rest(lhs, k_ref, v_ref, c, acc, tail, first, live0):
    live = _stick_live

    def cond(carry):
        kb, alive, _, _ = carry
        return jnp.logical_and(kb >= 0, alive > 0)

    def body(carry):
        kb, _, acc, tail = carry
        st = pl.multiple_of(kb * BLOCK, BLOCK)
        z = lax.dot_general(lhs, k_ref[pl.ds(st, BLOCK), :], _NT,
                            preferred_element_type=jnp.float32)
        sums = jnp.dot(_softplus(z).astype(jnp.bfloat16), c, preferred_element_type=jnp.float32)
        a = jnp.exp(z - sums[:, :BLOCK] - tail)
        acc = acc + jnp.dot(a.astype(jnp.bfloat16), v_ref[pl.ds(st, BLOCK), :],
                            preferred_element_type=jnp.float32)
        tail = tail + sums[:, BLOCK:]
        return kb - 1, live(tail), acc, tail

    _, _, acc, _ = lax.while_loop(cond, body, (first - 1, live0, acc, tail))
    return acc


def _attn_kernel(*refs, q_blocks, n_cast):
    swa_in, refs = refs[:N_SWA_IN], refs[N_SWA_IN:]
    q_ref, k_ref, v_ref, c_ref = refs[:4]
    w_refs = refs[4:4 + n_cast]
    oa_ref, o_ref = refs[4 + n_cast:6 + n_cast]
    wo_refs = refs[6 + n_cast:]
    it = pl.program_id(1)
    step = pl.program_id(0) * pl.num_programs(1) + it
    lane = lax.broadcasted_iota(jnp.int32, (BLOCK, LANES), 1)
    left = lane < HEAD_DIM
    c = c_ref[...]

    def run(first_step):
        for w_ref, wo_ref in zip(w_refs, wo_refs):
            wo_ref[...] = w_ref[...].astype(wo_ref.dtype)
        blks = [it * q_blocks + g for g in range(q_blocks)]
        lhss = []
        for g in range(q_blocks):
            q = q_ref[g * BLOCK:(g + 1) * BLOCK, :] * jnp.bfloat16(SCALE)
            zero = jnp.zeros_like(q)
            lhss.append(jnp.concatenate([jnp.where(left, q, zero), jnp.where(left, zero, q)], axis=0))
        _swa_finish(step, swa_in[0], _swa_scores(*swa_in[1:]), oa_ref)
        nk = STICK_FIRST_BLOCKS
        clamped = [_stick_scores_clamped(lhss[g], k_ref, blks[g]) for g in range(q_blocks)]
        z = [zs for zs, _ in clamped]
        firsts = [first for _, first in clamped]
        masked = [_stick_masked(z[g], blks[g], firsts[g], first_step and g < nk - 1)
                  for g in range(q_blocks)]
        sums = jnp.dot(jnp.concatenate([sp for m in masked for _, sp in m], axis=0), c,
                       preferred_element_type=jnp.float32)
        rows = 2 * BLOCK
        terms = [[(masked[g][n][0], sums[(g * nk + n) * rows:(g * nk + n + 1) * rows])
                  for n in range(nk)] for g in range(q_blocks)]
        weights = [_stick_weights(terms[g]) for g in range(q_blocks)]
        accs = [_stick_values_clamped(weights[g][0], v_ref, firsts[g]) for g in range(q_blocks)]
        lives = [_stick_live(tail) for _, tail in weights]
        for g in range(q_blocks):
            acc = _stick_rest(lhss[g], k_ref, v_ref, c, accs[g], weights[g][1], firsts[g], lives[g])
            o_ref[g * BLOCK:(g + 1) * BLOCK, :] = jnp.where(
                left, acc[:BLOCK], acc[BLOCK:]).astype(o_ref.dtype)

    assert q_blocks >= STICK_FIRST_BLOCKS - 1
    pl.when(it == 0)(lambda: run(True))
    pl.when(it > 0)(lambda: run(False))


def _cast_block_spec(shape, steps, n_inner):
    rows, cols = shape
    for n_cb in range(1, steps + 1):
        n_rb = steps // n_cb
        if steps % n_cb == 0 and rows % n_rb == 0 and cols % n_cb == 0 \
                and (rows // n_rb) % BF16_ROWS == 0 and (cols // n_cb) % LANES == 0:
            return pl.BlockSpec(
                (rows // n_rb, cols // n_cb),
                lambda b, i: ((b * n_inner + i) // n_cb, (b * n_inner + i) % n_cb))
    raise ValueError(f"cannot tile {shape} into {steps} blocks")


def _attention(proj, sinks, weights, *, q_blocks=8):
    s = proj.shape[0]
    tq = q_blocks * BLOCK
    n_q = s // tq
    steps = PAIRS_B * n_q
    assert steps == s // BLOCK
    c = _stick_constant()
    qb_blk = QB_OFF // LANES
    kb_blk = KB_OFF // LANES
    vb_blk = VB_OFF // LANES
    step_of = lambda b, i: b * n_q + i
    w_specs = [_cast_block_spec(w.shape, steps, n_q) for w in weights]
    outs = pl.pallas_call(
        functools.partial(_attn_kernel, q_blocks=q_blocks, n_cast=len(weights)),
        grid=(PAIRS_B, n_q),
        in_specs=[
            *_swa_in_specs(step_of),
            pl.BlockSpec((tq, LANES), lambda b, i: (i, qb_blk + b)),
            pl.BlockSpec((s, LANES), lambda b, i: (0, kb_blk + b)),
            pl.BlockSpec((s, LANES), lambda b, i: (0, vb_blk + b)),
            pl.BlockSpec(c.shape, lambda b, i: (0, 0)),
            *w_specs,
        ],
        out_specs=[pl.BlockSpec((BLOCK, A_Q_W), lambda b, i: (step_of(b, i), 0)),
                   pl.BlockSpec((tq, LANES), lambda b, i: (i, b)), *w_specs],
        out_shape=[jax.ShapeDtypeStruct((s, A_Q_W), jnp.bfloat16),
                   jax.ShapeDtypeStruct((s, B_W), jnp.bfloat16)]
        + [jax.ShapeDtypeStruct(w.shape, jnp.bfloat16) for w in weights],
        compiler_params=pltpu.CompilerParams(
            dimension_semantics=("arbitrary", "arbitrary"), vmem_limit_bytes=VMEM_LIMIT),
        name="attention",
    )(sinks, *([proj] * 5), _half_swap_matrix(A_KV_W), proj, proj, proj, c, *weights)
    return outs[0], outs[1], outs[2:]


def _merge_kernel(*refs, n_chunks):
    oa_ref, ob_ref = refs[0], refs[1]
    ga_refs = refs[2:2 + n_chunks]
    gb_refs = refs[2 + n_chunks:2 + 2 * n_chunks]
    x_ref, wa_ref, wb_ref, wo_ref, o_ref, m_ref = refs[2 + 2 * n_chunks:]
    tn = o_ref.shape[1] // n_chunks
    for c in range(n_chunks):
        sl = slice(c * tn, (c + 1) * tn)
        ya = jnp.dot(oa_ref[...], wa_ref[:, sl], preferred_element_type=jnp.float32)
        yb = jnp.dot(ob_ref[...], wb_ref[:, sl], preferred_element_type=jnp.float32)
        ga = jax.nn.sigmoid(ga_refs[c][...].astype(jnp.float32))
        gb = jax.nn.sigmoid(gb_refs[c][...].astype(jnp.float32))
        m_ref[:, sl] = (ga * ya + gb * yb).astype(m_ref.dtype)
    for c in range(n_chunks):
        sl = slice(c * tn, (c + 1) * tn)
        o_ref[:, sl] = x_ref[:, sl] + jnp.dot(m_ref[...], wo_ref[:, sl],
                                              preferred_element_type=jnp.float32)


def _merge(oa, ob, proj, x, wa, wb, wo, *, tm=512, tn=512):
    s, d = x.shape
    n_chunks = d // tn
    const = lambda i: (0, 0)
    gate_specs = [pl.BlockSpec((tm, tn), functools.partial(lambda i, blk: (i, blk), blk=off // tn + c))
                  for off in (GA_OFF, GB_OFF) for c in range(n_chunks)]
    return pl.pallas_call(
        functools.partial(_merge_kernel, n_chunks=n_chunks),
        grid=(s // tm,),
        in_specs=[
            pl.BlockSpec((tm, A_Q_W), lambda i: (i, 0)),
            pl.BlockSpec((tm, B_W), lambda i: (i, 0)),
            *gate_specs,
            pl.BlockSpec((tm, d), lambda i: (i, 0)),
            pl.BlockSpec(wa.shape, const),
            pl.BlockSpec(wb.shape, const),
            pl.BlockSpec(wo.shape, const),
        ],
        out_specs=pl.BlockSpec((tm, d), lambda i: (i, 0)),
        out_shape=jax.ShapeDtypeStruct((s, d), jnp.float32),
        scratch_shapes=[pltpu.VMEM((tm, d), jnp.bfloat16)],
        compiler_params=pltpu.CompilerParams(
            dimension_semantics=("arbitrary",), vmem_limit_bytes=VMEM_LIMIT),
        name="merge",
    )(oa, ob, *([proj] * (2 * n_chunks)), x, wa, wb, wo)


def _ffn_kernel(x_hbm, g_ref, wg_ref, wu_ref, wd_ref, gf_ref, o_ref, xbuf, sem, h_ref, *,
                row_chunk):
    f = pl.program_id(1)

    def start_row_block():
        _rmsnorm_rows(xbuf, h_ref, g_ref[...], row_chunk)
        o_ref[...] = xbuf[...]

    _consume_row_block(x_hbm, xbuf, sem, start_row_block)

    h = h_ref[...]
    gate = jnp.dot(h, wg_ref[...], preferred_element_type=jnp.float32)
    up = jnp.dot(h, wu_ref[...], preferred_element_type=jnp.float32)
    act = (gate * jax.nn.sigmoid(gate) * up).astype(jnp.bfloat16)
    o_ref[...] += jnp.dot(act, wd_ref[...], preferred_element_type=jnp.float32)

    @pl.when(f == pl.num_programs(1) - 1)
    def _():
        _rmsnorm_rows(o_ref, o_ref, gf_ref[...], row_chunk)


def _ffn(x1, g, w_in, w_down, gf, *, tm=1024, tf=512):
    s, d = x1.shape
    nf = D_FF // tf
    return pl.pallas_call(
        functools.partial(_ffn_kernel, row_chunk=128),
        grid=(s // tm, nf),
        in_specs=[
            pl.BlockSpec(memory_space=pl.ANY),
            pl.BlockSpec((1, d), lambda i, f: (0, 0)),
            pl.BlockSpec((d, tf), lambda i, f: (0, f)),
            pl.BlockSpec((d, tf), lambda i, f: (0, f + nf)),
            pl.BlockSpec((tf, d), lambda i, f: (f, 0)),
            pl.BlockSpec((1, d), lambda i, f: (0, 0)),
        ],
        out_specs=pl.BlockSpec((tm, d), lambda i, f: (i, 0)),
        out_shape=jax.ShapeDtypeStruct((s, d), jnp.float32),
        scratch_shapes=[pltpu.VMEM((tm, d), x1.dtype), pltpu.SemaphoreType.DMA(()),
                        pltpu.VMEM((tm, d), jnp.bfloat16)],
        compiler_params=pltpu.CompilerParams(
            dimension_semantics=("arbitrary", "arbitrary"), vmem_limit_bytes=VMEM_LIMIT),
        name="ffn",
    )(x1, g, w_in, w_in, w_down, gf)


def kernel(x, norm_mix_g, w_in, sink_logits, w_branch_a, w_branch_b, w_out,
           norm_ffn_g, w_ffn_in, w_ffn_down, norm_final_g):
    b, s, d = x.shape
    assert (b, s, d) == (1, SEQ, D_MODEL) and w_in.shape[0] == 1
    x2 = x.reshape(s, d)
    proj = _in_proj(x2, norm_mix_g[0].reshape(1, d), w_in[0])
    f32_weights = [w_branch_a[0], w_branch_b[0], w_out[0], w_ffn_in[0], w_ffn_down[0]]
    oa, ob, (wa, wb, wo, wfi, wfd) = _attention(proj, sink_logits[0], f32_weights)
    x1 = _merge(oa, ob, proj, x2, wa, wb, wo)
    out = _ffn(x1, norm_ffn_g[0].reshape(1, d), wfi, wfd, norm_final_g.reshape(1, d))
    return out.reshape(b, s, d)
```

```python
import functools
import math

import numpy as np
import jax
import jax.numpy as jnp
from jax import lax
from jax.experimental import pallas as pl
from jax.experimental.pallas import tpu as pltpu

D_MODEL = 2048
SEQ = 8192
HEAD_DIM = 64
A_Q_HEADS = 16
A_KV_HEADS = 4
A_GROUP = A_Q_HEADS // A_KV_HEADS
WINDOW = 128
B_HEADS = 16
BLOCK = 128
D_FF = 5632
EPS = 1e-6
SCALE = 1.0 / math.sqrt(HEAD_DIM)

A_Q_W = A_Q_HEADS * HEAD_DIM
A_KV_W = A_KV_HEADS * HEAD_DIM
B_W = B_HEADS * HEAD_DIM
IN_WIDTH = A_Q_W + 2 * A_KV_W + 3 * B_W + 2 * D_MODEL

LANES = 128
BF16_ROWS = 16
PAIRS_B = B_HEADS // 2

QA_OFF = 0
KA_OFF = QA_OFF + A_Q_W
VA_OFF = KA_OFF + A_KV_W
QB_OFF = VA_OFF + A_KV_W
KB_OFF = QB_OFF + B_W
VB_OFF = KB_OFF + B_W
GA_OFF = VB_OFF + B_W
GB_OFF = GA_OFF + D_MODEL

F32_EXP_ZERO = -104.0

VMEM_LIMIT = 56 * 1024 * 1024

_NT = (((1,), (1,)), ((), ()))


def _rmsnorm_rows(src_ref, dst_ref, g, row_chunk):
    def body(c, _):
        r = pl.multiple_of(c * row_chunk, row_chunk)
        x = src_ref[pl.ds(r, row_chunk), :]
        inv = lax.rsqrt(jnp.mean(x * x, axis=-1, keepdims=True) + EPS)
        dst_ref[pl.ds(r, row_chunk), :] = (x * inv * g).astype(dst_ref.dtype)
        return 0

    lax.fori_loop(0, src_ref.shape[0] // row_chunk, body, 0)


def _consume_row_block(x_hbm, xbuf, sem, consume):
    i = pl.program_id(0)
    rows = xbuf.shape[0]

    def copy(blk):
        return pltpu.make_async_copy(x_hbm.at[pl.ds(blk * rows, rows), :], xbuf, sem)

    @pl.when(pl.program_id(1) == 0)
    def _():
        @pl.when(i == 0)
        def _():
            copy(0).start()

        copy(i).wait()
        consume()

        @pl.when(i + 1 < pl.num_programs(0))
        def _():
            copy(i + 1).start()


def _in_proj_kernel(x_hbm, g_ref, w_ref, o_ref, xbuf, sem, h_ref, *, row_chunk):
    _consume_row_block(x_hbm, xbuf, sem,
                       lambda: _rmsnorm_rows(xbuf, h_ref, g_ref[...], row_chunk))
    o_ref[...] = jnp.dot(h_ref[...], w_ref[...].astype(jnp.bfloat16),
                         preferred_element_type=jnp.float32).astype(o_ref.dtype)


def _in_proj(x, g, w, *, tm=2048, tn=512):
    s, d = x.shape
    n = w.shape[1]
    return pl.pallas_call(
        functools.partial(_in_proj_kernel, row_chunk=128),
        grid=(s // tm, n // tn),
        in_specs=[
            pl.BlockSpec(memory_space=pl.ANY),
            pl.BlockSpec((1, d), lambda i, j: (0, 0)),
            pl.BlockSpec((d, tn), lambda i, j: (0, j)),
        ],
        out_specs=pl.BlockSpec((tm, tn), lambda i, j: (i, j)),
        out_shape=jax.ShapeDtypeStruct((s, n), jnp.bfloat16),
        scratch_shapes=[pltpu.VMEM((tm, d), x.dtype), pltpu.SemaphoreType.DMA(()),
                        pltpu.VMEM((tm, d), jnp.bfloat16)],
        compiler_params=pltpu.CompilerParams(
            dimension_semantics=("arbitrary", "arbitrary"),
            vmem_limit_bytes=VMEM_LIMIT),
        name="in_proj",
    )(x, g, w)


def _half_swap_matrix(n):
    c = np.arange(n)
    p = np.zeros((n, n), np.float32)
    p[c, c ^ HEAD_DIM] = 1.0
    return jnp.asarray(p, jnp.bfloat16)


def _swa_scores(q_ref, kp_ref, kc_ref, vp_ref, vc_ref, p_ref):
    k = jnp.concatenate([kp_ref[...], kc_ref[...]], axis=0)
    v = jnp.concatenate([vp_ref[...], vc_ref[...]], axis=0)
    k_sw = jnp.dot(k, p_ref[...], preferred_element_type=jnp.float32).astype(k.dtype)
    v_sw = jnp.dot(v, p_ref[...], preferred_element_type=jnp.float32).astype(v.dtype)
    lane = lax.broadcasted_iota(jnp.int32, (BLOCK, LANES), 1)
    left = lane < HEAD_DIM
    out = []
    for j in range(A_KV_HEADS):
        tile = slice((j // 2) * LANES, (j // 2 + 1) * LANES)
        qa = q_ref[:, (2 * j) * LANES:(2 * j + 1) * LANES] * jnp.bfloat16(SCALE)
        qb = q_ref[:, (2 * j + 1) * LANES:(2 * j + 2) * LANES] * jnp.bfloat16(SCALE)
        zero = jnp.zeros_like(qa)
        for side in range(2):
            keep = left if side == 0 else jnp.logical_not(left)
            aligned = (j % 2) == side
            kh = (k if aligned else k_sw)[:, tile]
            vh = (v if aligned else v_sw)[:, tile]
            lhs = jnp.concatenate([jnp.where(keep, qa, zero), jnp.where(keep, qb, zero)], axis=0)
            out.append((lax.dot_general(lhs, kh, _NT, preferred_element_type=jnp.float32), vh))
    return out


def _swa_finish(i, sink_ref, scores, o_ref):
    qi = lax.broadcasted_iota(jnp.int32, (BLOCK, 2 * BLOCK), 0)
    ki = lax.broadcasted_iota(jnp.int32, (BLOCK, 2 * BLOCK), 1)
    dist = BLOCK + qi - ki
    valid = (dist >= 0) & (dist < WINDOW) & ((ki >= BLOCK) | (i > 0))
    distf = dist.astype(jnp.float32)
    lane = lax.broadcasted_iota(jnp.int32, (BLOCK, LANES), 1)
    left = lane < HEAD_DIM
    for j in range(A_KV_HEADS):
        outs = []
        for side in range(2):
            s2, vh = scores[2 * j + side]
            ps, inv_den = [], []
            for t in range(2):
                h = A_GROUP * j + 2 * t + side
                slope = 2.0 ** (-8.0 * (h + 1) / A_Q_HEADS)
                s = s2[t * BLOCK:(t + 1) * BLOCK] - slope * distf
                s = jnp.where(valid, s, -jnp.inf)
                sink = sink_ref[h]
                m = jnp.maximum(jnp.max(s, axis=-1, keepdims=True), sink)
                p = jnp.exp(s - m)
                den = jnp.sum(p, axis=-1, keepdims=True) + jnp.exp(sink - m)
                ps.append(p.astype(jnp.bfloat16))
                inv_den.append(1.0 / den)
            o2 = jnp.dot(jnp.concatenate(ps, axis=0), vh, preferred_element_type=jnp.float32)
            outs.append((o2[:BLOCK] * inv_den[0], o2[BLOCK:] * inv_den[1]))
        for t in range(2):
            out = jnp.where(left, outs[0][t], outs[1][t])
            o_ref[:, (2 * j + t) * LANES:(2 * j + t + 1) * LANES] = out.astype(o_ref.dtype)


N_SWA_IN = 7


def _swa_in_specs(step_of):
    qa_blk = QA_OFF // A_Q_W
    ka_blk = KA_OFF // A_KV_W
    va_blk = VA_OFF // A_KV_W
    cur = lambda *g: step_of(*g)
    prev = lambda *g: jnp.maximum(step_of(*g) - 1, 0)
    return [
        pl.BlockSpec(memory_space=pltpu.SMEM),
        pl.BlockSpec((BLOCK, A_Q_W), lambda *g: (cur(*g), qa_blk)),
        pl.BlockSpec((BLOCK, A_KV_W), lambda *g: (prev(*g), ka_blk)),
        pl.BlockSpec((BLOCK, A_KV_W), lambda *g: (cur(*g), ka_blk)),
        pl.BlockSpec((BLOCK, A_KV_W), lambda *g: (prev(*g), va_blk)),
        pl.BlockSpec((BLOCK, A_KV_W), lambda *g: (cur(*g), va_blk)),
        pl.BlockSpec((A_KV_W, A_KV_W), lambda *g: (0, 0)),
    ]


STICK_FIRST_BLOCKS = 3
NEG_LOG2E = -1.0 / math.log(2.0)
MASKED = 1e30


def _stick_constant():
    j = np.arange(BLOCK)[:, None]
    s = np.arange(BLOCK)[None, :]
    tri = (j >= s).astype(np.float32)
    zero = np.zeros_like(tri)
    return jnp.asarray(np.block([[tri, zero], [zero, tri]]), jnp.bfloat16)


def _row_total(sums):
    return jnp.broadcast_to(sums[:, :1], sums.shape)


def _softplus(z):
    return jnp.maximum(z, 0.0) + jnp.log(1.0 + jnp.exp2(jnp.abs(z) * NEG_LOG2E))


def _stick_scores_clamped(lhs, k_ref, blk):
    nk = STICK_FIRST_BLOCKS
    first = jnp.maximum(blk - (nk - 1), 0)
    start = pl.multiple_of(first * BLOCK, BLOCK)
    z = lax.dot_general(lhs, k_ref[pl.ds(start, nk * BLOCK), :], _NT,
                        preferred_element_type=jnp.float32)
    return [z[:, n * BLOCK:(n + 1) * BLOCK] for n in range(nk)], first


def _stick_masked(z, blk, first, mask_all):
    nk = STICK_FIRST_BLOCKS
    row = lax.broadcasted_iota(jnp.int32, (2 * BLOCK, BLOCK), 0) & (BLOCK - 1)
    col = lax.broadcasted_iota(jnp.int32, (2 * BLOCK, BLOCK), 1)
    out = []
    for n in range(nk):
        zn = z[n]
        if mask_all:
            m = (first + n - blk) * BLOCK + col < row
        elif n == nk - 1:
            m = col < row
        else:
            m = None
        if m is not None:
            zn = jnp.where(m, zn, -MASKED)
        out.append((zn, _softplus(zn).astype(jnp.bfloat16)))
    return out


def _stick_weights(terms):
    tail = None
    a = [None] * len(terms)
    for n in reversed(range(len(terms))):
        zn, sums = terms[n]
        e = zn - sums
        if tail is not None:
            e = e - tail
        a[n] = jnp.exp(e).astype(jnp.bfloat16)
        tail = _row_total(sums) if tail is None else tail + _row_total(sums)
    return a, tail


def _stick_values_clamped(a, v_ref, first):
    start = pl.multiple_of(first * BLOCK, BLOCK)
    v0 = v_ref[pl.ds(start, STICK_FIRST_BLOCKS * BLOCK), :]
    return jnp.dot(jnp.concatenate(a, axis=1), v0, preferred_element_type=jnp.float32)


def _stick_live(tail):
    return (jnp.min(tail) < -F32_EXP_ZERO).astype(jnp.int32)


def _stick_rest(lhs, k_ref, v_ref, c, acc, tail, first, live0):
    live = _stick_live

    def cond(carry):
        kb, alive, _, _ = carry
        return jnp.logical_and(kb >= 0, alive > 0)

    def body(carry):
        kb, _, acc, tail = carry
        st = pl.multiple_of(kb * BLOCK, BLOCK)
        z = lax.dot_general(lhs, k_ref[pl.ds(st, BLOCK), :], _NT,
                            preferred_element_type=jnp.float32)
        sums = jnp.dot(_softplus(z).astype(jnp.bfloat16), c[:BLOCK, :BLOCK],
                       preferred_element_type=jnp.float32)
        a = jnp.exp(z - sums - tail)
        acc = acc + jnp.dot(a.astype(jnp.bfloat16), v_ref[pl.ds(st, BLOCK), :],
                            preferred_element_type=jnp.float32)
        tail = tail + _row_total(sums)
        return kb - 1, live(tail), acc, tail

    _, _, acc, _ = lax.while_loop(cond, body, (first - 1, live0, acc, tail))
    return acc


def _attn_kernel(*refs, q_blocks, n_cast):
    swa_in, refs = refs[:N_SWA_IN], refs[N_SWA_IN:]
    q_ref, k_ref, v_ref, c_ref = refs[:4]
    w_refs = refs[4:4 + n_cast]
    oa_ref, o_ref = refs[4 + n_cast:6 + n_cast]
    wo_refs = refs[6 + n_cast:]
    it = pl.program_id(1)
    step = pl.program_id(0) * pl.num_programs(1) + it
    lane = lax.broadcasted_iota(jnp.int32, (BLOCK, LANES), 1)
    left = lane < HEAD_DIM
    c = c_ref[...]

    def run(first_step):
        for w_ref, wo_ref in zip(w_refs, wo_refs):
            wo_ref[...] = w_ref[...].astype(wo_ref.dtype)
        blks = [it * q_blocks + g for g in range(q_blocks)]
        lhss = []
        for g in range(q_blocks):
            q = q_ref[g * BLOCK:(g + 1) * BLOCK, :] * jnp.bfloat16(SCALE)
            zero = jnp.zeros_like(q)
            lhss.append(jnp.concatenate([jnp.where(left, q, zero), jnp.where(left, zero, q)], axis=0))
        _swa_finish(step, swa_in[0], _swa_scores(*swa_in[1:]), oa_ref)
        nk = STICK_FIRST_BLOCKS
        clamped = [_stick_scores_clamped(lhss[g], k_ref, blks[g]) for g in range(q_blocks)]
        z = [zs for zs, _ in clamped]
        firsts = [first for _, first in clamped]
        masked = [_stick_masked(z[g], blks[g], firsts[g], first_step and g < nk - 1)
                  for g in range(q_blocks)]
        sps = [sp for m in masked for _, sp in m]
        sums = jnp.dot(
            jnp.concatenate([jnp.concatenate(sps[t:t + 2], axis=1) for t in range(0, len(sps), 2)],
                            axis=0), c, preferred_element_type=jnp.float32)
        rows = 2 * BLOCK

        def block_sums(t):
            return sums[(t // 2) * rows:(t // 2 + 1) * rows, (t % 2) * BLOCK:(t % 2 + 1) * BLOCK]

        terms = [[(masked[g][n][0], block_sums(g * nk + n)) for n in range(nk)]
                 for g in range(q_blocks)]
        weights = [_stick_weights(terms[g]) for g in range(q_blocks)]
        accs = [_stick_values_clamped(weights[g][0], v_ref, firsts[g]) for g in range(q_blocks)]
        lives = [_stick_live(tail) for _, tail in weights]
        for g in range(q_blocks):
            acc = _stick_rest(lhss[g], k_ref, v_ref, c, accs[g], weights[g][1], firsts[g], lives[g])
            o_ref[g * BLOCK:(g + 1) * BLOCK, :] = jnp.where(
                left, acc[:BLOCK], acc[BLOCK:]).astype(o_ref.dtype)

    assert q_blocks >= STICK_FIRST_BLOCKS - 1
    pl.when(it == 0)(lambda: run(True))
    pl.when(it > 0)(lambda: run(False))


def _cast_block_spec(shape, steps, n_inner):
    rows, cols = shape
    for n_cb in range(1, steps + 1):
        n_rb = steps // n_cb
        if steps % n_cb == 0 and rows % n_rb == 0 and cols % n_cb == 0 \
                and (rows // n_rb) % BF16_ROWS == 0 and (cols // n_cb) % LANES == 0:
            return pl.BlockSpec(
                (rows // n_rb, cols // n_cb),
                lambda b, i: ((b * n_inner + i) // n_cb, (b * n_inner + i) % n_cb))
    raise ValueError(f"cannot tile {shape} into {steps} blocks")


def _attention(proj, sinks, weights, *, q_blocks=8):
    s = proj.shape[0]
    tq = q_blocks * BLOCK
    n_q = s // tq
    steps = PAIRS_B * n_q
    assert steps == s // BLOCK
    c = _stick_constant()
    qb_blk = QB_OFF // LANES
    kb_blk = KB_OFF // LANES
    vb_blk = VB_OFF // LANES
    step_of = lambda b, i: b * n_q + i
    w_specs = [_cast_block_spec(w.shape, steps, n_q) for w in weights]
    outs = pl.pallas_call(
        functools.partial(_attn_kernel, q_blocks=q_blocks, n_cast=len(weights)),
        grid=(PAIRS_B, n_q),
        in_specs=[
            *_swa_in_specs(step_of),
            pl.BlockSpec((tq, LANES), lambda b, i: (i, qb_blk + b)),
            pl.BlockSpec((s, LANES), lambda b, i: (0, kb_blk + b)),
            pl.BlockSpec((s, LANES), lambda b, i: (0, vb_blk + b)),
            pl.BlockSpec(c.shape, lambda b, i: (0, 0)),
            *w_specs,
        ],
        out_specs=[pl.BlockSpec((BLOCK, A_Q_W), lambda b, i: (step_of(b, i), 0)),
                   pl.BlockSpec((tq, LANES), lambda b, i: (i, b)), *w_specs],
        out_shape=[jax.ShapeDtypeStruct((s, A_Q_W), jnp.bfloat16),
                   jax.ShapeDtypeStruct((s, B_W), jnp.bfloat16)]
        + [jax.ShapeDtypeStruct(w.shape, jnp.bfloat16) for w in weights],
        compiler_params=pltpu.CompilerParams(
            dimension_semantics=("arbitrary", "arbitrary"), vmem_limit_bytes=VMEM_LIMIT),
        name="attention",
    )(sinks, *([proj] * 5), _half_swap_matrix(A_KV_W), proj, proj, proj, c, *weights)
    return outs[0], outs[1], outs[2:]


def _merge_kernel(*refs, n_chunks):
    oa_ref, ob_ref = refs[0], refs[1]
    ga_refs = refs[2:2 + n_chunks]
    gb_refs = refs[2 + n_chunks:2 + 2 * n_chunks]
    x_ref, wa_ref, wb_ref, wo_ref, o_ref, m_ref = refs[2 + 2 * n_chunks:]
    tn = o_ref.shape[1] // n_chunks
    for c in range(n_chunks):
        sl = slice(c * tn, (c + 1) * tn)
        ya = jnp.dot(oa_ref[...], wa_ref[:, sl], preferred_element_type=jnp.float32)
        yb = jnp.dot(ob_ref[...], wb_ref[:, sl], preferred_element_type=jnp.float32)
        ga = jax.nn.sigmoid(ga_refs[c][...].astype(jnp.float32))
        gb = jax.nn.sigmoid(gb_refs[c][...].astype(jnp.float32))
        m_ref[:, sl] = (ga * ya + gb * yb).astype(m_ref.dtype)
    for c in range(n_chunks):
        sl = slice(c * tn, (c + 1) * tn)
        o_ref[:, sl] = x_ref[:, sl] + jnp.dot(m_ref[...], wo_ref[:, sl],
                                              preferred_element_type=jnp.float32)


def _merge(oa, ob, proj, x, wa, wb, wo, *, tm=512, tn=512):
    s, d = x.shape
    n_chunks = d // tn
    const = lambda i: (0, 0)
    gate_specs = [pl.BlockSpec((tm, tn), functools.partial(lambda i, blk: (i, blk), blk=off // tn + c))
                  for off in (GA_OFF, GB_OFF) for c in range(n_chunks)]
    return pl.pallas_call(
        functools.partial(_merge_kernel, n_chunks=n_chunks),
        grid=(s // tm,),
        in_specs=[
            pl.BlockSpec((tm, A_Q_W), lambda i: (i, 0)),
            pl.BlockSpec((tm, B_W), lambda i: (i, 0)),
            *gate_specs,
            pl.BlockSpec((tm, d), lambda i: (i, 0)),
            pl.BlockSpec(wa.shape, const),
            pl.BlockSpec(wb.shape, const),
            pl.BlockSpec(wo.shape, const),
        ],
        out_specs=pl.BlockSpec((tm, d), lambda i: (i, 0)),
        out_shape=jax.ShapeDtypeStruct((s, d), jnp.float32),
        scratch_shapes=[pltpu.VMEM((tm, d), jnp.bfloat16)],
        compiler_params=pltpu.CompilerParams(
            dimension_semantics=("arbitrary",), vmem_limit_bytes=VMEM_LIMIT),
        name="merge",
    )(oa, ob, *([proj] * (2 * n_chunks)), x, wa, wb, wo)


def _ffn_kernel(x_hbm, g_ref, wg_ref, wu_ref, wd_ref, gf_ref, o_ref, xbuf, sem, h_ref, *,
                row_chunk):
    f = pl.program_id(1)

    def start_row_block():
        _rmsnorm_rows(xbuf, h_ref, g_ref[...], row_chunk)
        o_ref[...] = xbuf[...]

    _consume_row_block(x_hbm, xbuf, sem, start_row_block)

    h = h_ref[...]
    gate = jnp.dot(h, wg_ref[...], preferred_element_type=jnp.float32)
    up = jnp.dot(h, wu_ref[...], preferred_element_type=jnp.float32)
    act = (gate * jax.nn.sigmoid(gate) * up).astype(jnp.bfloat16)
    o_ref[...] += jnp.dot(act, wd_ref[...], preferred_element_type=jnp.float32)

    @pl.when(f == pl.num_programs(1) - 1)
    def _():
        _rmsnorm_rows(o_ref, o_ref, gf_ref[...], row_chunk)


def _ffn(x1, g, w_in, w_down, gf, *, tm=1024, tf=512):
    s, d = x1.shape
    nf = D_FF // tf
    return pl.pallas_call(
        functools.partial(_ffn_kernel, row_chunk=128),
        grid=(s // tm, nf),
        in_specs=[
            pl.BlockSpec(memory_space=pl.ANY),
            pl.BlockSpec((1, d), lambda i, f: (0, 0)),
            pl.BlockSpec((d, tf), lambda i, f: (0, f)),
            pl.BlockSpec((d, tf), lambda i, f: (0, f + nf)),
            pl.BlockSpec((tf, d), lambda i, f: (f, 0)),
            pl.BlockSpec((1, d), lambda i, f: (0, 0)),
        ],
        out_specs=pl.BlockSpec((tm, d), lambda i, f: (i, 0)),
        out_shape=jax.ShapeDtypeStruct((s, d), jnp.float32),
        scratch_shapes=[pltpu.VMEM((tm, d), x1.dtype), pltpu.SemaphoreType.DMA(()),
                        pltpu.VMEM((tm, d), jnp.bfloat16)],
        compiler_params=pltpu.CompilerParams(
            dimension_semantics=("arbitrary", "arbitrary"), vmem_limit_bytes=VMEM_LIMIT),
        name="ffn",
    )(x1, g, w_in, w_in, w_down, gf)


def kernel(x, norm_mix_g, w_in, sink_logits, w_branch_a, w_branch_b, w_out,
           norm_ffn_g, w_ffn_in, w_ffn_down, norm_final_g):
    b, s, d = x.shape
    assert (b, s, d) == (1, SEQ, D_MODEL) and w_in.shape[0] == 1
    x2 = x.reshape(s, d)
    proj = _in_proj(x2, norm_mix_g[0].reshape(1, d), w_in[0])
    f32_weights = [w_branch_a[0], w_branch_b[0], w_out[0], w_ffn_in[0], w_ffn_down[0]]
    oa, ob, (wa, wb, wo, wfi, wfd) = _attention(proj, sink_logits[0], f32_weights)
    x1 = _merge(oa, ob, proj, x2, wa, wb, wo)
    out = _ffn(x1, norm_ffn_g[0].reshape(1, d), wfi, wfd, norm_final_g.reshape(1, d))
    return out.reshape(b, s, d)
```

```python
import functools
import math

import numpy as np
import jax
import jax.numpy as jnp
from jax import lax
from jax.experimental import pallas as pl
from jax.experimental.pallas import tpu as pltpu

D_MODEL = 2048
SEQ = 8192
HEAD_DIM = 64
A_Q_HEADS = 16
A_KV_HEADS = 4
A_GROUP = A_Q_HEADS // A_KV_HEADS
WINDOW = 128
B_HEADS = 16
BLOCK = 128
D_FF = 5632
EPS = 1e-6
SCALE = 1.0 / math.sqrt(HEAD_DIM)

A_Q_W = A_Q_HEADS * HEAD_DIM
A_KV_W = A_KV_HEADS * HEAD_DIM
B_W = B_HEADS * HEAD_DIM
IN_WIDTH = A_Q_W + 2 * A_KV_W + 3 * B_W + 2 * D_MODEL

LANES = 128
BF16_ROWS = 16
PAIRS_B = B_HEADS // 2

QA_OFF = 0
KA_OFF = QA_OFF + A_Q_W
VA_OFF = KA_OFF + A_KV_W
QB_OFF = VA_OFF + A_KV_W
KB_OFF = QB_OFF + B_W
VB_OFF = KB_OFF + B_W
GA_OFF = VB_OFF + B_W
GB_OFF = GA_OFF + D_MODEL

F32_EXP_ZERO = -104.0

VMEM_LIMIT = 56 * 1024 * 1024

_NT = (((1,), (1,)), ((), ()))


def _rmsnorm_rows(src_ref, dst_ref, g, row_chunk):
    def body(c, _):
        r = pl.multiple_of(c * row_chunk, row_chunk)
        x = src_ref[pl.ds(r, row_chunk), :]
        inv = lax.rsqrt(jnp.mean(x * x, axis=-1, keepdims=True) + EPS)
        dst_ref[pl.ds(r, row_chunk), :] = (x * inv * g).astype(dst_ref.dtype)
        return 0

    lax.fori_loop(0, src_ref.shape[0] // row_chunk, body, 0)


def _consume_row_block(x_hbm, xbuf, sem, consume):
    i = pl.program_id(0)
    rows = xbuf.shape[0]

    def copy(blk):
        return pltpu.make_async_copy(x_hbm.at[pl.ds(blk * rows, rows), :], xbuf, sem)

    @pl.when(pl.program_id(1) == 0)
    def _():
        @pl.when(i == 0)
        def _():
            copy(0).start()

        copy(i).wait()
        consume()

        @pl.when(i + 1 < pl.num_programs(0))
        def _():
            copy(i + 1).start()


def _with_normed_rows(x_hbm, xbuf, sem, h_ref, g_ref, row_chunk, step_body):
    i, j = pl.program_id(0), pl.program_id(1)
    n_i = pl.num_programs(0)
    rows = xbuf.shape[0]
    n_chunks = rows // row_chunk
    cur = i % 2

    def copy(blk):
        return pltpu.make_async_copy(x_hbm.at[pl.ds(blk * rows, rows), :], xbuf, sem)

    def norm_chunk(c, slot):
        r = pl.multiple_of(c * row_chunk, row_chunk)
        x = xbuf[pl.ds(r, row_chunk), :]
        inv = lax.rsqrt(jnp.mean(x * x, axis=-1, keepdims=True) + EPS)
        h_ref[slot, pl.ds(r, row_chunk), :] = (x * inv * g_ref[...]).astype(h_ref.dtype)

    @pl.when(j == 0)
    def _():
        @pl.when(i == 0)
        def _():
            copy(0).start()
            copy(0).wait()

            def body(c, _):
                norm_chunk(c, 0)
                return 0

            lax.fori_loop(0, n_chunks, body, 0)

        @pl.when(i + 1 < n_i)
        def _():
            copy(i + 1).start()

        step_body(h_ref.at[cur])

    @pl.when(j > 0)
    def _():
        @pl.when(jnp.logical_and(j == 1, i + 1 < n_i))
        def _():
            copy(i + 1).wait()

        step_body(h_ref.at[cur])
        norm_chunk(jnp.minimum(j, n_chunks) - 1, 1 - cur)


def _in_proj_kernel(x_hbm, g_ref, w_ref, o_ref, xbuf, sem, h_ref, *, row_chunk):
    def step_body(h):
        o_ref[...] = jnp.dot(h[...], w_ref[...].astype(jnp.bfloat16),
                             preferred_element_type=jnp.float32).astype(o_ref.dtype)

    _with_normed_rows(x_hbm, xbuf, sem, h_ref, g_ref, row_chunk, step_body)


def _in_proj(x, g, w, *, tm=2048, tn=512):
    s, d = x.shape
    n = w.shape[1]
    assert n // tn - 1 >= tm // 128
    return pl.pallas_call(
        functools.partial(_in_proj_kernel, row_chunk=128),
        grid=(s // tm, n // tn),
        in_specs=[
            pl.BlockSpec(memory_space=pl.ANY),
            pl.BlockSpec((1, d), lambda i, j: (0, 0)),
            pl.BlockSpec((d, tn), lambda i, j: (0, j)),
        ],
        out_specs=pl.BlockSpec((tm, tn), lambda i, j: (i, j)),
        out_shape=jax.ShapeDtypeStruct((s, n), jnp.bfloat16),
        scratch_shapes=[pltpu.VMEM((tm, d), x.dtype), pltpu.SemaphoreType.DMA(()),
                        pltpu.VMEM((2, tm, d), jnp.bfloat16)],
        compiler_params=pltpu.CompilerParams(
            dimension_semantics=("arbitrary", "arbitrary"),
            vmem_limit_bytes=VMEM_LIMIT),
        name="in_proj",
    )(x, g, w)


def _half_swap_matrix(n):
    c = np.arange(n)
    p = np.zeros((n, n), np.float32)
    p[c, c ^ HEAD_DIM] = 1.0
    return jnp.asarray(p, jnp.bfloat16)


def _swa_scores(q_ref, kp_ref, kc_ref, vp_ref, vc_ref, p_ref):
    k = jnp.concatenate([kp_ref[...], kc_ref[...]], axis=0)
    v = jnp.concatenate([vp_ref[...], vc_ref[...]], axis=0)
    k_sw = jnp.dot(k, p_ref[...], preferred_element_type=jnp.float32).astype(k.dtype)
    v_sw = jnp.dot(v, p_ref[...], preferred_element_type=jnp.float32).astype(v.dtype)
    lane = lax.broadcasted_iota(jnp.int32, (BLOCK, LANES), 1)
    left = lane < HEAD_DIM
    out = []
    for j in range(A_KV_HEADS):
        tile = slice((j // 2) * LANES, (j // 2 + 1) * LANES)
        qa = q_ref[:, (2 * j) * LANES:(2 * j + 1) * LANES] * jnp.bfloat16(SCALE)
        qb = q_ref[:, (2 * j + 1) * LANES:(2 * j + 2) * LANES] * jnp.bfloat16(SCALE)
        zero = jnp.zeros_like(qa)
        for side in range(2):
            keep = left if side == 0 else jnp.logical_not(left)
            aligned = (j % 2) == side
            kh = (k if aligned else k_sw)[:, tile]
            vh = (v if aligned else v_sw)[:, tile]
            lhs = jnp.concatenate([jnp.where(keep, qa, zero), jnp.where(keep, qb, zero)], axis=0)
            out.append((lax.dot_general(lhs, kh, _NT, preferred_element_type=jnp.float32), vh))
    return out


def _swa_finish(i, sink_ref, scores, o_ref):
    qi = lax.broadcasted_iota(jnp.int32, (BLOCK, 2 * BLOCK), 0)
    ki = lax.broadcasted_iota(jnp.int32, (BLOCK, 2 * BLOCK), 1)
    dist = BLOCK + qi - ki
    valid = (dist >= 0) & (dist < WINDOW) & ((ki >= BLOCK) | (i > 0))
    distf = dist.astype(jnp.float32)
    lane = lax.broadcasted_iota(jnp.int32, (BLOCK, LANES), 1)
    left = lane < HEAD_DIM
    for j in range(A_KV_HEADS):
        outs = []
        for side in range(2):
            s2, vh = scores[2 * j + side]
            ps, inv_den = [], []
            for t in range(2):
                h = A_GROUP * j + 2 * t + side
                slope = 2.0 ** (-8.0 * (h + 1) / A_Q_HEADS)
                s = s2[t * BLOCK:(t + 1) * BLOCK] - slope * distf
                s = jnp.where(valid, s, -jnp.inf)
                sink = sink_ref[h]
                m = jnp.maximum(jnp.max(s, axis=-1, keepdims=True), sink)
                p = jnp.exp(s - m)
                den = jnp.sum(p, axis=-1, keepdims=True) + jnp.exp(sink - m)
                ps.append(p.astype(jnp.bfloat16))
                inv_den.append(1.0 / den)
            o2 = jnp.dot(jnp.concatenate(ps, axis=0), vh, preferred_element_type=jnp.float32)
            outs.append((o2[:BLOCK] * inv_den[0], o2[BLOCK:] * inv_den[1]))
        for t in range(2):
            out = jnp.where(left, outs[0][t], outs[1][t])
            o_ref[:, (2 * j + t) * LANES:(2 * j + t + 1) * LANES] = out.astype(o_ref.dtype)


N_SWA_IN = 7


def _swa_in_specs(step_of):
    qa_blk = QA_OFF // A_Q_W
    ka_blk = KA_OFF // A_KV_W
    va_blk = VA_OFF // A_KV_W
    cur = lambda *g: step_of(*g)
    prev = lambda *g: jnp.maximum(step_of(*g) - 1, 0)
    return [
        pl.BlockSpec(memory_space=pltpu.SMEM),
        pl.BlockSpec((BLOCK, A_Q_W), lambda *g: (cur(*g), qa_blk)),
        pl.BlockSpec((BLOCK, A_KV_W), lambda *g: (prev(*g), ka_blk)),
        pl.BlockSpec((BLOCK, A_KV_W), lambda *g: (cur(*g), ka_blk)),
        pl.BlockSpec((BLOCK, A_KV_W), lambda *g: (prev(*g), va_blk)),
        pl.BlockSpec((BLOCK, A_KV_W), lambda *g: (cur(*g), va_blk)),
        pl.BlockSpec((A_KV_W, A_KV_W), lambda *g: (0, 0)),
    ]


STICK_FIRST_BLOCKS = 3
NEG_LOG2E = -1.0 / math.log(2.0)
MASKED = 1e30


def _stick_constant():
    j = np.arange(BLOCK)[:, None]
    s = np.arange(BLOCK)[None, :]
    tri = (j >= s).astype(np.float32)
    zero = np.zeros_like(tri)
    return jnp.asarray(np.block([[tri, zero], [zero, tri]]), jnp.bfloat16)


def _row_total(sums):
    return jnp.broadcast_to(sums[:, :1], sums.shape)


def _softplus(z):
    return jnp.maximum(z, 0.0) + jnp.log(1.0 + jnp.exp2(jnp.abs(z) * NEG_LOG2E))


def _stick_scores_clamped(lhs, k_ref, blk):
    nk = STICK_FIRST_BLOCKS
    first = jnp.maximum(blk - (nk - 1), 0)
    start = pl.multiple_of(first * BLOCK, BLOCK)
    z = lax.dot_general(lhs, k_ref[pl.ds(start, nk * BLOCK), :], _NT,
                        preferred_element_type=jnp.float32)
    return [z[:, n * BLOCK:(n + 1) * BLOCK] for n in range(nk)], first


def _stick_masked(z, blk, first, mask_all):
    nk = STICK_FIRST_BLOCKS
    row = lax.broadcasted_iota(jnp.int32, (2 * BLOCK, BLOCK), 0) & (BLOCK - 1)
    col = lax.broadcasted_iota(jnp.int32, (2 * BLOCK, BLOCK), 1)
    out = []
    for n in range(nk):
        zn = z[n]
        if mask_all:
            m = (first + n - blk) * BLOCK + col < row
        elif n == nk - 1:
            m = col < row
        else:
            m = None
        if m is not None:
            zn = jnp.where(m, zn, -MASKED)
        out.append((zn, _softplus(zn).astype(jnp.bfloat16)))
    return out


def _stick_weights(terms):
    tail = None
    a = [None] * len(terms)
    for n in reversed(range(len(terms))):
        zn, sums = terms[n]
        e = zn - sums
        if tail is not None:
            e = e - tail
        a[n] = jnp.exp(e).astype(jnp.bfloat16)
        tail = _row_total(sums) if tail is None else tail + _row_total(sums)
    return a, tail


def _stick_values_clamped(a, v_ref, first):
    start = pl.multiple_of(first * BLOCK, BLOCK)
    v0 = v_ref[pl.ds(start, STICK_FIRST_BLOCKS * BLOCK), :]
    return jnp.dot(jnp.concatenate(a, axis=1), v0, preferred_element_type=jnp.float32)


def _stick_live(tail):
    return (jnp.min(tail) < -F32_EXP_ZERO).astype(jnp.int32)


def _stick_rest(lhs, k_ref, v_ref, c, acc, tail, first, live0):
    live = _stick_live

    def cond(carry):
        kb, alive, _, _ = carry
        return jnp.logical_and(kb >= 0, alive > 0)

    def body(carry):
        kb, _, acc, tail = carry
        st = pl.multiple_of(kb * BLOCK, BLOCK)
        z = lax.dot_general(lhs, k_ref[pl.ds(st, BLOCK), :], _NT,
                            preferred_element_type=jnp.float32)
        sums = jnp.dot(_softplus(z).astype(jnp.bfloat16), c[:BLOCK, :BLOCK],
                       preferred_element_type=jnp.float32)
        a = jnp.exp(z - sums - tail)
        acc = acc + jnp.dot(a.astype(jnp.bfloat16), v_ref[pl.ds(st, BLOCK), :],
                            preferred_element_type=jnp.float32)
        tail = tail + _row_total(sums)
        return kb - 1, live(tail), acc, tail

    _, _, acc, _ = lax.while_loop(cond, body, (first - 1, live0, acc, tail))
    return acc


def _attn_kernel(*refs, q_blocks, n_cast):
    swa_in, refs = refs[:N_SWA_IN], refs[N_SWA_IN:]
    q_ref, k_ref, v_ref, c_ref = refs[:4]
    w_refs = refs[4:4 + n_cast]
    oa_ref, o_ref = refs[4 + n_cast:6 + n_cast]
    wo_refs = refs[6 + n_cast:]
    it = pl.program_id(1)
    step = pl.program_id(0) * pl.num_programs(1) + it
    lane = lax.broadcasted_iota(jnp.int32, (BLOCK, LANES), 1)
    left = lane < HEAD_DIM
    c = c_ref[...]

    def run(first_step):
        for w_ref, wo_ref in zip(w_refs, wo_refs):
            wo_ref[...] = w_ref[...].astype(wo_ref.dtype)
        blks = [it * q_blocks + g for g in range(q_blocks)]
        lhss = []
        for g in range(q_blocks):
            q = q_ref[g * BLOCK:(g + 1) * BLOCK, :] * jnp.bfloat16(SCALE)
            zero = jnp.zeros_like(q)
            lhss.append(jnp.concatenate([jnp.where(left, q, zero), jnp.where(left, zero, q)], axis=0))
        _swa_finish(step, swa_in[0], _swa_scores(*swa_in[1:]), oa_ref)
        nk = STICK_FIRST_BLOCKS
        clamped = [_stick_scores_clamped(lhss[g], k_ref, blks[g]) for g in range(q_blocks)]
        z = [zs for zs, _ in clamped]
        firsts = [first for _, first in clamped]
        masked = [_stick_masked(z[g], blks[g], firsts[g], first_step and g < nk - 1)
                  for g in range(q_blocks)]
        sps = [sp for m in masked for _, sp in m]
        sums = jnp.dot(
            jnp.concatenate([jnp.concatenate(sps[t:t + 2], axis=1) for t in range(0, len(sps), 2)],
                            axis=0), c, preferred_element_type=jnp.float32)
        rows = 2 * BLOCK

        def block_sums(t):
            return sums[(t // 2) * rows:(t // 2 + 1) * rows, (t % 2) * BLOCK:(t % 2 + 1) * BLOCK]

        terms = [[(masked[g][n][0], block_sums(g * nk + n)) for n in range(nk)]
                 for g in range(q_blocks)]
        weights = [_stick_weights(terms[g]) for g in range(q_blocks)]
        accs = [_stick_values_clamped(weights[g][0], v_ref, firsts[g]) for g in range(q_blocks)]
        lives = [_stick_live(tail) for _, tail in weights]
        for g in range(q_blocks):
            acc = _stick_rest(lhss[g], k_ref, v_ref, c, accs[g], weights[g][1], firsts[g], lives[g])
            o_ref[g * BLOCK:(g + 1) * BLOCK, :] = jnp.where(
                left, acc[:BLOCK], acc[BLOCK:]).astype(o_ref.dtype)

    assert q_blocks >= STICK_FIRST_BLOCKS - 1
    pl.when(it == 0)(lambda: run(True))
    pl.when(it > 0)(lambda: run(False))


def _cast_block_spec(shape, steps, n_inner):
    rows, cols = shape
    for n_cb in range(1, steps + 1):
        n_rb = steps // n_cb
        if steps % n_cb == 0 and rows % n_rb == 0 and cols % n_cb == 0 \
                and (rows // n_rb) % BF16_ROWS == 0 and (cols // n_cb) % LANES == 0:
            return pl.BlockSpec(
                (rows // n_rb, cols // n_cb),
                lambda b, i: ((b * n_inner + i) // n_cb, (b * n_inner + i) % n_cb))
    raise ValueError(f"cannot tile {shape} into {steps} blocks")


def _attention(proj, sinks, weights, *, q_blocks=8):
    s = proj.shape[0]
    tq = q_blocks * BLOCK
    n_q = s // tq
    steps = PAIRS_B * n_q
    assert steps == s // BLOCK
    c = _stick_constant()
    qb_blk = QB_OFF // LANES
    kb_blk = KB_OFF // LANES
    vb_blk = VB_OFF // LANES
    step_of = lambda b, i: b * n_q + i
    w_specs = [_cast_block_spec(w.shape, steps, n_q) for w in weights]
    outs = pl.pallas_call(
        functools.partial(_attn_kernel, q_blocks=q_blocks, n_cast=len(weights)),
        grid=(PAIRS_B, n_q),
        in_specs=[
            *_swa_in_specs(step_of),
            pl.BlockSpec((tq, LANES), lambda b, i: (i, qb_blk + b)),
            pl.BlockSpec((s, LANES), lambda b, i: (0, kb_blk + b)),
            pl.BlockSpec((s, LANES), lambda b, i: (0, vb_blk + b)),
            pl.BlockSpec(c.shape, lambda b, i: (0, 0)),
            *w_specs,
        ],
        out_specs=[pl.BlockSpec((BLOCK, A_Q_W), lambda b, i: (step_of(b, i), 0)),
                   pl.BlockSpec((tq, LANES), lambda b, i: (i, b)), *w_specs],
        out_shape=[jax.ShapeDtypeStruct((s, A_Q_W), jnp.bfloat16),
                   jax.ShapeDtypeStruct((s, B_W), jnp.bfloat16)]
        + [jax.ShapeDtypeStruct(w.shape, jnp.bfloat16) for w in weights],
        compiler_params=pltpu.CompilerParams(
            dimension_semantics=("arbitrary", "arbitrary"), vmem_limit_bytes=VMEM_LIMIT),
        name="attention",
    )(sinks, *([proj] * 5), _half_swap_matrix(A_KV_W), proj, proj, proj, c, *weights)
    return outs[0], outs[1], outs[2:]


def _merge_kernel(*refs, n_chunks):
    oa_ref, ob_ref = refs[0], refs[1]
    ga_refs = refs[2:2 + n_chunks]
    gb_refs = refs[2 + n_chunks:2 + 2 * n_chunks]
    x_ref, wa_ref, wb_ref, wo_ref, o_ref, m_ref = refs[2 + 2 * n_chunks:]
    tn = o_ref.shape[1] // n_chunks
    for c in range(n_chunks):
        sl = slice(c * tn, (c + 1) * tn)
        ya = jnp.dot(oa_ref[...], wa_ref[:, sl], preferred_element_type=jnp.float32)
        yb = jnp.dot(ob_ref[...], wb_ref[:, sl], preferred_element_type=jnp.float32)
        ga = jax.nn.sigmoid(ga_refs[c][...].astype(jnp.float32))
        gb = jax.nn.sigmoid(gb_refs[c][...].astype(jnp.float32))
        m_ref[:, sl] = (ga * ya + gb * yb).astype(m_ref.dtype)
    for c in range(n_chunks):
        sl = slice(c * tn, (c + 1) * tn)
        o_ref[:, sl] = x_ref[:, sl] + jnp.dot(m_ref[...], wo_ref[:, sl],
                                              preferred_element_type=jnp.float32)


def _merge(oa, ob, proj, x, wa, wb, wo, *, tm=512, tn=512):
    s, d = x.shape
    n_chunks = d // tn
    const = lambda i: (0, 0)
    gate_specs = [pl.BlockSpec((tm, tn), functools.partial(lambda i, blk: (i, blk), blk=off // tn + c))
                  for off in (GA_OFF, GB_OFF) for c in range(n_chunks)]
    return pl.pallas_call(
        functools.partial(_merge_kernel, n_chunks=n_chunks),
        grid=(s // tm,),
        in_specs=[
            pl.BlockSpec((tm, A_Q_W), lambda i: (i, 0)),
            pl.BlockSpec((tm, B_W), lambda i: (i, 0)),
            *gate_specs,
            pl.BlockSpec((tm, d), lambda i: (i, 0)),
            pl.BlockSpec(wa.shape, const),
            pl.BlockSpec(wb.shape, const),
            pl.BlockSpec(wo.shape, const),
        ],
        out_specs=pl.BlockSpec((tm, d), lambda i: (i, 0)),
        out_shape=jax.ShapeDtypeStruct((s, d), jnp.float32),
        scratch_shapes=[pltpu.VMEM((tm, d), jnp.bfloat16)],
        compiler_params=pltpu.CompilerParams(
            dimension_semantics=("arbitrary",), vmem_limit_bytes=VMEM_LIMIT),
        name="merge",
    )(oa, ob, *([proj] * (2 * n_chunks)), x, wa, wb, wo)


def _ffn_kernel(x_hbm, g_ref, wg_ref, wu_ref, wd_ref, gf_ref, o_ref, xbuf, sem, h_ref, *,
                row_chunk):
    f = pl.program_id(1)

    def start_row_block():
        _rmsnorm_rows(xbuf, h_ref, g_ref[...], row_chunk)
        o_ref[...] = xbuf[...]

    _consume_row_block(x_hbm, xbuf, sem, start_row_block)

    h = h_ref[...]
    gate = jnp.dot(h, wg_ref[...], preferred_element_type=jnp.float32)
    up = jnp.dot(h, wu_ref[...], preferred_element_type=jnp.float32)
    act = (gate * jax.nn.sigmoid(gate) * up).astype(jnp.bfloat16)
    o_ref[...] += jnp.dot(act, wd_ref[...], preferred_element_type=jnp.float32)

    @pl.when(f == pl.num_programs(1) - 1)
    def _():
        _rmsnorm_rows(o_ref, o_ref, gf_ref[...], row_chunk)


def _ffn(x1, g, w_in, w_down, gf, *, tm=1024, tf=512):
    s, d = x1.shape
    nf = D_FF // tf
    return pl.pallas_call(
        functools.partial(_ffn_kernel, row_chunk=128),
        grid=(s // tm, nf),
        in_specs=[
            pl.BlockSpec(memory_space=pl.ANY),
            pl.BlockSpec((1, d), lambda i, f: (0, 0)),
            pl.BlockSpec((d, tf), lambda i, f: (0, f)),
            pl.BlockSpec((d, tf), lambda i, f: (0, f + nf)),
            pl.BlockSpec((tf, d), lambda i, f: (f, 0)),
            pl.BlockSpec((1, d), lambda i, f: (0, 0)),
        ],
        out_specs=pl.BlockSpec((tm, d), lambda i, f: (i, 0)),
        out_shape=jax.ShapeDtypeStruct((s, d), jnp.float32),
        scratch_shapes=[pltpu.VMEM((tm, d), x1.dtype), pltpu.SemaphoreType.DMA(()),
                        pltpu.VMEM((tm, d), jnp.bfloat16)],
        compiler_params=pltpu.CompilerParams(
            dimension_semantics=("arbitrary", "arbitrary"), vmem_limit_bytes=VMEM_LIMIT),
        name="ffn",
    )(x1, g, w_in, w_in, w_down, gf)


def kernel(x, norm_mix_g, w_in, sink_logits, w_branch_a, w_branch_b, w_out,
           norm_ffn_g, w_ffn_in, w_ffn_down, norm_final_g):
    b, s, d = x.shape
    assert (b, s, d) == (1, SEQ, D_MODEL) and w_in.shape[0] == 1
    x2 = x.reshape(s, d)
    proj = _in_proj(x2, norm_mix_g[0].reshape(1, d), w_in[0])
    f32_weights = [w_branch_a[0], w_branch_b[0], w_out[0], w_ffn_in[0], w_ffn_down[0]]
    oa, ob, (wa, wb, wo, wfi, wfd) = _attention(proj, sink_logits[0], f32_weights)
    x1 = _merge(oa, ob, proj, x2, wa, wb, wo)
    out = _ffn(x1, norm_ffn_g[0].reshape(1, d), wfi, wfd, norm_final_g.reshape(1, d))
    return out.reshape(b, s, d)
```

```python
import functools
import math

import numpy as np
import jax
import jax.numpy as jnp
from jax import lax
from jax.experimental import pallas as pl
from jax.experimental.pallas import tpu as pltpu

D_MODEL = 2048
SEQ = 8192
HEAD_DIM = 64
A_Q_HEADS = 16
A_KV_HEADS = 4
A_GROUP = A_Q_HEADS // A_KV_HEADS
WINDOW = 128
B_HEADS = 16
BLOCK = 128
D_FF = 5632
EPS = 1e-6
SCALE = 1.0 / math.sqrt(HEAD_DIM)

A_Q_W = A_Q_HEADS * HEAD_DIM
A_KV_W = A_KV_HEADS * HEAD_DIM
B_W = B_HEADS * HEAD_DIM
IN_WIDTH = A_Q_W + 2 * A_KV_W + 3 * B_W + 2 * D_MODEL

LANES = 128
BF16_ROWS = 16
PAIRS_B = B_HEADS // 2

QA_OFF = 0
KA_OFF = QA_OFF + A_Q_W
VA_OFF = KA_OFF + A_KV_W
QB_OFF = VA_OFF + A_KV_W
KB_OFF = QB_OFF + B_W
VB_OFF = KB_OFF + B_W
GA_OFF = VB_OFF + B_W
GB_OFF = GA_OFF + D_MODEL

F32_EXP_ZERO = -104.0

VMEM_LIMIT = 56 * 1024 * 1024
NORM_ROWS = 128

_NT = (((1,), (1,)), ((), ()))


def _rmsnorm_rows(src_ref, dst_ref, g, row_chunk):
    def body(c, _):
        r = pl.multiple_of(c * row_chunk, row_chunk)
        x = src_ref[pl.ds(r, row_chunk), :]
        inv = lax.rsqrt(jnp.mean(x * x, axis=-1, keepdims=True) + EPS)
        dst_ref[pl.ds(r, row_chunk), :] = (x * inv * g).astype(dst_ref.dtype)
        return 0

    lax.fori_loop(0, src_ref.shape[0] // row_chunk, body, 0)


def _consume_row_block(x_hbm, xbuf, sem, consume):
    i = pl.program_id(0)
    rows = xbuf.shape[0]

    def copy(blk):
        return pltpu.make_async_copy(x_hbm.at[pl.ds(blk * rows, rows), :], xbuf, sem)

    @pl.when(pl.program_id(1) == 0)
    def _():
        @pl.when(i == 0)
        def _():
            copy(0).start()

        copy(i).wait()
        consume()

        @pl.when(i + 1 < pl.num_programs(0))
        def _():
            copy(i + 1).start()


def _in_proj_kernel(x_hbm, g_ref, w_ref, o_ref, xbuf, sem, h_ref, *, row_chunk):
    _consume_row_block(x_hbm, xbuf, sem,
                       lambda: _rmsnorm_rows(xbuf, h_ref, g_ref[...], row_chunk))
    o_ref[...] = jnp.dot(h_ref[...], w_ref[...].astype(jnp.bfloat16),
                         preferred_element_type=jnp.float32).astype(o_ref.dtype)


def _in_proj(x, g, w, *, tm=2048, tn=512):
    s, d = x.shape
    n = w.shape[1]
    return pl.pallas_call(
        functools.partial(_in_proj_kernel, row_chunk=NORM_ROWS),
        grid=(s // tm, n // tn),
        in_specs=[
            pl.BlockSpec(memory_space=pl.ANY),
            pl.BlockSpec((1, d), lambda i, j: (0, 0)),
            pl.BlockSpec((d, tn), lambda i, j: (0, j)),
        ],
        out_specs=pl.BlockSpec((tm, tn), lambda i, j: (i, j)),
        out_shape=jax.ShapeDtypeStruct((s, n), jnp.bfloat16),
        scratch_shapes=[pltpu.VMEM((tm, d), x.dtype), pltpu.SemaphoreType.DMA(()),
                        pltpu.VMEM((tm, d), jnp.bfloat16)],
        compiler_params=pltpu.CompilerParams(
            dimension_semantics=("arbitrary", "arbitrary"),
            vmem_limit_bytes=VMEM_LIMIT),
        name="in_proj",
    )(x, g, w)


def _half_swap_matrix(n):
    c = np.arange(n)
    p = np.zeros((n, n), np.float32)
    p[c, c ^ HEAD_DIM] = 1.0
    return jnp.asarray(p, jnp.bfloat16)


def _swa_scores(q_ref, kp_ref, kc_ref, vp_ref, vc_ref, p_ref):
    k = jnp.concatenate([kp_ref[...], kc_ref[...]], axis=0)
    v = jnp.concatenate([vp_ref[...], vc_ref[...]], axis=0)
    k_sw = jnp.dot(k, p_ref[...], preferred_element_type=jnp.float32).astype(k.dtype)
    v_sw = jnp.dot(v, p_ref[...], preferred_element_type=jnp.float32).astype(v.dtype)
    lane = lax.broadcasted_iota(jnp.int32, (BLOCK, LANES), 1)
    left = lane < HEAD_DIM
    out = []
    for j in range(A_KV_HEADS):
        tile = slice((j // 2) * LANES, (j // 2 + 1) * LANES)
        qa = q_ref[:, (2 * j) * LANES:(2 * j + 1) * LANES] * jnp.bfloat16(SCALE)
        qb = q_ref[:, (2 * j + 1) * LANES:(2 * j + 2) * LANES] * jnp.bfloat16(SCALE)
        zero = jnp.zeros_like(qa)
        for side in range(2):
            keep = left if side == 0 else jnp.logical_not(left)
            aligned = (j % 2) == side
            kh = (k if aligned else k_sw)[:, tile]
            vh = (v if aligned else v_sw)[:, tile]
            lhs = jnp.concatenate([jnp.where(keep, qa, zero), jnp.where(keep, qb, zero)], axis=0)
            out.append((lax.dot_general(lhs, kh, _NT, preferred_element_type=jnp.float32), vh))
    return out


def _swa_finish(i, sink_ref, scores, o_ref):
    qi = lax.broadcasted_iota(jnp.int32, (BLOCK, 2 * BLOCK), 0)
    ki = lax.broadcasted_iota(jnp.int32, (BLOCK, 2 * BLOCK), 1)
    dist = BLOCK + qi - ki
    valid = (dist >= 0) & (dist < WINDOW) & ((ki >= BLOCK) | (i > 0))
    distf = dist.astype(jnp.float32)
    lane = lax.broadcasted_iota(jnp.int32, (BLOCK, LANES), 1)
    left = lane < HEAD_DIM
    for j in range(A_KV_HEADS):
        outs = []
        for side in range(2):
            s2, vh = scores[2 * j + side]
            ps, inv_den = [], []
            for t in range(2):
                h = A_GROUP * j + 2 * t + side
                slope = 2.0 ** (-8.0 * (h + 1) / A_Q_HEADS)
                s = s2[t * BLOCK:(t + 1) * BLOCK] - slope * distf
                s = jnp.where(valid, s, -jnp.inf)
                sink = sink_ref[h]
                m = jnp.maximum(jnp.max(s, axis=-1, keepdims=True), sink)
                p = jnp.exp(s - m)
                den = jnp.sum(p, axis=-1, keepdims=True) + jnp.exp(sink - m)
                ps.append(p.astype(jnp.bfloat16))
                inv_den.append(1.0 / den)
            o2 = jnp.dot(jnp.concatenate(ps, axis=0), vh, preferred_element_type=jnp.float32)
            outs.append((o2[:BLOCK] * inv_den[0], o2[BLOCK:] * inv_den[1]))
        for t in range(2):
            out = jnp.where(left, outs[0][t], outs[1][t])
            o_ref[:, (2 * j + t) * LANES:(2 * j + t + 1) * LANES] = out.astype(o_ref.dtype)


N_SWA_IN = 7


def _swa_in_specs(step_of):
    qa_blk = QA_OFF // A_Q_W
    ka_blk = KA_OFF // A_KV_W
    va_blk = VA_OFF // A_KV_W
    cur = lambda *g: step_of(*g)
    prev = lambda *g: jnp.maximum(step_of(*g) - 1, 0)
    return [
        pl.BlockSpec(memory_space=pltpu.SMEM),
        pl.BlockSpec((BLOCK, A_Q_W), lambda *g: (cur(*g), qa_blk)),
        pl.BlockSpec((BLOCK, A_KV_W), lambda *g: (prev(*g), ka_blk)),
        pl.BlockSpec((BLOCK, A_KV_W), lambda *g: (cur(*g), ka_blk)),
        pl.BlockSpec((BLOCK, A_KV_W), lambda *g: (prev(*g), va_blk)),
        pl.BlockSpec((BLOCK, A_KV_W), lambda *g: (cur(*g), va_blk)),
        pl.BlockSpec((A_KV_W, A_KV_W), lambda *g: (0, 0)),
    ]


STICK_FIRST_BLOCKS = 3
NEG_LOG2E = -1.0 / math.log(2.0)
MASKED = 1e30


def _stick_constant():
    j = np.arange(BLOCK)[:, None]
    s = np.arange(BLOCK)[None, :]
    tri = (j >= s).astype(np.float32)
    zero = np.zeros_like(tri)
    return jnp.asarray(np.block([[tri, zero], [zero, tri]]), jnp.bfloat16)


def _row_total(sums):
    return jnp.broadcast_to(sums[:, :1], sums.shape)


def _softplus(z):
    return jnp.maximum(z, 0.0) + jnp.log(1.0 + jnp.exp2(jnp.abs(z) * NEG_LOG2E))


def _stick_scores_clamped(lhs, k_ref, blk):
    nk = STICK_FIRST_BLOCKS
    first = jnp.maximum(blk - (nk - 1), 0)
    start = pl.multiple_of(first * BLOCK, BLOCK)
    z = lax.dot_general(lhs, k_ref[pl.ds(start, nk * BLOCK), :], _NT,
                        preferred_element_type=jnp.float32)
    return [z[:, n * BLOCK:(n + 1) * BLOCK] for n in range(nk)], first


def _stick_masked(z, blk, first, mask_all):
    nk = STICK_FIRST_BLOCKS
    row = lax.broadcasted_iota(jnp.int32, (2 * BLOCK, BLOCK), 0) & (BLOCK - 1)
    col = lax.broadcasted_iota(jnp.int32, (2 * BLOCK, BLOCK), 1)
    out = []
    for n in range(nk):
        zn = z[n]
        if mask_all:
            m = (first + n - blk) * BLOCK + col < row
        elif n == nk - 1:
            m = col < row
        else:
            m = None
        if m is not None:
            zn = jnp.where(m, zn, -MASKED)
        out.append((zn, _softplus(zn).astype(jnp.bfloat16)))
    return out


def _stick_weights(terms):
    tail = None
    a = [None] * len(terms)
    for n in reversed(range(len(terms))):
        zn, sums = terms[n]
        e = zn - sums
        if tail is not None:
            e = e - tail
        a[n] = jnp.exp(e).astype(jnp.bfloat16)
        tail = _row_total(sums) if tail is None else tail + _row_total(sums)
    return a, tail


def _stick_values_clamped(a, v_ref, first):
    start = pl.multiple_of(first * BLOCK, BLOCK)
    v0 = v_ref[pl.ds(start, STICK_FIRST_BLOCKS * BLOCK), :]
    return jnp.dot(jnp.concatenate(a, axis=1), v0, preferred_element_type=jnp.float32)


def _stick_live(tail):
    return (jnp.min(tail) < -F32_EXP_ZERO).astype(jnp.int32)


def _stick_rest(lhs, k_ref, v_ref, c, acc, tail, first, live0):
    live = _stick_live

    def cond(carry):
        kb, alive, _, _ = carry
        return jnp.logical_and(kb >= 0, alive > 0)

    def body(carry):
        kb, _, acc, tail = carry
        st = pl.multiple_of(kb * BLOCK, BLOCK)
        z = lax.dot_general(lhs, k_ref[pl.ds(st, BLOCK), :], _NT,
                            preferred_element_type=jnp.float32)
        sums = jnp.dot(_softplus(z).astype(jnp.bfloat16), c[:BLOCK, :BLOCK],
                       preferred_element_type=jnp.float32)
        a = jnp.exp(z - sums - tail)
        acc = acc + jnp.dot(a.astype(jnp.bfloat16), v_ref[pl.ds(st, BLOCK), :],
                            preferred_element_type=jnp.float32)
        tail = tail + _row_total(sums)
        return kb - 1, live(tail), acc, tail

    _, _, acc, _ = lax.while_loop(cond, body, (first - 1, live0, acc, tail))
    return acc


def _attn_kernel(*refs, q_blocks, n_cast):
    swa_in, refs = refs[:N_SWA_IN], refs[N_SWA_IN:]
    q_ref, k_ref, v_ref, c_ref = refs[:4]
    w_refs = refs[4:4 + n_cast]
    oa_ref, o_ref = refs[4 + n_cast:6 + n_cast]
    wo_refs = refs[6 + n_cast:]
    it = pl.program_id(1)
    step = pl.program_id(0) * pl.num_programs(1) + it
    lane = lax.broadcasted_iota(jnp.int32, (BLOCK, LANES), 1)
    left = lane < HEAD_DIM
    c = c_ref[...]

    def run(first_step):
        for w_ref, wo_ref in zip(w_refs, wo_refs):
            wo_ref[...] = w_ref[...].astype(wo_ref.dtype)
        blks = [it * q_blocks + g for g in range(q_blocks)]
        lhss = []
        for g in range(q_blocks):
            q = q_ref[g * BLOCK:(g + 1) * BLOCK, :] * jnp.bfloat16(SCALE)
            zero = jnp.zeros_like(q)
            lhss.append(jnp.concatenate([jnp.where(left, q, zero), jnp.where(left, zero, q)], axis=0))
        _swa_finish(step, swa_in[0], _swa_scores(*swa_in[1:]), oa_ref)
        nk = STICK_FIRST_BLOCKS
        clamped = [_stick_scores_clamped(lhss[g], k_ref, blks[g]) for g in range(q_blocks)]
        z = [zs for zs, _ in clamped]
        firsts = [first for _, first in clamped]
        masked = [_stick_masked(z[g], blks[g], firsts[g], first_step and g < nk - 1)
                  for g in range(q_blocks)]
        sps = [sp for m in masked for _, sp in m]
        sums = jnp.dot(
            jnp.concatenate([jnp.concatenate(sps[t:t + 2], axis=1) for t in range(0, len(sps), 2)],
                            axis=0), c, preferred_element_type=jnp.float32)
        rows = 2 * BLOCK

        def block_sums(t):
            return sums[(t // 2) * rows:(t // 2 + 1) * rows, (t % 2) * BLOCK:(t % 2 + 1) * BLOCK]

        terms = [[(masked[g][n][0], block_sums(g * nk + n)) for n in range(nk)]
                 for g in range(q_blocks)]
        weights = [_stick_weights(terms[g]) for g in range(q_blocks)]
        accs = [_stick_values_clamped(weights[g][0], v_ref, firsts[g]) for g in range(q_blocks)]
        lives = [_stick_live(tail) for _, tail in weights]
        for g in range(q_blocks):
            acc = _stick_rest(lhss[g], k_ref, v_ref, c, accs[g], weights[g][1], firsts[g], lives[g])
            o_ref[g * BLOCK:(g + 1) * BLOCK, :] = jnp.where(
                left, acc[:BLOCK], acc[BLOCK:]).astype(o_ref.dtype)

    assert q_blocks >= STICK_FIRST_BLOCKS - 1
    pl.when(it == 0)(lambda: run(True))
    pl.when(it > 0)(lambda: run(False))


def _cast_block_spec(shape, steps, n_inner):
    rows, cols = shape
    for n_cb in range(1, steps + 1):
        n_rb = steps // n_cb
        if steps % n_cb == 0 and rows % n_rb == 0 and cols % n_cb == 0 \
                and (rows // n_rb) % BF16_ROWS == 0 and (cols // n_cb) % LANES == 0:
            return pl.BlockSpec(
                (rows // n_rb, cols // n_cb),
                lambda b, i: ((b * n_inner + i) // n_cb, (b * n_inner + i) % n_cb))
    raise ValueError(f"cannot tile {shape} into {steps} blocks")


def _attention(proj, sinks, weights, *, q_blocks=8):
    s = proj.shape[0]
    tq = q_blocks * BLOCK
    n_q = s // tq
    steps = PAIRS_B * n_q
    assert steps == s // BLOCK
    c = _stick_constant()
    qb_blk = QB_OFF // LANES
    kb_blk = KB_OFF // LANES
    vb_blk = VB_OFF // LANES
    step_of = lambda b, i: b * n_q + i
    w_specs = [_cast_block_spec(w.shape, steps, n_q) for w in weights]
    outs = pl.pallas_call(
        functools.partial(_attn_kernel, q_blocks=q_blocks, n_cast=len(weights)),
        grid=(PAIRS_B, n_q),
        in_specs=[
            *_swa_in_specs(step_of),
            pl.BlockSpec((tq, LANES), lambda b, i: (i, qb_blk + b)),
            pl.BlockSpec((s, LANES), lambda b, i: (0, kb_blk + b)),
            pl.BlockSpec((s, LANES), lambda b, i: (0, vb_blk + b)),
            pl.BlockSpec(c.shape, lambda b, i: (0, 0)),
            *w_specs,
        ],
        out_specs=[pl.BlockSpec((BLOCK, A_Q_W), lambda b, i: (step_of(b, i), 0)),
                   pl.BlockSpec((tq, LANES), lambda b, i: (i, b)), *w_specs],
        out_shape=[jax.ShapeDtypeStruct((s, A_Q_W), jnp.bfloat16),
                   jax.ShapeDtypeStruct((s, B_W), jnp.bfloat16)]
        + [jax.ShapeDtypeStruct(w.shape, jnp.bfloat16) for w in weights],
        compiler_params=pltpu.CompilerParams(
            dimension_semantics=("arbitrary", "arbitrary"), vmem_limit_bytes=VMEM_LIMIT),
        name="attention",
    )(sinks, *([proj] * 5), _half_swap_matrix(A_KV_W), proj, proj, proj, c, *weights)
    return outs[0], outs[1], outs[2:]


def _merge_kernel(*refs, n_chunks):
    oa_ref, ob_ref = refs[0], refs[1]
    ga_refs = refs[2:2 + n_chunks]
    gb_refs = refs[2 + n_chunks:2 + 2 * n_chunks]
    x_ref, wa_ref, wb_ref, wo_ref, o_ref, m_ref = refs[2 + 2 * n_chunks:]
    tn = o_ref.shape[1] // n_chunks
    for c in range(n_chunks):
        sl = slice(c * tn, (c + 1) * tn)
        ya = jnp.dot(oa_ref[...], wa_ref[:, sl], preferred_element_type=jnp.float32)
        yb = jnp.dot(ob_ref[...], wb_ref[:, sl], preferred_element_type=jnp.float32)
        ga = jax.nn.sigmoid(ga_refs[c][...].astype(jnp.float32))
        gb = jax.nn.sigmoid(gb_refs[c][...].astype(jnp.float32))
        m_ref[:, sl] = (ga * ya + gb * yb).astype(m_ref.dtype)
    for c in range(n_chunks):
        sl = slice(c * tn, (c + 1) * tn)
        o_ref[:, sl] = x_ref[:, sl] + jnp.dot(m_ref[...], wo_ref[:, sl],
                                              preferred_element_type=jnp.float32)


def _merge(oa, ob, proj, x, wa, wb, wo, *, tm=512, tn=512):
    s, d = x.shape
    n_chunks = d // tn
    const = lambda i: (0, 0)
    gate_specs = [pl.BlockSpec((tm, tn), functools.partial(lambda i, blk: (i, blk), blk=off // tn + c))
                  for off in (GA_OFF, GB_OFF) for c in range(n_chunks)]
    return pl.pallas_call(
        functools.partial(_merge_kernel, n_chunks=n_chunks),
        grid=(s // tm,),
        in_specs=[
            pl.BlockSpec((tm, A_Q_W), lambda i: (i, 0)),
            pl.BlockSpec((tm, B_W), lambda i: (i, 0)),
            *gate_specs,
            pl.BlockSpec((tm, d), lambda i: (i, 0)),
            pl.BlockSpec(wa.shape, const),
            pl.BlockSpec(wb.shape, const),
            pl.BlockSpec(wo.shape, const),
        ],
        out_specs=pl.BlockSpec((tm, d), lambda i: (i, 0)),
        out_shape=jax.ShapeDtypeStruct((s, d), jnp.float32),
        scratch_shapes=[pltpu.VMEM((tm, d), jnp.bfloat16)],
        compiler_params=pltpu.CompilerParams(
            dimension_semantics=("arbitrary",), vmem_limit_bytes=VMEM_LIMIT),
        name="merge",
    )(oa, ob, *([proj] * (2 * n_chunks)), x, wa, wb, wo)


def _ffn_kernel(x_hbm, g_ref, wg_ref, wu_ref, wd_ref, gf_ref, o_ref, xbuf, sem, h_ref, *,
                row_chunk):
    f = pl.program_id(1)

    def start_row_block():
        _rmsnorm_rows(xbuf, h_ref, g_ref[...], row_chunk)
        o_ref[...] = xbuf[...]

    _consume_row_block(x_hbm, xbuf, sem, start_row_block)

    h = h_ref[...]
    gate = jnp.dot(h, wg_ref[...], preferred_element_type=jnp.float32)
    up = jnp.dot(h, wu_ref[...], preferred_element_type=jnp.float32)
    act = (gate * jax.nn.sigmoid(gate) * up).astype(jnp.bfloat16)
    o_ref[...] += jnp.dot(act, wd_ref[...], preferred_element_type=jnp.float32)

    @pl.when(f == pl.num_programs(1) - 1)
    def _():
        _rmsnorm_rows(o_ref, o_ref, gf_ref[...], row_chunk)


def _ffn(x1, g, w_in, w_down, gf, *, tm=1024, tf=512):
    s, d = x1.shape
    nf = D_FF // tf
    return pl.pallas_call(
        functools.partial(_ffn_kernel, row_chunk=NORM_ROWS),
        grid=(s // tm, nf),
        in_specs=[
            pl.BlockSpec(memory_space=pl.ANY),
            pl.BlockSpec((1, d), lambda i, f: (0, 0)),
            pl.BlockSpec((d, tf), lambda i, f: (0, f)),
            pl.BlockSpec((d, tf), lambda i, f: (0, f + nf)),
            pl.BlockSpec((tf, d), lambda i, f: (f, 0)),
            pl.BlockSpec((1, d), lambda i, f: (0, 0)),
        ],
        out_specs=pl.BlockSpec((tm, d), lambda i, f: (i, 0)),
        out_shape=jax.ShapeDtypeStruct((s, d), jnp.float32),
        scratch_shapes=[pltpu.VMEM((tm, d), x1.dtype), pltpu.SemaphoreType.DMA(()),
                        pltpu.VMEM((tm, d), jnp.bfloat16)],
        compiler_params=pltpu.CompilerParams(
            dimension_semantics=("arbitrary", "arbitrary"), vmem_limit_bytes=VMEM_LIMIT),
        name="ffn",
    )(x1, g, w_in, w_in, w_down, gf)


def kernel(x, norm_mix_g, w_in, sink_logits, w_branch_a, w_branch_b, w_out,
           norm_ffn_g, w_ffn_in, w_ffn_down, norm_final_g):
    b, s, d = x.shape
    assert (b, s, d) == (1, SEQ, D_MODEL) and w_in.shape[0] == 1
    x2 = x.reshape(s, d)
    proj = _in_proj(x2, norm_mix_g[0].reshape(1, d), w_in[0])
    f32_weights = [w_branch_a[0], w_branch_b[0], w_out[0], w_ffn_in[0], w_ffn_down[0]]
    oa, ob, (wa, wb, wo, wfi, wfd) = _attention(proj, sink_logits[0], f32_weights)
    x1 = _merge(oa, ob, proj, x2, wa, wb, wo)
    out = _ffn(x1, norm_ffn_g[0].reshape(1, d), wfi, wfd, norm_final_g.reshape(1, d))
    return out.reshape(b, s, d)
```

```python
import functools
import math

import numpy as np
import jax
import jax.numpy as jnp
from jax import lax
from jax.experimental import pallas as pl
from jax.experimental.pallas import tpu as pltpu

D_MODEL = 2048
SEQ = 8192
HEAD_DIM = 64
A_Q_HEADS = 16
A_KV_HEADS = 4
A_GROUP = A_Q_HEADS // A_KV_HEADS
WINDOW = 128
B_HEADS = 16
BLOCK = 128
D_FF = 5632
EPS = 1e-6
SCALE = 1.0 / math.sqrt(HEAD_DIM)

A_Q_W = A_Q_HEADS * HEAD_DIM
A_KV_W = A_KV_HEADS * HEAD_DIM
B_W = B_HEADS * HEAD_DIM
IN_WIDTH = A_Q_W + 2 * A_KV_W + 3 * B_W + 2 * D_MODEL

LANES = 128
BF16_ROWS = 16
PAIRS_B = B_HEADS // 2

QA_OFF = 0
KA_OFF = QA_OFF + A_Q_W
VA_OFF = KA_OFF + A_KV_W
QB_OFF = VA_OFF + A_KV_W
KB_OFF = QB_OFF + B_W
VB_OFF = KB_OFF + B_W
GA_OFF = VB_OFF + B_W
GB_OFF = GA_OFF + D_MODEL

F32_EXP_ZERO = -104.0

VMEM_LIMIT = 56 * 1024 * 1024
NORM_ROWS = 128

_NT = (((1,), (1,)), ((), ()))


def _rmsnorm_rows(src_ref, dst_ref, g, row_chunk):
    def body(c, _):
        r = pl.multiple_of(c * row_chunk, row_chunk)
        x = src_ref[pl.ds(r, row_chunk), :]
        inv = lax.rsqrt(jnp.mean(x * x, axis=-1, keepdims=True) + EPS)
        dst_ref[pl.ds(r, row_chunk), :] = (x * inv * g).astype(dst_ref.dtype)
        return 0

    lax.fori_loop(0, src_ref.shape[0] // row_chunk, body, 0)


def _consume_row_block(x_hbm, xbuf, sem, consume):
    i = pl.program_id(0)
    rows = xbuf.shape[0]

    def copy(blk):
        return pltpu.make_async_copy(x_hbm.at[pl.ds(blk * rows, rows), :], xbuf, sem)

    @pl.when(pl.program_id(1) == 0)
    def _():
        @pl.when(i == 0)
        def _():
            copy(0).start()

        copy(i).wait()
        consume()

        @pl.when(i + 1 < pl.num_programs(0))
        def _():
            copy(i + 1).start()


def _in_proj_kernel(x_hbm, g_ref, w_ref, o_ref, xbuf, sem, h_ref, *, row_chunk):
    _consume_row_block(x_hbm, xbuf, sem,
                       lambda: _rmsnorm_rows(xbuf, h_ref, g_ref[...], row_chunk))
    o_ref[...] = jnp.dot(h_ref[...], w_ref[...].astype(jnp.bfloat16),
                         preferred_element_type=jnp.float32).astype(o_ref.dtype)


def _in_proj(x, g, w, *, tm=2048, tn=512):
    s, d = x.shape
    n = w.shape[1]
    return pl.pallas_call(
        functools.partial(_in_proj_kernel, row_chunk=NORM_ROWS),
        grid=(s // tm, n // tn),
        in_specs=[
            pl.BlockSpec(memory_space=pl.ANY),
            pl.BlockSpec((1, d), lambda i, j: (0, 0)),
            pl.BlockSpec((d, tn), lambda i, j: (0, j)),
        ],
        out_specs=pl.BlockSpec((tm, tn), lambda i, j: (i, j)),
        out_shape=jax.ShapeDtypeStruct((s, n), jnp.bfloat16),
        scratch_shapes=[pltpu.VMEM((tm, d), x.dtype), pltpu.SemaphoreType.DMA(()),
                        pltpu.VMEM((tm, d), jnp.bfloat16)],
        compiler_params=pltpu.CompilerParams(
            dimension_semantics=("arbitrary", "arbitrary"),
            vmem_limit_bytes=VMEM_LIMIT),
        name="in_proj",
    )(x, g, w)


def _half_swap_matrix(n):
    c = np.arange(n)
    p = np.zeros((n, n), np.float32)
    p[c, c ^ HEAD_DIM] = 1.0
    return jnp.asarray(p, jnp.bfloat16)


def _swa_scores(q_ref, kp_ref, kc_ref, vp_ref, vc_ref, p_ref):
    k = jnp.concatenate([kp_ref[...], kc_ref[...]], axis=0)
    v = jnp.concatenate([vp_ref[...], vc_ref[...]], axis=0)
    k_sw = jnp.dot(k, p_ref[...], preferred_element_type=jnp.float32).astype(k.dtype)
    v_sw = jnp.dot(v, p_ref[...], preferred_element_type=jnp.float32).astype(v.dtype)
    lane = lax.broadcasted_iota(jnp.int32, (BLOCK, LANES), 1)
    left = lane < HEAD_DIM
    out = []
    for j in range(A_KV_HEADS):
        tile = slice((j // 2) * LANES, (j // 2 + 1) * LANES)
        qa = q_ref[:, (2 * j) * LANES:(2 * j + 1) * LANES] * jnp.bfloat16(SCALE)
        qb = q_ref[:, (2 * j + 1) * LANES:(2 * j + 2) * LANES] * jnp.bfloat16(SCALE)
        zero = jnp.zeros_like(qa)
        for side in range(2):
            keep = left if side == 0 else jnp.logical_not(left)
            aligned = (j % 2) == side
            kh = (k if aligned else k_sw)[:, tile]
            vh = (v if aligned else v_sw)[:, tile]
            lhs = jnp.concatenate([jnp.where(keep, qa, zero), jnp.where(keep, qb, zero)], axis=0)
            out.append((lax.dot_general(lhs, kh, _NT, preferred_element_type=jnp.float32), vh))
    return out


def _swa_bias():
    qi = np.arange(BLOCK)[:, None]
    ki = np.arange(2 * BLOCK)[None, :]
    dist = BLOCK + qi - ki
    window = (dist >= 0) & (dist < WINDOW)
    heads = np.arange(1, A_Q_HEADS + 1, dtype=np.float32)
    slopes = np.exp2(np.float32(-8.0) * heads / np.float32(A_Q_HEADS)).astype(np.float32)
    bias = np.where(window[None], -(slopes[:, None, None] * dist[None].astype(np.float32)),
                    -np.inf).astype(np.float32)
    first = np.where((ki >= BLOCK)[None], bias, -np.inf).astype(np.float32)
    return jnp.asarray(np.stack([first, bias]))


def _swa_finish(bias_ref, sink_ref, scores, o_ref):
    lane = lax.broadcasted_iota(jnp.int32, (BLOCK, LANES), 1)
    left = lane < HEAD_DIM
    for j in range(A_KV_HEADS):
        outs = []
        for side in range(2):
            s2, vh = scores[2 * j + side]
            ps, inv_den = [], []
            for t in range(2):
                h = A_GROUP * j + 2 * t + side
                s = s2[t * BLOCK:(t + 1) * BLOCK] + bias_ref[0, h]
                sink = sink_ref[h]
                m = jnp.maximum(jnp.max(s, axis=-1, keepdims=True), sink)
                p = jnp.exp(s - m)
                den = jnp.sum(p, axis=-1, keepdims=True) + jnp.exp(sink - m)
                ps.append(p.astype(jnp.bfloat16))
                inv_den.append(1.0 / den)
            o2 = jnp.dot(jnp.concatenate(ps, axis=0), vh, preferred_element_type=jnp.float32)
            outs.append((o2[:BLOCK] * inv_den[0], o2[BLOCK:] * inv_den[1]))
        for t in range(2):
            out = jnp.where(left, outs[0][t], outs[1][t])
            o_ref[:, (2 * j + t) * LANES:(2 * j + t + 1) * LANES] = out.astype(o_ref.dtype)


N_SWA_IN = 8


def _swa_in_specs(step_of):
    qa_blk = QA_OFF // A_Q_W
    ka_blk = KA_OFF // A_KV_W
    va_blk = VA_OFF // A_KV_W
    cur = lambda *g: step_of(*g)
    prev = lambda *g: jnp.maximum(step_of(*g) - 1, 0)
    return [
        pl.BlockSpec(memory_space=pltpu.SMEM),
        pl.BlockSpec((BLOCK, A_Q_W), lambda *g: (cur(*g), qa_blk)),
        pl.BlockSpec((BLOCK, A_KV_W), lambda *g: (prev(*g), ka_blk)),
        pl.BlockSpec((BLOCK, A_KV_W), lambda *g: (cur(*g), ka_blk)),
        pl.BlockSpec((BLOCK, A_KV_W), lambda *g: (prev(*g), va_blk)),
        pl.BlockSpec((BLOCK, A_KV_W), lambda *g: (cur(*g), va_blk)),
        pl.BlockSpec((A_KV_W, A_KV_W), lambda *g: (0, 0)),
        pl.BlockSpec((1, A_Q_HEADS, BLOCK, 2 * BLOCK),
                     lambda *g: (jnp.minimum(step_of(*g), 1), 0, 0, 0)),
    ]


STICK_FIRST_BLOCKS = 3
NEG_LOG2E = -1.0 / math.log(2.0)
MASKED = 1e30


def _stick_constant():
    j = np.arange(BLOCK)[:, None]
    s = np.arange(BLOCK)[None, :]
    tri = (j >= s).astype(np.float32)
    zero = np.zeros_like(tri)
    return jnp.asarray(np.block([[tri, zero], [zero, tri]]), jnp.bfloat16)


def _row_total(sums):
    return jnp.broadcast_to(sums[:, :1], sums.shape)


def _softplus(z):
    return jnp.maximum(z, 0.0) + jnp.log(1.0 + jnp.exp2(jnp.abs(z) * NEG_LOG2E))


def _stick_scores_clamped(lhs, k_ref, blk):
    nk = STICK_FIRST_BLOCKS
    first = jnp.maximum(blk - (nk - 1), 0)
    start = pl.multiple_of(first * BLOCK, BLOCK)
    z = lax.dot_general(lhs, k_ref[pl.ds(start, nk * BLOCK), :], _NT,
                        preferred_element_type=jnp.float32)
    return [z[:, n * BLOCK:(n + 1) * BLOCK] for n in range(nk)], first


def _stick_masked(z, blk, first, mask_all):
    nk = STICK_FIRST_BLOCKS
    row = lax.broadcasted_iota(jnp.int32, (2 * BLOCK, BLOCK), 0) & (BLOCK - 1)
    col = lax.broadcasted_iota(jnp.int32, (2 * BLOCK, BLOCK), 1)
    out = []
    for n in range(nk):
        zn = z[n]
        if mask_all:
            m = (first + n - blk) * BLOCK + col < row
        elif n == nk - 1:
            m = col < row
        else:
            m = None
        if m is not None:
            zn = jnp.where(m, zn, -MASKED)
        out.append((zn, _softplus(zn).astype(jnp.bfloat16)))
    return out


def _stick_weights(terms):
    tail = None
    a = [None] * len(terms)
    for n in reversed(range(len(terms))):
        zn, sums = terms[n]
        e = zn - sums
        if tail is not None:
            e = e - tail
        a[n] = jnp.exp(e).astype(jnp.bfloat16)
        tail = _row_total(sums) if tail is None else tail + _row_total(sums)
    return a, tail


def _stick_values_clamped(a, v_ref, first):
    start = pl.multiple_of(first * BLOCK, BLOCK)
    v0 = v_ref[pl.ds(start, STICK_FIRST_BLOCKS * BLOCK), :]
    return jnp.dot(jnp.concatenate(a, axis=1), v0, preferred_element_type=jnp.float32)


def _stick_live(tail):
    return (jnp.min(tail) < -F32_EXP_ZERO).astype(jnp.int32)


def _stick_rest(lhs, k_ref, v_ref, c, acc, tail, first, live0):
    live = _stick_live

    def cond(carry):
        kb, alive, _, _ = carry
        return jnp.logical_and(kb >= 0, alive > 0)

    def body(carry):
        kb, _, acc, tail = carry
        st = pl.multiple_of(kb * BLOCK, BLOCK)
        z = lax.dot_general(lhs, k_ref[pl.ds(st, BLOCK), :], _NT,
                            preferred_element_type=jnp.float32)
        sums = jnp.dot(_softplus(z).astype(jnp.bfloat16), c[:BLOCK, :BLOCK],
                       preferred_element_type=jnp.float32)
        a = jnp.exp(z - sums - tail)
        acc = acc + jnp.dot(a.astype(jnp.bfloat16), v_ref[pl.ds(st, BLOCK), :],
                            preferred_element_type=jnp.float32)
        tail = tail + _row_total(sums)
        return kb - 1, live(tail), acc, tail

    _, _, acc, _ = lax.while_loop(cond, body, (first - 1, live0, acc, tail))
    return acc


def _attn_kernel(*refs, q_blocks, n_cast):
    swa_in, refs = refs[:N_SWA_IN], refs[N_SWA_IN:]
    q_ref, k_ref, v_ref, c_ref = refs[:4]
    w_refs = refs[4:4 + n_cast]
    oa_ref, o_ref = refs[4 + n_cast:6 + n_cast]
    wo_refs = refs[6 + n_cast:]
    it = pl.program_id(1)
    lane = lax.broadcasted_iota(jnp.int32, (BLOCK, LANES), 1)
    left = lane < HEAD_DIM
    c = c_ref[...]

    def run(first_step):
        for w_ref, wo_ref in zip(w_refs, wo_refs):
            wo_ref[...] = w_ref[...].astype(wo_ref.dtype)
        blks = [it * q_blocks + g for g in range(q_blocks)]
        lhss = []
        for g in range(q_blocks):
            q = q_ref[g * BLOCK:(g + 1) * BLOCK, :] * jnp.bfloat16(SCALE)
            zero = jnp.zeros_like(q)
            lhss.append(jnp.concatenate([jnp.where(left, q, zero), jnp.where(left, zero, q)], axis=0))
        _swa_finish(swa_in[7], swa_in[0], _swa_scores(*swa_in[1:7]), oa_ref)
        nk = STICK_FIRST_BLOCKS
        clamped = [_stick_scores_clamped(lhss[g], k_ref, blks[g]) for g in range(q_blocks)]
        z = [zs for zs, _ in clamped]
        firsts = [first for _, first in clamped]
        masked = [_stick_masked(z[g], blks[g], firsts[g], first_step and g < nk - 1)
                  for g in range(q_blocks)]
        sps = [sp for m in masked for _, sp in m]
        sums = jnp.dot(
            jnp.concatenate([jnp.concatenate(sps[t:t + 2], axis=1) for t in range(0, len(sps), 2)],
                            axis=0), c, preferred_element_type=jnp.float32)
        rows = 2 * BLOCK

        def block_sums(t):
            return sums[(t // 2) * rows:(t // 2 + 1) * rows, (t % 2) * BLOCK:(t % 2 + 1) * BLOCK]

        terms = [[(masked[g][n][0], block_sums(g * nk + n)) for n in range(nk)]
                 for g in range(q_blocks)]
        weights = [_stick_weights(terms[g]) for g in range(q_blocks)]
        accs = [_stick_values_clamped(weights[g][0], v_ref, firsts[g]) for g in range(q_blocks)]
        lives = [_stick_live(tail) for _, tail in weights]
        for g in range(q_blocks):
            acc = _stick_rest(lhss[g], k_ref, v_ref, c, accs[g], weights[g][1], firsts[g], lives[g])
            o_ref[g * BLOCK:(g + 1) * BLOCK, :] = jnp.where(
                left, acc[:BLOCK], acc[BLOCK:]).astype(o_ref.dtype)

    assert q_blocks >= STICK_FIRST_BLOCKS - 1
    pl.when(it == 0)(lambda: run(True))
    pl.when(it > 0)(lambda: run(False))


def _cast_block_spec(shape, steps, n_inner):
    rows, cols = shape
    for n_cb in range(1, steps + 1):
        n_rb = steps // n_cb
        if steps % n_cb == 0 and rows % n_rb == 0 and cols % n_cb == 0 \
                and (rows // n_rb) % BF16_ROWS == 0 and (cols // n_cb) % LANES == 0:
            return pl.BlockSpec(
                (rows // n_rb, cols // n_cb),
                lambda b, i: ((b * n_inner + i) // n_cb, (b * n_inner + i) % n_cb))
    raise ValueError(f"cannot tile {shape} into {steps} blocks")


def _attention(proj, sinks, weights, *, q_blocks=8):
    s = proj.shape[0]
    tq = q_blocks * BLOCK
    n_q = s // tq
    steps = PAIRS_B * n_q
    assert steps == s // BLOCK
    c = _stick_constant()
    qb_blk = QB_OFF // LANES
    kb_blk = KB_OFF // LANES
    vb_blk = VB_OFF // LANES
    step_of = lambda b, i: b * n_q + i
    w_specs = [_cast_block_spec(w.shape, steps, n_q) for w in weights]
    outs = pl.pallas_call(
        functools.partial(_attn_kernel, q_blocks=q_blocks, n_cast=len(weights)),
        grid=(PAIRS_B, n_q),
        in_specs=[
            *_swa_in_specs(step_of),
            pl.BlockSpec((tq, LANES), lambda b, i: (i, qb_blk + b)),
            pl.BlockSpec((s, LANES), lambda b, i: (0, kb_blk + b)),
            pl.BlockSpec((s, LANES), lambda b, i: (0, vb_blk + b)),
            pl.BlockSpec(c.shape, lambda b, i: (0, 0)),
            *w_specs,
        ],
        out_specs=[pl.BlockSpec((BLOCK, A_Q_W), lambda b, i: (step_of(b, i), 0)),
                   pl.BlockSpec((tq, LANES), lambda b, i: (i, b)), *w_specs],
        out_shape=[jax.ShapeDtypeStruct((s, A_Q_W), jnp.bfloat16),
                   jax.ShapeDtypeStruct((s, B_W), jnp.bfloat16)]
        + [jax.ShapeDtypeStruct(w.shape, jnp.bfloat16) for w in weights],
        compiler_params=pltpu.CompilerParams(
            dimension_semantics=("arbitrary", "arbitrary"), vmem_limit_bytes=VMEM_LIMIT),
        name="attention",
    )(sinks, *([proj] * 5), _half_swap_matrix(A_KV_W), _swa_bias(), proj, proj, proj, c, *weights)
    return outs[0], outs[1], outs[2:]


def _merge_kernel(*refs, n_chunks):
    oa_ref, ob_ref = refs[0], refs[1]
    ga_refs = refs[2:2 + n_chunks]
    gb_refs = refs[2 + n_chunks:2 + 2 * n_chunks]
    x_ref, wa_ref, wb_ref, wo_ref, o_ref, m_ref = refs[2 + 2 * n_chunks:]
    tn = o_ref.shape[1] // n_chunks
    for c in range(n_chunks):
        sl = slice(c * tn, (c + 1) * tn)
        ya = jnp.dot(oa_ref[...], wa_ref[:, sl], preferred_element_type=jnp.float32)
        yb = jnp.dot(ob_ref[...], wb_ref[:, sl], preferred_element_type=jnp.float32)
        ga = jax.nn.sigmoid(ga_refs[c][...].astype(jnp.float32))
        gb = jax.nn.sigmoid(gb_refs[c][...].astype(jnp.float32))
        m_ref[:, sl] = (ga * ya + gb * yb).astype(m_ref.dtype)
    for c in range(n_chunks):
        sl = slice(c * tn, (c + 1) * tn)
        o_ref[:, sl] = x_ref[:, sl] + jnp.dot(m_ref[...], wo_ref[:, sl],
                                              preferred_element_type=jnp.float32)


def _merge(oa, ob, proj, x, wa, wb, wo, *, tm=512, tn=512):
    s, d = x.shape
    n_chunks = d // tn
    const = lambda i: (0, 0)
    gate_specs = [pl.BlockSpec((tm, tn), functools.partial(lambda i, blk: (i, blk), blk=off // tn + c))
                  for off in (GA_OFF, GB_OFF) for c in range(n_chunks)]
    return pl.pallas_call(
        functools.partial(_merge_kernel, n_chunks=n_chunks),
        grid=(s // tm,),
        in_specs=[
            pl.BlockSpec((tm, A_Q_W), lambda i: (i, 0)),
            pl.BlockSpec((tm, B_W), lambda i: (i, 0)),
            *gate_specs,
            pl.BlockSpec((tm, d), lambda i: (i, 0)),
            pl.BlockSpec(wa.shape, const),
            pl.BlockSpec(wb.shape, const),
            pl.BlockSpec(wo.shape, const),
        ],
        out_specs=pl.BlockSpec((tm, d), lambda i: (i, 0)),
        out_shape=jax.ShapeDtypeStruct((s, d), jnp.float32),
        scratch_shapes=[pltpu.VMEM((tm, d), jnp.bfloat16)],
        compiler_params=pltpu.CompilerParams(
            dimension_semantics=("arbitrary",), vmem_limit_bytes=VMEM_LIMIT),
        name="merge",
    )(oa, ob, *([proj] * (2 * n_chunks)), x, wa, wb, wo)


def _ffn_kernel(x_hbm, g_ref, wg_ref, wu_ref, wd_ref, gf_ref, o_ref, xbuf, sem, h_ref, *,
                row_chunk):
    f = pl.program_id(1)

    def start_row_block():
        _rmsnorm_rows(xbuf, h_ref, g_ref[...], row_chunk)
        o_ref[...] = xbuf[...]

    _consume_row_block(x_hbm, xbuf, sem, start_row_block)

    h = h_ref[...]
    gate = jnp.dot(h, wg_ref[...], preferred_element_type=jnp.float32)
    up = jnp.dot(h, wu_ref[...], preferred_element_type=jnp.float32)
    act = (gate * jax.nn.sigmoid(gate) * up).astype(jnp.bfloat16)
    o_ref[...] += jnp.dot(act, wd_ref[...], preferred_element_type=jnp.float32)

    @pl.when(f == pl.num_programs(1) - 1)
    def _():
        _rmsnorm_rows(o_ref, o_ref, gf_ref[...], row_chunk)


def _ffn(x1, g, w_in, w_down, gf, *, tm=1024, tf=512):
    s, d = x1.shape
    nf = D_FF // tf
    return pl.pallas_call(
        functools.partial(_ffn_kernel, row_chunk=NORM_ROWS),
        grid=(s // tm, nf),
        in_specs=[
            pl.BlockSpec(memory_space=pl.ANY),
            pl.BlockSpec((1, d), lambda i, f: (0, 0)),
            pl.BlockSpec((d, tf), lambda i, f: (0, f)),
            pl.BlockSpec((d, tf), lambda i, f: (0, f + nf)),
            pl.BlockSpec((tf, d), lambda i, f: (f, 0)),
            pl.BlockSpec((1, d), lambda i, f: (0, 0)),
        ],
        out_specs=pl.BlockSpec((tm, d), lambda i, f: (i, 0)),
        out_shape=jax.ShapeDtypeStruct((s, d), jnp.float32),
        scratch_shapes=[pltpu.VMEM((tm, d), x1.dtype), pltpu.SemaphoreType.DMA(()),
                        pltpu.VMEM((tm, d), jnp.bfloat16)],
        compiler_params=pltpu.CompilerParams(
            dimension_semantics=("arbitrary", "arbitrary"), vmem_limit_bytes=VMEM_LIMIT),
        name="ffn",
    )(x1, g, w_in, w_in, w_down, gf)


def kernel(x, norm_mix_g, w_in, sink_logits, w_branch_a, w_branch_b, w_out,
           norm_ffn_g, w_ffn_in, w_ffn_down, norm_final_g):
    b, s, d = x.shape
    assert (b, s, d) == (1, SEQ, D_MODEL) and w_in.shape[0] == 1
    x2 = x.reshape(s, d)
    proj = _in_proj(x2, norm_mix_g[0].reshape(1, d), w_in[0])
    f32_weights = [w_branch_a[0], w_branch_b[0], w_out[0], w_ffn_in[0], w_ffn_down[0]]
    oa, ob, (wa, wb, wo, wfi, wfd) = _attention(proj, sink_logits[0], f32_weights)
    x1 = _merge(oa, ob, proj, x2, wa, wb, wo)
    out = _ffn(x1, norm_ffn_g[0].reshape(1, d), wfi, wfd, norm_final_g.reshape(1, d))
    return out.reshape(b, s, d)
```

```python
import functools
import math

import numpy as np
import jax
import jax.numpy as jnp
from jax import lax
from jax.experimental import pallas as pl
from jax.experimental.pallas import tpu as pltpu

D_MODEL = 2048
SEQ = 8192
HEAD_DIM = 64
A_Q_HEADS = 16
A_KV_HEADS = 4
A_GROUP = A_Q_HEADS // A_KV_HEADS
WINDOW = 128
B_HEADS = 16
BLOCK = 128
D_FF = 5632
EPS = 1e-6
SCALE = 1.0 / math.sqrt(HEAD_DIM)

A_Q_W = A_Q_HEADS * HEAD_DIM
A_KV_W = A_KV_HEADS * HEAD_DIM
B_W = B_HEADS * HEAD_DIM
IN_WIDTH = A_Q_W + 2 * A_KV_W + 3 * B_W + 2 * D_MODEL

LANES = 128
BF16_ROWS = 16
PAIRS_B = B_HEADS // 2

QA_OFF = 0
KA_OFF = QA_OFF + A_Q_W
VA_OFF = KA_OFF + A_KV_W
QB_OFF = VA_OFF + A_KV_W
KB_OFF = QB_OFF + B_W
VB_OFF = KB_OFF + B_W
GA_OFF = VB_OFF + B_W
GB_OFF = GA_OFF + D_MODEL

F32_EXP_ZERO = -104.0

VMEM_LIMIT = 56 * 1024 * 1024
NORM_ROWS = 128
FFN_TILE = 512

_NT = (((1,), (1,)), ((), ()))


def _rmsnorm_rows(src_ref, dst_ref, g, row_chunk):
    def body(c, _):
        r = pl.multiple_of(c * row_chunk, row_chunk)
        x = src_ref[pl.ds(r, row_chunk), :]
        inv = lax.rsqrt(jnp.mean(x * x, axis=-1, keepdims=True) + EPS)
        dst_ref[pl.ds(r, row_chunk), :] = (x * inv * g).astype(dst_ref.dtype)
        return 0

    lax.fori_loop(0, src_ref.shape[0] // row_chunk, body, 0)


def _consume_row_block(x_hbm, xbuf, sem, consume):
    i = pl.program_id(0)
    rows = xbuf.shape[0]

    def copy(blk):
        return pltpu.make_async_copy(x_hbm.at[pl.ds(blk * rows, rows), :], xbuf, sem)

    @pl.when(pl.program_id(1) == 0)
    def _():
        @pl.when(i == 0)
        def _():
            copy(0).start()

        copy(i).wait()
        consume()

        @pl.when(i + 1 < pl.num_programs(0))
        def _():
            copy(i + 1).start()


def _in_proj_kernel(x_hbm, g_ref, w_ref, o_ref, xbuf, sem, h_ref, *, row_chunk):
    _consume_row_block(x_hbm, xbuf, sem,
                       lambda: _rmsnorm_rows(xbuf, h_ref, g_ref[...], row_chunk))
    o_ref[...] = jnp.dot(h_ref[...], w_ref[...].astype(jnp.bfloat16),
                         preferred_element_type=jnp.float32).astype(o_ref.dtype)


def _in_proj(x, g, w, *, tm=2048, tn=512):
    s, d = x.shape
    n = w.shape[1]
    return pl.pallas_call(
        functools.partial(_in_proj_kernel, row_chunk=NORM_ROWS),
        grid=(s // tm, n // tn),
        in_specs=[
            pl.BlockSpec(memory_space=pl.ANY),
            pl.BlockSpec((1, d), lambda i, j: (0, 0)),
            pl.BlockSpec((d, tn), lambda i, j: (0, j)),
        ],
        out_specs=pl.BlockSpec((tm, tn), lambda i, j: (i, j)),
        out_shape=jax.ShapeDtypeStruct((s, n), jnp.bfloat16),
        scratch_shapes=[pltpu.VMEM((tm, d), x.dtype), pltpu.SemaphoreType.DMA(()),
                        pltpu.VMEM((tm, d), jnp.bfloat16)],
        compiler_params=pltpu.CompilerParams(
            dimension_semantics=("arbitrary", "arbitrary"),
            vmem_limit_bytes=VMEM_LIMIT),
        name="in_proj",
    )(x, g, w)


def _half_swap_matrix(n):
    c = np.arange(n)
    p = np.zeros((n, n), np.float32)
    p[c, c ^ HEAD_DIM] = 1.0
    return jnp.asarray(p, jnp.bfloat16)


def _swa_scores(q_ref, kp_ref, kc_ref, vp_ref, vc_ref, p_ref):
    k = jnp.concatenate([kp_ref[...], kc_ref[...]], axis=0)
    v = jnp.concatenate([vp_ref[...], vc_ref[...]], axis=0)
    k_sw = jnp.dot(k, p_ref[...], preferred_element_type=jnp.float32).astype(k.dtype)
    v_sw = jnp.dot(v, p_ref[...], preferred_element_type=jnp.float32).astype(v.dtype)
    lane = lax.broadcasted_iota(jnp.int32, (BLOCK, LANES), 1)
    left = lane < HEAD_DIM
    out = []
    for j in range(A_KV_HEADS):
        tile = slice((j // 2) * LANES, (j // 2 + 1) * LANES)
        qa = q_ref[:, (2 * j) * LANES:(2 * j + 1) * LANES] * jnp.bfloat16(SCALE)
        qb = q_ref[:, (2 * j + 1) * LANES:(2 * j + 2) * LANES] * jnp.bfloat16(SCALE)
        zero = jnp.zeros_like(qa)
        for side in range(2):
            keep = left if side == 0 else jnp.logical_not(left)
            aligned = (j % 2) == side
            kh = (k if aligned else k_sw)[:, tile]
            vh = (v if aligned else v_sw)[:, tile]
            lhs = jnp.concatenate([jnp.where(keep, qa, zero), jnp.where(keep, qb, zero)], axis=0)
            out.append((lax.dot_general(lhs, kh, _NT, preferred_element_type=jnp.float32), vh))
    return out


def _swa_bias():
    qi = np.arange(BLOCK)[:, None]
    ki = np.arange(2 * BLOCK)[None, :]
    dist = BLOCK + qi - ki
    window = (dist >= 0) & (dist < WINDOW)
    heads = np.arange(1, A_Q_HEADS + 1, dtype=np.float32)
    slopes = np.exp2(np.float32(-8.0) * heads / np.float32(A_Q_HEADS)).astype(np.float32)
    bias = np.where(window[None], -(slopes[:, None, None] * dist[None].astype(np.float32)),
                    -np.inf).astype(np.float32)
    first = np.where((ki >= BLOCK)[None], bias, -np.inf).astype(np.float32)
    return jnp.asarray(np.stack([first, bias]))


def _swa_finish(bias_ref, sink_ref, scores, o_ref):
    lane = lax.broadcasted_iota(jnp.int32, (BLOCK, LANES), 1)
    left = lane < HEAD_DIM
    for j in range(A_KV_HEADS):
        outs = []
        for side in range(2):
            s2, vh = scores[2 * j + side]
            ps, inv_den = [], []
            for t in range(2):
                h = A_GROUP * j + 2 * t + side
                s = s2[t * BLOCK:(t + 1) * BLOCK] + bias_ref[0, h]
                sink = sink_ref[h]
                m = jnp.maximum(jnp.max(s, axis=-1, keepdims=True), sink)
                p = jnp.exp(s - m)
                den = jnp.sum(p, axis=-1, keepdims=True) + jnp.exp(sink - m)
                ps.append(p.astype(jnp.bfloat16))
                inv_den.append(1.0 / den)
            o2 = jnp.dot(jnp.concatenate(ps, axis=0), vh, preferred_element_type=jnp.float32)
            outs.append((o2[:BLOCK] * inv_den[0], o2[BLOCK:] * inv_den[1]))
        for t in range(2):
            out = jnp.where(left, outs[0][t], outs[1][t])
            o_ref[:, (2 * j + t) * LANES:(2 * j + t + 1) * LANES] = out.astype(o_ref.dtype)


N_SWA_IN = 8


def _swa_in_specs(step_of):
    qa_blk = QA_OFF // A_Q_W
    ka_blk = KA_OFF // A_KV_W
    va_blk = VA_OFF // A_KV_W
    cur = lambda *g: step_of(*g)
    prev = lambda *g: jnp.maximum(step_of(*g) - 1, 0)
    return [
        pl.BlockSpec(memory_space=pltpu.SMEM),
        pl.BlockSpec((BLOCK, A_Q_W), lambda *g: (cur(*g), qa_blk)),
        pl.BlockSpec((BLOCK, A_KV_W), lambda *g: (prev(*g), ka_blk)),
        pl.BlockSpec((BLOCK, A_KV_W), lambda *g: (cur(*g), ka_blk)),
        pl.BlockSpec((BLOCK, A_KV_W), lambda *g: (prev(*g), va_blk)),
        pl.BlockSpec((BLOCK, A_KV_W), lambda *g: (cur(*g), va_blk)),
        pl.BlockSpec((A_KV_W, A_KV_W), lambda *g: (0, 0)),
        pl.BlockSpec((1, A_Q_HEADS, BLOCK, 2 * BLOCK),
                     lambda *g: (jnp.minimum(step_of(*g), 1), 0, 0, 0)),
    ]


STICK_FIRST_BLOCKS = 3
NEG_LOG2E = -1.0 / math.log(2.0)
MASKED = 1e30


def _stick_constant():
    j = np.arange(BLOCK)[:, None]
    s = np.arange(BLOCK)[None, :]
    tri = (j >= s).astype(np.float32)
    zero = np.zeros_like(tri)
    return jnp.asarray(np.block([[tri, zero], [zero, tri]]), jnp.bfloat16)


def _row_total(sums):
    return jnp.broadcast_to(sums[:, :1], sums.shape)


def _softplus(z):
    return jnp.maximum(z, 0.0) + jnp.log(1.0 + jnp.exp2(jnp.abs(z) * NEG_LOG2E))


def _stick_scores_clamped(lhs, k_ref, blk):
    nk = STICK_FIRST_BLOCKS
    first = jnp.maximum(blk - (nk - 1), 0)
    start = pl.multiple_of(first * BLOCK, BLOCK)
    z = lax.dot_general(lhs, k_ref[pl.ds(start, nk * BLOCK), :], _NT,
                        preferred_element_type=jnp.float32)
    return [z[:, n * BLOCK:(n + 1) * BLOCK] for n in range(nk)], first


def _stick_masked(z, blk, first, mask_all):
    nk = STICK_FIRST_BLOCKS
    row = lax.broadcasted_iota(jnp.int32, (2 * BLOCK, BLOCK), 0) & (BLOCK - 1)
    col = lax.broadcasted_iota(jnp.int32, (2 * BLOCK, BLOCK), 1)
    out = []
    for n in range(nk):
        zn = z[n]
        if mask_all:
            m = (first + n - blk) * BLOCK + col < row
        elif n == nk - 1:
            m = col < row
        else:
            m = None
        if m is not None:
            zn = jnp.where(m, zn, -MASKED)
        out.append((zn, _softplus(zn).astype(jnp.bfloat16)))
    return out


def _stick_weights(terms):
    tail = None
    a = [None] * len(terms)
    for n in reversed(range(len(terms))):
        zn, sums = terms[n]
        e = zn - sums
        if tail is not None:
            e = e - tail
        a[n] = jnp.exp(e).astype(jnp.bfloat16)
        tail = _row_total(sums) if tail is None else tail + _row_total(sums)
    return a, tail


def _stick_values_clamped(a, v_ref, first):
    start = pl.multiple_of(first * BLOCK, BLOCK)
    v0 = v_ref[pl.ds(start, STICK_FIRST_BLOCKS * BLOCK), :]
    return jnp.dot(jnp.concatenate(a, axis=1), v0, preferred_element_type=jnp.float32)


def _stick_live(tail):
    return (jnp.min(tail) < -F32_EXP_ZERO).astype(jnp.int32)


def _stick_rest(lhs, k_ref, v_ref, c, acc, tail, first, live0):
    live = _stick_live

    def cond(carry):
        kb, alive, _, _ = carry
        return jnp.logical_and(kb >= 0, alive > 0)

    def body(carry):
        kb, _, acc, tail = carry
        st = pl.multiple_of(kb * BLOCK, BLOCK)
        z = lax.dot_general(lhs, k_ref[pl.ds(st, BLOCK), :], _NT,
                            preferred_element_type=jnp.float32)
        sums = jnp.dot(_softplus(z).astype(jnp.bfloat16), c[:BLOCK, :BLOCK],
                       preferred_element_type=jnp.float32)
        a = jnp.exp(z - sums - tail)
        acc = acc + jnp.dot(a.astype(jnp.bfloat16), v_ref[pl.ds(st, BLOCK), :],
                            preferred_element_type=jnp.float32)
        tail = tail + _row_total(sums)
        return kb - 1, live(tail), acc, tail

    _, _, acc, _ = lax.while_loop(cond, body, (first - 1, live0, acc, tail))
    return acc


def _attn_kernel(*refs, q_blocks, n_cast):
    swa_in, refs = refs[:N_SWA_IN], refs[N_SWA_IN:]
    q_ref, k_ref, v_ref, c_ref = refs[:4]
    w_refs = refs[4:4 + n_cast]
    oa_ref, o_ref = refs[4 + n_cast:6 + n_cast]
    wo_refs = refs[6 + n_cast:]
    it = pl.program_id(1)
    lane = lax.broadcasted_iota(jnp.int32, (BLOCK, LANES), 1)
    left = lane < HEAD_DIM
    c = c_ref[...]

    def run(first_step):
        for w_ref, wo_ref in zip(w_refs, wo_refs):
            if len(wo_ref.shape) == 3:
                tc = wo_ref.shape[2]
                for t in range(wo_ref.shape[0]):
                    wo_ref[t] = w_ref[:, t * tc:(t + 1) * tc].astype(wo_ref.dtype)
            else:
                wo_ref[...] = w_ref[...].astype(wo_ref.dtype)
        blks = [it * q_blocks + g for g in range(q_blocks)]
        lhss = []
        for g in range(q_blocks):
            q = q_ref[g * BLOCK:(g + 1) * BLOCK, :] * jnp.bfloat16(SCALE)
            zero = jnp.zeros_like(q)
            lhss.append(jnp.concatenate([jnp.where(left, q, zero), jnp.where(left, zero, q)], axis=0))
        _swa_finish(swa_in[7], swa_in[0], _swa_scores(*swa_in[1:7]), oa_ref)
        nk = STICK_FIRST_BLOCKS
        clamped = [_stick_scores_clamped(lhss[g], k_ref, blks[g]) for g in range(q_blocks)]
        z = [zs for zs, _ in clamped]
        firsts = [first for _, first in clamped]
        masked = [_stick_masked(z[g], blks[g], firsts[g], first_step and g < nk - 1)
                  for g in range(q_blocks)]
        sps = [sp for m in masked for _, sp in m]
        sums = jnp.dot(
            jnp.concatenate([jnp.concatenate(sps[t:t + 2], axis=1) for t in range(0, len(sps), 2)],
                            axis=0), c, preferred_element_type=jnp.float32)
        rows = 2 * BLOCK

        def block_sums(t):
            return sums[(t // 2) * rows:(t // 2 + 1) * rows, (t % 2) * BLOCK:(t % 2 + 1) * BLOCK]

        terms = [[(masked[g][n][0], block_sums(g * nk + n)) for n in range(nk)]
                 for g in range(q_blocks)]
        weights = [_stick_weights(terms[g]) for g in range(q_blocks)]
        accs = [_stick_values_clamped(weights[g][0], v_ref, firsts[g]) for g in range(q_blocks)]
        lives = [_stick_live(tail) for _, tail in weights]
        for g in range(q_blocks):
            acc = _stick_rest(lhss[g], k_ref, v_ref, c, accs[g], weights[g][1], firsts[g], lives[g])
            o_ref[g * BLOCK:(g + 1) * BLOCK, :] = jnp.where(
                left, acc[:BLOCK], acc[BLOCK:]).astype(o_ref.dtype)

    assert q_blocks >= STICK_FIRST_BLOCKS - 1
    pl.when(it == 0)(lambda: run(True))
    pl.when(it > 0)(lambda: run(False))


def _cast_block_spec(shape, steps, n_inner):
    rows, cols = shape
    for n_cb in range(1, steps + 1):
        n_rb = steps // n_cb
        if steps % n_cb == 0 and rows % n_rb == 0 and cols % n_cb == 0 \
                and (rows // n_rb) % BF16_ROWS == 0 and (cols // n_cb) % LANES == 0:
            return pl.BlockSpec(
                (rows // n_rb, cols // n_cb),
                lambda b, i: ((b * n_inner + i) // n_cb, (b * n_inner + i) % n_cb))
    raise ValueError(f"cannot tile {shape} into {steps} blocks")


def _attention(proj, sinks, weights, col_tiles, *, q_blocks=8):
    s = proj.shape[0]
    tq = q_blocks * BLOCK
    n_q = s // tq
    steps = PAIRS_B * n_q
    assert steps == s // BLOCK
    c = _stick_constant()
    qb_blk = QB_OFF // LANES
    kb_blk = KB_OFF // LANES
    vb_blk = VB_OFF // LANES
    step_of = lambda b, i: b * n_q + i
    w_specs = [_cast_block_spec(w.shape, steps, n_q) for w in weights]
    wo_specs, wo_shapes = [], []
    for w, spec, tc in zip(weights, w_specs, col_tiles):
        if tc is None:
            wo_specs.append(spec)
            wo_shapes.append(jax.ShapeDtypeStruct(w.shape, jnp.bfloat16))
        else:
            rows, cols = spec.block_shape
            assert cols == w.shape[1] and cols % tc == 0
            wo_specs.append(pl.BlockSpec((cols // tc, rows, tc), lambda b, i: (0, step_of(b, i), 0)))
            wo_shapes.append(jax.ShapeDtypeStruct((cols // tc, w.shape[0], tc), jnp.bfloat16))
    outs = pl.pallas_call(
        functools.partial(_attn_kernel, q_blocks=q_blocks, n_cast=len(weights)),
        grid=(PAIRS_B, n_q),
        in_specs=[
            *_swa_in_specs(step_of),
            pl.BlockSpec((tq, LANES), lambda b, i: (i, qb_blk + b)),
            pl.BlockSpec((s, LANES), lambda b, i: (0, kb_blk + b)),
            pl.BlockSpec((s, LANES), lambda b, i: (0, vb_blk + b)),
            pl.BlockSpec(c.shape, lambda b, i: (0, 0)),
            *w_specs,
        ],
        out_specs=[pl.BlockSpec((BLOCK, A_Q_W), lambda b, i: (step_of(b, i), 0)),
                   pl.BlockSpec((tq, LANES), lambda b, i: (i, b)), *wo_specs],
        out_shape=[jax.ShapeDtypeStruct((s, A_Q_W), jnp.bfloat16),
                   jax.ShapeDtypeStruct((s, B_W), jnp.bfloat16)] + wo_shapes,
        compiler_params=pltpu.CompilerParams(
            dimension_semantics=("arbitrary", "arbitrary"), vmem_limit_bytes=VMEM_LIMIT),
        name="attention",
    )(sinks, *([proj] * 5), _half_swap_matrix(A_KV_W), _swa_bias(), proj, proj, proj, c, *weights)
    return outs[0], outs[1], outs[2:]


def _merge_kernel(*refs, n_chunks):
    oa_ref, ob_ref = refs[0], refs[1]
    ga_refs = refs[2:2 + n_chunks]
    gb_refs = refs[2 + n_chunks:2 + 2 * n_chunks]
    x_ref, wa_ref, wb_ref, wo_ref, o_ref, m_ref = refs[2 + 2 * n_chunks:]
    tn = o_ref.shape[1] // n_chunks
    for c in range(n_chunks):
        sl = slice(c * tn, (c + 1) * tn)
        ya = jnp.dot(oa_ref[...], wa_ref[:, sl], preferred_element_type=jnp.float32)
        yb = jnp.dot(ob_ref[...], wb_ref[:, sl], preferred_element_type=jnp.float32)
        ga = jax.nn.sigmoid(ga_refs[c][...].astype(jnp.float32))
        gb = jax.nn.sigmoid(gb_refs[c][...].astype(jnp.float32))
        m_ref[:, sl] = (ga * ya + gb * yb).astype(m_ref.dtype)
    for c in range(n_chunks):
        sl = slice(c * tn, (c + 1) * tn)
        o_ref[:, sl] = x_ref[:, sl] + jnp.dot(m_ref[...], wo_ref[:, sl],
                                              preferred_element_type=jnp.float32)


def _merge(oa, ob, proj, x, wa, wb, wo, *, tm=512, tn=512):
    s, d = x.shape
    n_chunks = d // tn
    const = lambda i: (0, 0)
    gate_specs = [pl.BlockSpec((tm, tn), functools.partial(lambda i, blk: (i, blk), blk=off // tn + c))
                  for off in (GA_OFF, GB_OFF) for c in range(n_chunks)]
    return pl.pallas_call(
        functools.partial(_merge_kernel, n_chunks=n_chunks),
        grid=(s // tm,),
        in_specs=[
            pl.BlockSpec((tm, A_Q_W), lambda i: (i, 0)),
            pl.BlockSpec((tm, B_W), lambda i: (i, 0)),
            *gate_specs,
            pl.BlockSpec((tm, d), lambda i: (i, 0)),
            pl.BlockSpec(wa.shape, const),
            pl.BlockSpec(wb.shape, const),
            pl.BlockSpec(wo.shape, const),
        ],
        out_specs=pl.BlockSpec((tm, d), lambda i: (i, 0)),
        out_shape=jax.ShapeDtypeStruct((s, d), jnp.float32),
        scratch_shapes=[pltpu.VMEM((tm, d), jnp.bfloat16)],
        compiler_params=pltpu.CompilerParams(
            dimension_semantics=("arbitrary",), vmem_limit_bytes=VMEM_LIMIT),
        name="merge",
    )(oa, ob, *([proj] * (2 * n_chunks)), x, wa, wb, wo)


def _ffn_kernel(x_hbm, g_ref, wg_ref, wu_ref, wd_ref, gf_ref, o_ref, xbuf, sem, h_ref, *,
                row_chunk):
    f = pl.program_id(1)

    def start_row_block():
        _rmsnorm_rows(xbuf, h_ref, g_ref[...], row_chunk)
        o_ref[...] = xbuf[...]

    _consume_row_block(x_hbm, xbuf, sem, start_row_block)

    h = h_ref[...]
    gate = jnp.dot(h, wg_ref[0], preferred_element_type=jnp.float32)
    up = jnp.dot(h, wu_ref[0], preferred_element_type=jnp.float32)
    act = (gate * jax.nn.sigmoid(gate) * up).astype(jnp.bfloat16)
    o_ref[...] += jnp.dot(act, wd_ref[...], preferred_element_type=jnp.float32)

    @pl.when(f == pl.num_programs(1) - 1)
    def _():
        _rmsnorm_rows(o_ref, o_ref, gf_ref[...], row_chunk)


def _ffn(x1, g, w_in_tiles, w_down, gf, *, tm=1024):
    s, d = x1.shape
    tf = w_in_tiles.shape[2]
    nf = D_FF // tf
    return pl.pallas_call(
        functools.partial(_ffn_kernel, row_chunk=NORM_ROWS),
        grid=(s // tm, nf),
        in_specs=[
            pl.BlockSpec(memory_space=pl.ANY),
            pl.BlockSpec((1, d), lambda i, f: (0, 0)),
            pl.BlockSpec((1, d, tf), lambda i, f: (f, 0, 0)),
            pl.BlockSpec((1, d, tf), lambda i, f: (f + nf, 0, 0)),
            pl.BlockSpec((tf, d), lambda i, f: (f, 0)),
            pl.BlockSpec((1, d), lambda i, f: (0, 0)),
        ],
        out_specs=pl.BlockSpec((tm, d), lambda i, f: (i, 0)),
        out_shape=jax.ShapeDtypeStruct((s, d), jnp.float32),
        scratch_shapes=[pltpu.VMEM((tm, d), x1.dtype), pltpu.SemaphoreType.DMA(()),
                        pltpu.VMEM((tm, d), jnp.bfloat16)],
        compiler_params=pltpu.CompilerParams(
            dimension_semantics=("arbitrary", "arbitrary"), vmem_limit_bytes=VMEM_LIMIT),
        name="ffn",
    )(x1, g, w_in_tiles, w_in_tiles, w_down, gf)


def kernel(x, norm_mix_g, w_in, sink_logits, w_branch_a, w_branch_b, w_out,
           norm_ffn_g, w_ffn_in, w_ffn_down, norm_final_g):
    b, s, d = x.shape
    assert (b, s, d) == (1, SEQ, D_MODEL) and w_in.shape[0] == 1
    x2 = x.reshape(s, d)
    proj = _in_proj(x2, norm_mix_g[0].reshape(1, d), w_in[0])
    f32_weights = [w_branch_a[0], w_branch_b[0], w_out[0], w_ffn_in[0], w_ffn_down[0]]
    oa, ob, (wa, wb, wo, wfi, wfd) = _attention(proj, sink_logits[0], f32_weights,
                                                (None, None, None, FFN_TILE, None))
    x1 = _merge(oa, ob, proj, x2, wa, wb, wo)
    out = _ffn(x1, norm_ffn_g[0].reshape(1, d), wfi, wfd, norm_final_g.reshape(1, d))
    return out.reshape(b, s, d)
```

```python
import functools
import math

import numpy as np
import jax
import jax.numpy as jnp
from jax import lax
from jax.experimental import pallas as pl
from jax.experimental.pallas import tpu as pltpu

D_MODEL = 2048
SEQ = 8192
HEAD_DIM = 64
A_Q_HEADS = 16
A_KV_HEADS = 4
A_GROUP = A_Q_HEADS // A_KV_HEADS
WINDOW = 128
B_HEADS = 16
BLOCK = 128
D_FF = 5632
EPS = 1e-6
SCALE = 1.0 / math.sqrt(HEAD_DIM)

A_Q_W = A_Q_HEADS * HEAD_DIM
A_KV_W = A_KV_HEADS * HEAD_DIM
B_W = B_HEADS * HEAD_DIM
IN_WIDTH = A_Q_W + 2 * A_KV_W + 3 * B_W + 2 * D_MODEL

LANES = 128
BF16_ROWS = 16
PAIRS_B = B_HEADS // 2

QA_OFF = 0
KA_OFF = QA_OFF + A_Q_W
VA_OFF = KA_OFF + A_KV_W
QB_OFF = VA_OFF + A_KV_W
KB_OFF = QB_OFF + B_W
VB_OFF = KB_OFF + B_W
GA_OFF = VB_OFF + B_W
GB_OFF = GA_OFF + D_MODEL

F32_EXP_ZERO = -104.0

VMEM_LIMIT = 56 * 1024 * 1024
NORM_ROWS = 128

_NT = (((1,), (1,)), ((), ()))


def _rmsnorm_rows(src_ref, dst_ref, g, row_chunk):
    def body(c, _):
        r = pl.multiple_of(c * row_chunk, row_chunk)
        x = src_ref[pl.ds(r, row_chunk), :]
        inv = lax.rsqrt(jnp.mean(x * x, axis=-1, keepdims=True) + EPS)
        dst_ref[pl.ds(r, row_chunk), :] = (x * inv * g).astype(dst_ref.dtype)
        return 0

    lax.fori_loop(0, src_ref.shape[0] // row_chunk, body, 0)


def _consume_row_block(x_hbm, xbuf, sem, consume):
    i = pl.program_id(0)
    rows = xbuf.shape[0]

    def copy(blk):
        return pltpu.make_async_copy(x_hbm.at[pl.ds(blk * rows, rows), :], xbuf, sem)

    @pl.when(pl.program_id(1) == 0)
    def _():
        @pl.when(i == 0)
        def _():
            copy(0).start()

        copy(i).wait()
        consume()

        @pl.when(i + 1 < pl.num_programs(0))
        def _():
            copy(i + 1).start()


def _in_proj_kernel(x_hbm, g_ref, w_ref, o_ref, xbuf, sem, h_ref, *, row_chunk):
    _consume_row_block(x_hbm, xbuf, sem,
                       lambda: _rmsnorm_rows(xbuf, h_ref, g_ref[...], row_chunk))
    o_ref[...] = jnp.dot(h_ref[...], w_ref[...].astype(jnp.bfloat16),
                         preferred_element_type=jnp.float32).astype(o_ref.dtype)


def _in_proj(x, g, w, *, tm=2048, tn=512):
    s, d = x.shape
    n = w.shape[1]
    return pl.pallas_call(
        functools.partial(_in_proj_kernel, row_chunk=NORM_ROWS),
        grid=(s // tm, n // tn),
        in_specs=[
            pl.BlockSpec(memory_space=pl.ANY),
            pl.BlockSpec((1, d), lambda i, j: (0, 0)),
            pl.BlockSpec((d, tn), lambda i, j: (0, j)),
        ],
        out_specs=pl.BlockSpec((tm, tn), lambda i, j: (i, j)),
        out_shape=jax.ShapeDtypeStruct((s, n), jnp.bfloat16),
        scratch_shapes=[pltpu.VMEM((tm, d), x.dtype), pltpu.SemaphoreType.DMA(()),
                        pltpu.VMEM((tm, d), jnp.bfloat16)],
        compiler_params=pltpu.CompilerParams(
            dimension_semantics=("arbitrary", "arbitrary"),
            vmem_limit_bytes=VMEM_LIMIT),
        name="in_proj",
    )(x, g, w)


def _half_swap_matrix(n):
    c = np.arange(n)
    p = np.zeros((n, n), np.float32)
    p[c, c ^ HEAD_DIM] = 1.0
    return jnp.asarray(p, jnp.bfloat16)


def _swa_scores(q_ref, kp_ref, kc_ref, vp_ref, vc_ref, p_ref):
    k = jnp.concatenate([kp_ref[...], kc_ref[...]], axis=0)
    v = jnp.concatenate([vp_ref[...], vc_ref[...]], axis=0)
    k_sw = jnp.dot(k, p_ref[...], preferred_element_type=jnp.float32).astype(k.dtype)
    v_sw = jnp.dot(v, p_ref[...], preferred_element_type=jnp.float32).astype(v.dtype)
    lane = lax.broadcasted_iota(jnp.int32, (BLOCK, LANES), 1)
    left = lane < HEAD_DIM
    out = []
    for j in range(A_KV_HEADS):
        tile = slice((j // 2) * LANES, (j // 2 + 1) * LANES)
        qa = q_ref[:, (2 * j) * LANES:(2 * j + 1) * LANES] * jnp.bfloat16(SCALE)
        qb = q_ref[:, (2 * j + 1) * LANES:(2 * j + 2) * LANES] * jnp.bfloat16(SCALE)
        zero = jnp.zeros_like(qa)
        for side in range(2):
            keep = left if side == 0 else jnp.logical_not(left)
            aligned = (j % 2) == side
            kh = (k if aligned else k_sw)[:, tile]
            vh = (v if aligned else v_sw)[:, tile]
            lhs = jnp.concatenate([jnp.where(keep, qa, zero), jnp.where(keep, qb, zero)], axis=0)
            out.append((lax.dot_general(lhs, kh, _NT, preferred_element_type=jnp.float32), vh))
    return out


def _swa_bias():
    qi = np.arange(BLOCK)[:, None]
    ki = np.arange(2 * BLOCK)[None, :]
    dist = BLOCK + qi - ki
    window = (dist >= 0) & (dist < WINDOW)
    heads = np.arange(1, A_Q_HEADS + 1, dtype=np.float32)
    slopes = np.exp2(np.float32(-8.0) * heads / np.float32(A_Q_HEADS)).astype(np.float32)
    bias = np.where(window[None], -(slopes[:, None, None] * dist[None].astype(np.float32)),
                    -np.inf).astype(np.float32)
    first = np.where((ki >= BLOCK)[None], bias, -np.inf).astype(np.float32)
    return jnp.asarray(np.stack([first, bias]))


def _swa_finish(bias_ref, sink_ref, scores, o_ref):
    lane = lax.broadcasted_iota(jnp.int32, (BLOCK, LANES), 1)
    left = lane < HEAD_DIM
    for j in range(A_KV_HEADS):
        outs = []
        for side in range(2):
            s2, vh = scores[2 * j + side]
            ps, inv_den = [], []
            for t in range(2):
                h = A_GROUP * j + 2 * t + side
                s = s2[t * BLOCK:(t + 1) * BLOCK] + bias_ref[0, h]
                sink = sink_ref[h]
                m = jnp.maximum(jnp.max(s, axis=-1, keepdims=True), sink)
                p = jnp.exp(s - m)
                den = jnp.sum(p, axis=-1, keepdims=True) + jnp.exp(sink - m)
                ps.append(p.astype(jnp.bfloat16))
                inv_den.append(1.0 / den)
            o2 = jnp.dot(jnp.concatenate(ps, axis=0), vh, preferred_element_type=jnp.float32)
            outs.append((o2[:BLOCK] * inv_den[0], o2[BLOCK:] * inv_den[1]))
        for t in range(2):
            out = jnp.where(left, outs[0][t], outs[1][t])
            o_ref[:, (2 * j + t) * LANES:(2 * j + t + 1) * LANES] = out.astype(o_ref.dtype)


N_SWA_IN = 8


def _swa_in_specs(step_of):
    qa_blk = QA_OFF // A_Q_W
    ka_blk = KA_OFF // A_KV_W
    va_blk = VA_OFF // A_KV_W
    cur = lambda *g: step_of(*g)
    prev = lambda *g: jnp.maximum(step_of(*g) - 1, 0)
    return [
        pl.BlockSpec(memory_space=pltpu.SMEM),
        pl.BlockSpec((BLOCK, A_Q_W), lambda *g: (cur(*g), qa_blk)),
        pl.BlockSpec((BLOCK, A_KV_W), lambda *g: (prev(*g), ka_blk)),
        pl.BlockSpec((BLOCK, A_KV_W), lambda *g: (cur(*g), ka_blk)),
        pl.BlockSpec((BLOCK, A_KV_W), lambda *g: (prev(*g), va_blk)),
        pl.BlockSpec((BLOCK, A_KV_W), lambda *g: (cur(*g), va_blk)),
        pl.BlockSpec((A_KV_W, A_KV_W), lambda *g: (0, 0)),
        pl.BlockSpec((1, A_Q_HEADS, BLOCK, 2 * BLOCK),
                     lambda *g: (jnp.minimum(step_of(*g), 1), 0, 0, 0)),
    ]


STICK_FIRST_BLOCKS = 3
SOFTPLUS_CLAMP = 40.0
MASKED = 1e30


def _stick_constant():
    j = np.arange(BLOCK)[:, None]
    s = np.arange(BLOCK)[None, :]
    tri = (j >= s).astype(np.float32)
    zero = np.zeros_like(tri)
    return jnp.asarray(np.block([[tri, zero], [zero, tri]]), jnp.bfloat16)


def _row_total(sums):
    return jnp.broadcast_to(sums[:, :1], sums.shape)


def _softplus(z):
    return jnp.maximum(z, jnp.log(1.0 + jnp.exp(jnp.minimum(z, SOFTPLUS_CLAMP))))


def _stick_scores_clamped(lhs, k_ref, blk):
    nk = STICK_FIRST_BLOCKS
    first = jnp.maximum(blk - (nk - 1), 0)
    start = pl.multiple_of(first * BLOCK, BLOCK)
    z = lax.dot_general(lhs, k_ref[pl.ds(start, nk * BLOCK), :], _NT,
                        preferred_element_type=jnp.float32)
    return [z[:, n * BLOCK:(n + 1) * BLOCK] for n in range(nk)], first


def _stick_masked(z, blk, first, mask_all):
    nk = STICK_FIRST_BLOCKS
    row = lax.broadcasted_iota(jnp.int32, (2 * BLOCK, BLOCK), 0) & (BLOCK - 1)
    col = lax.broadcasted_iota(jnp.int32, (2 * BLOCK, BLOCK), 1)
    out = []
    for n in range(nk):
        zn = z[n]
        if mask_all:
            m = (first + n - blk) * BLOCK + col < row
        elif n == nk - 1:
            m = col < row
        else:
            m = None
        if m is not None:
            zn = jnp.where(m, zn, -MASKED)
        out.append((zn, _softplus(zn).astype(jnp.bfloat16)))
    return out


def _stick_weights(terms):
    tail = None
    a = [None] * len(terms)
    for n in reversed(range(len(terms))):
        zn, sums = terms[n]
        e = zn - sums
        if tail is not None:
            e = e - tail
        a[n] = jnp.exp(e).astype(jnp.bfloat16)
        tail = _row_total(sums) if tail is None else tail + _row_total(sums)
    return a, tail


def _stick_values_clamped(a, v_ref, first):
    start = pl.multiple_of(first * BLOCK, BLOCK)
    v0 = v_ref[pl.ds(start, STICK_FIRST_BLOCKS * BLOCK), :]
    return jnp.dot(jnp.concatenate(a, axis=1), v0, preferred_element_type=jnp.float32)


def _stick_live(tail):
    return (jnp.min(tail) < -F32_EXP_ZERO).astype(jnp.int32)


def _stick_rest(lhs, k_ref, v_ref, c, acc, tail, first, live0):
    live = _stick_live

    def cond(carry):
        kb, alive, _, _ = carry
        return jnp.logical_and(kb >= 0, alive > 0)

    def body(carry):
        kb, _, acc, tail = carry
        st = pl.multiple_of(kb * BLOCK, BLOCK)
        z = lax.dot_general(lhs, k_ref[pl.ds(st, BLOCK), :], _NT,
                            preferred_element_type=jnp.float32)
        sums = jnp.dot(_softplus(z).astype(jnp.bfloat16), c[:BLOCK, :BLOCK],
                       preferred_element_type=jnp.float32)
        a = jnp.exp(z - sums - tail)
        acc = acc + jnp.dot(a.astype(jnp.bfloat16), v_ref[pl.ds(st, BLOCK), :],
                            preferred_element_type=jnp.float32)
        tail = tail + _row_total(sums)
        return kb - 1, live(tail), acc, tail

    _, _, acc, _ = lax.while_loop(cond, body, (first - 1, live0, acc, tail))
    return acc


def _attn_kernel(*refs, q_blocks, n_cast):
    swa_in, refs = refs[:N_SWA_IN], refs[N_SWA_IN:]
    q_ref, k_ref, v_ref, c_ref = refs[:4]
    w_refs = refs[4:4 + n_cast]
    oa_ref, o_ref = refs[4 + n_cast:6 + n_cast]
    wo_refs = refs[6 + n_cast:]
    it = pl.program_id(1)
    lane = lax.broadcasted_iota(jnp.int32, (BLOCK, LANES), 1)
    left = lane < HEAD_DIM
    c = c_ref[...]

    def run(first_step):
        for w_ref, wo_ref in zip(w_refs, wo_refs):
            wo_ref[...] = w_ref[...].astype(wo_ref.dtype)
        blks = [it * q_blocks + g for g in range(q_blocks)]
        lhss = []
        for g in range(q_blocks):
            q = q_ref[g * BLOCK:(g + 1) * BLOCK, :] * jnp.bfloat16(SCALE)
            zero = jnp.zeros_like(q)
            lhss.append(jnp.concatenate([jnp.where(left, q, zero), jnp.where(left, zero, q)], axis=0))
        _swa_finish(swa_in[7], swa_in[0], _swa_scores(*swa_in[1:7]), oa_ref)
        nk = STICK_FIRST_BLOCKS
        clamped = [_stick_scores_clamped(lhss[g], k_ref, blks[g]) for g in range(q_blocks)]
        z = [zs for zs, _ in clamped]
        firsts = [first for _, first in clamped]
        masked = [_stick_masked(z[g], blks[g], firsts[g], first_step and g < nk - 1)
                  for g in range(q_blocks)]
        sps = [sp for m in masked for _, sp in m]
        sums = jnp.dot(
            jnp.concatenate([jnp.concatenate(sps[t:t + 2], axis=1) for t in range(0, len(sps), 2)],
                            axis=0), c, preferred_element_type=jnp.float32)
        rows = 2 * BLOCK

        def block_sums(t):
            return sums[(t // 2) * rows:(t // 2 + 1) * rows, (t % 2) * BLOCK:(t % 2 + 1) * BLOCK]

        terms = [[(masked[g][n][0], block_sums(g * nk + n)) for n in range(nk)]
                 for g in range(q_blocks)]
        weights = [_stick_weights(terms[g]) for g in range(q_blocks)]
        accs = [_stick_values_clamped(weights[g][0], v_ref, firsts[g]) for g in range(q_blocks)]
        lives = [_stick_live(tail) for _, tail in weights]
        for g in range(q_blocks):
            acc = _stick_rest(lhss[g], k_ref, v_ref, c, accs[g], weights[g][1], firsts[g], lives[g])
            o_ref[g * BLOCK:(g + 1) * BLOCK, :] = jnp.where(
                left, acc[:BLOCK], acc[BLOCK:]).astype(o_ref.dtype)

    assert q_blocks >= STICK_FIRST_BLOCKS - 1
    pl.when(it == 0)(lambda: run(True))
    pl.when(it > 0)(lambda: run(False))


def _cast_block_spec(shape, steps, n_inner):
    rows, cols = shape
    for n_cb in range(1, steps + 1):
        n_rb = steps // n_cb
        if steps % n_cb == 0 and rows % n_rb == 0 and cols % n_cb == 0 \
                and (rows // n_rb) % BF16_ROWS == 0 and (cols // n_cb) % LANES == 0:
            return pl.BlockSpec(
                (rows // n_rb, cols // n_cb),
                lambda b, i: ((b * n_inner + i) // n_cb, (b * n_inner + i) % n_cb))
    raise ValueError(f"cannot tile {shape} into {steps} blocks")


def _attention(proj, sinks, weights, *, q_blocks=8):
    s = proj.shape[0]
    tq = q_blocks * BLOCK
    n_q = s // tq
    steps = PAIRS_B * n_q
    assert steps == s // BLOCK
    c = _stick_constant()
    qb_blk = QB_OFF // LANES
    kb_blk = KB_OFF // LANES
    vb_blk = VB_OFF // LANES
    step_of = lambda b, i: b * n_q + i
    w_specs = [_cast_block_spec(w.shape, steps, n_q) for w in weights]
    outs = pl.pallas_call(
        functools.partial(_attn_kernel, q_blocks=q_blocks, n_cast=len(weights)),
        grid=(PAIRS_B, n_q),
        in_specs=[
            *_swa_in_specs(step_of),
            pl.BlockSpec((tq, LANES), lambda b, i: (i, qb_blk + b)),
            pl.BlockSpec((s, LANES), lambda b, i: (0, kb_blk + b)),
            pl.BlockSpec((s, LANES), lambda b, i: (0, vb_blk + b)),
            pl.BlockSpec(c.shape, lambda b, i: (0, 0)),
            *w_specs,
        ],
        out_specs=[pl.BlockSpec((BLOCK, A_Q_W), lambda b, i: (step_of(b, i), 0)),
                   pl.BlockSpec((tq, LANES), lambda b, i: (i, b)), *w_specs],
        out_shape=[jax.ShapeDtypeStruct((s, A_Q_W), jnp.bfloat16),
                   jax.ShapeDtypeStruct((s, B_W), jnp.bfloat16)]
        + [jax.ShapeDtypeStruct(w.shape, jnp.bfloat16) for w in weights],
        compiler_params=pltpu.CompilerParams(
            dimension_semantics=("arbitrary", "arbitrary"), vmem_limit_bytes=VMEM_LIMIT),
        name="attention",
    )(sinks, *([proj] * 5), _half_swap_matrix(A_KV_W), _swa_bias(), proj, proj, proj, c, *weights)
    return outs[0], outs[1], outs[2:]


def _merge_kernel(*refs, n_chunks):
    oa_ref, ob_ref = refs[0], refs[1]
    ga_refs = refs[2:2 + n_chunks]
    gb_refs = refs[2 + n_chunks:2 + 2 * n_chunks]
    x_ref, wa_ref, wb_ref, wo_ref, o_ref, m_ref = refs[2 + 2 * n_chunks:]
    tn = o_ref.shape[1] // n_chunks
    for c in range(n_chunks):
        sl = slice(c * tn, (c + 1) * tn)
        ya = jnp.dot(oa_ref[...], wa_ref[:, sl], preferred_element_type=jnp.float32)
        yb = jnp.dot(ob_ref[...], wb_ref[:, sl], preferred_element_type=jnp.float32)
        ga = jax.nn.sigmoid(ga_refs[c][...].astype(jnp.float32))
        gb = jax.nn.sigmoid(gb_refs[c][...].astype(jnp.float32))
        m_ref[:, sl] = (ga * ya + gb * yb).astype(m_ref.dtype)
    for c in range(n_chunks):
        sl = slice(c * tn, (c + 1) * tn)
        o_ref[:, sl] = x_ref[:, sl] + jnp.dot(m_ref[...], wo_ref[:, sl],
                                              preferred_element_type=jnp.float32)


def _merge(oa, ob, proj, x, wa, wb, wo, *, tm=512, tn=512):
    s, d = x.shape
    n_chunks = d // tn
    const = lambda i: (0, 0)
    gate_specs = [pl.BlockSpec((tm, tn), functools.partial(lambda i, blk: (i, blk), blk=off // tn + c))
                  for off in (GA_OFF, GB_OFF) for c in range(n_chunks)]
    return pl.pallas_call(
        functools.partial(_merge_kernel, n_chunks=n_chunks),
        grid=(s // tm,),
        in_specs=[
            pl.BlockSpec((tm, A_Q_W), lambda i: (i, 0)),
            pl.BlockSpec((tm, B_W), lambda i: (i, 0)),
            *gate_specs,
            pl.BlockSpec((tm, d), lambda i: (i, 0)),
            pl.BlockSpec(wa.shape, const),
            pl.BlockSpec(wb.shape, const),
            pl.BlockSpec(wo.shape, const),
        ],
        out_specs=pl.BlockSpec((tm, d), lambda i: (i, 0)),
        out_shape=jax.ShapeDtypeStruct((s, d), jnp.float32),
        scratch_shapes=[pltpu.VMEM((tm, d), jnp.bfloat16)],
        compiler_params=pltpu.CompilerParams(
            dimension_semantics=("arbitrary",), vmem_limit_bytes=VMEM_LIMIT),
        name="merge",
    )(oa, ob, *([proj] * (2 * n_chunks)), x, wa, wb, wo)


def _ffn_kernel(x_hbm, g_ref, wg_ref, wu_ref, wd_ref, gf_ref, o_ref, xbuf, sem, h_ref, *,
                row_chunk):
    f = pl.program_id(1)

    def start_row_block():
        _rmsnorm_rows(xbuf, h_ref, g_ref[...], row_chunk)
        o_ref[...] = xbuf[...]

    _consume_row_block(x_hbm, xbuf, sem, start_row_block)

    h = h_ref[...]
    gate = jnp.dot(h, wg_ref[...], preferred_element_type=jnp.float32)
    up = jnp.dot(h, wu_ref[...], preferred_element_type=jnp.float32)
    act = (gate * jax.nn.sigmoid(gate) * up).astype(jnp.bfloat16)
    o_ref[...] += jnp.dot(act, wd_ref[...], preferred_element_type=jnp.float32)

    @pl.when(f == pl.num_programs(1) - 1)
    def _():
        _rmsnorm_rows(o_ref, o_ref, gf_ref[...], row_chunk)


def _ffn(x1, g, w_in, w_down, gf, *, tm=1024, tf=512):
    s, d = x1.shape
    nf = D_FF // tf
    return pl.pallas_call(
        functools.partial(_ffn_kernel, row_chunk=NORM_ROWS),
        grid=(s // tm, nf),
        in_specs=[
            pl.BlockSpec(memory_space=pl.ANY),
            pl.BlockSpec((1, d), lambda i, f: (0, 0)),
            pl.BlockSpec((d, tf), lambda i, f: (0, f)),
            pl.BlockSpec((d, tf), lambda i, f: (0, f + nf)),
            pl.BlockSpec((tf, d), lambda i, f: (f, 0)),
            pl.BlockSpec((1, d), lambda i, f: (0, 0)),
        ],
        out_specs=pl.BlockSpec((tm, d), lambda i, f: (i, 0)),
        out_shape=jax.ShapeDtypeStruct((s, d), jnp.float32),
        scratch_shapes=[pltpu.VMEM((tm, d), x1.dtype), pltpu.SemaphoreType.DMA(()),
                        pltpu.VMEM((tm, d), jnp.bfloat16)],
        compiler_params=pltpu.CompilerParams(
            dimension_semantics=("arbitrary", "arbitrary"), vmem_limit_bytes=VMEM_LIMIT),
        name="ffn",
    )(x1, g, w_in, w_in, w_down, gf)


def kernel(x, norm_mix_g, w_in, sink_logits, w_branch_a, w_branch_b, w_out,
           norm_ffn_g, w_ffn_in, w_ffn_down, norm_final_g):
    b, s, d = x.shape
    assert (b, s, d) == (1, SEQ, D_MODEL) and w_in.shape[0] == 1
    x2 = x.reshape(s, d)
    proj = _in_proj(x2, norm_mix_g[0].reshape(1, d), w_in[0])
    f32_weights = [w_branch_a[0], w_branch_b[0], w_out[0], w_ffn_in[0], w_ffn_down[0]]
    oa, ob, (wa, wb, wo, wfi, wfd) = _attention(proj, sink_logits[0], f32_weights)
    x1 = _merge(oa, ob, proj, x2, wa, wb, wo)
    out = _ffn(x1, norm_ffn_g[0].reshape(1, d), wfi, wfd, norm_final_g.reshape(1, d))
    return out.reshape(b, s, d)
```

```python
import functools
import math

import numpy as np
import jax
import jax.numpy as jnp
from jax import lax
from jax.experimental import pallas as pl
from jax.experimental.pallas import tpu as pltpu

D_MODEL = 2048
SEQ = 8192
HEAD_DIM = 64
A_Q_HEADS = 16
A_KV_HEADS = 4
A_GROUP = A_Q_HEADS // A_KV_HEADS
WINDOW = 128
B_HEADS = 16
BLOCK = 128
D_FF = 5632
EPS = 1e-6
SCALE = 1.0 / math.sqrt(HEAD_DIM)

A_Q_W = A_Q_HEADS * HEAD_DIM
A_KV_W = A_KV_HEADS * HEAD_DIM
B_W = B_HEADS * HEAD_DIM
IN_WIDTH = A_Q_W + 2 * A_KV_W + 3 * B_W + 2 * D_MODEL

LANES = 128
BF16_ROWS = 16
PAIRS_B = B_HEADS // 2

QA_OFF = 0
KA_OFF = QA_OFF + A_Q_W
VA_OFF = KA_OFF + A_KV_W
QB_OFF = VA_OFF + A_KV_W
KB_OFF = QB_OFF + B_W
VB_OFF = KB_OFF + B_W
GA_OFF = VB_OFF + B_W
GB_OFF = GA_OFF + D_MODEL

F32_EXP_ZERO = -104.0

VMEM_LIMIT = 56 * 1024 * 1024
NORM_ROWS = 128

_NT = (((1,), (1,)), ((), ()))


def _rmsnorm_rows(src_ref, dst_ref, g, row_chunk):
    def body(c, _):
        r = pl.multiple_of(c * row_chunk, row_chunk)
        x = src_ref[pl.ds(r, row_chunk), :]
        inv = lax.rsqrt(jnp.mean(x * x, axis=-1, keepdims=True) + EPS)
        dst_ref[pl.ds(r, row_chunk), :] = (x * inv * g).astype(dst_ref.dtype)
        return 0

    lax.fori_loop(0, src_ref.shape[0] // row_chunk, body, 0)


def _consume_row_block(x_hbm, xbuf, sem, consume):
    i = pl.program_id(0)
    rows = xbuf.shape[0]

    def copy(blk):
        return pltpu.make_async_copy(x_hbm.at[pl.ds(blk * rows, rows), :], xbuf, sem)

    @pl.when(pl.program_id(1) == 0)
    def _():
        @pl.when(i == 0)
        def _():
            copy(0).start()

        copy(i).wait()
        consume()

        @pl.when(i + 1 < pl.num_programs(0))
        def _():
            copy(i + 1).start()


def _in_proj_kernel(x_hbm, g_ref, w_ref, o_ref, xbuf, sems, h_ref, *, row_block):
    chunk = xbuf.shape[1]
    n_chunks = h_ref.shape[0] // chunk

    def copy(c, slot):
        return pltpu.make_async_copy(x_hbm.at[pl.ds(c * chunk, chunk), :], xbuf.at[slot],
                                     sems.at[slot])

    @pl.when(pl.program_id(0) == 0)
    def _():
        copy(0, 0).start()

        def body(c, _):
            slot = c % 2

            @pl.when(c + 1 < n_chunks)
            def _():
                copy(c + 1, 1 - slot).start()

            copy(c, slot).wait()
            x = xbuf[slot]
            inv = lax.rsqrt(jnp.mean(x * x, axis=-1, keepdims=True) + EPS)
            r = pl.multiple_of(c * chunk, chunk)
            h_ref[pl.ds(r, chunk), :] = (x * inv * g_ref[...]).astype(h_ref.dtype)
            return 0

        lax.fori_loop(0, n_chunks, body, 0)

    w = w_ref[...].astype(jnp.bfloat16)
    for r in range(0, h_ref.shape[0], row_block):
        o_ref[r:r + row_block, :] = jnp.dot(
            h_ref[r:r + row_block, :], w, preferred_element_type=jnp.float32).astype(o_ref.dtype)


def _in_proj(x, g, w, *, tn=256, chunk=2 * NORM_ROWS, row_block=2048):
    s, d = x.shape
    n = w.shape[1]
    return pl.pallas_call(
        functools.partial(_in_proj_kernel, row_block=row_block),
        grid=(n // tn,),
        in_specs=[
            pl.BlockSpec(memory_space=pl.ANY),
            pl.BlockSpec((1, d), lambda j: (0, 0)),
            pl.BlockSpec((d, tn), lambda j: (0, j)),
        ],
        out_specs=pl.BlockSpec((s, tn), lambda j: (0, j)),
        out_shape=jax.ShapeDtypeStruct((s, n), jnp.bfloat16),
        scratch_shapes=[pltpu.VMEM((2, chunk, d), x.dtype), pltpu.SemaphoreType.DMA((2,)),
                        pltpu.VMEM((s, d), jnp.bfloat16)],
        compiler_params=pltpu.CompilerParams(
            dimension_semantics=("arbitrary",), vmem_limit_bytes=VMEM_LIMIT),
        name="in_proj",
    )(x, g, w)


def _half_swap_matrix(n):
    c = np.arange(n)
    p = np.zeros((n, n), np.float32)
    p[c, c ^ HEAD_DIM] = 1.0
    return jnp.asarray(p, jnp.bfloat16)


def _swa_scores(q_ref, kp_ref, kc_ref, vp_ref, vc_ref, p_ref):
    k = jnp.concatenate([kp_ref[...], kc_ref[...]], axis=0)
    v = jnp.concatenate([vp_ref[...], vc_ref[...]], axis=0)
    k_sw = jnp.dot(k, p_ref[...], preferred_element_type=jnp.float32).astype(k.dtype)
    v_sw = jnp.dot(v, p_ref[...], preferred_element_type=jnp.float32).astype(v.dtype)
    lane = lax.broadcasted_iota(jnp.int32, (BLOCK, LANES), 1)
    left = lane < HEAD_DIM
    out = []
    for j in range(A_KV_HEADS):
        tile = slice((j // 2) * LANES, (j // 2 + 1) * LANES)
        qa = q_ref[:, (2 * j) * LANES:(2 * j + 1) * LANES] * jnp.bfloat16(SCALE)
        qb = q_ref[:, (2 * j + 1) * LANES:(2 * j + 2) * LANES] * jnp.bfloat16(SCALE)
        zero = jnp.zeros_like(qa)
        for side in range(2):
            keep = left if side == 0 else jnp.logical_not(left)
            aligned = (j % 2) == side
            kh = (k if aligned else k_sw)[:, tile]
            vh = (v if aligned else v_sw)[:, tile]
            lhs = jnp.concatenate([jnp.where(keep, qa, zero), jnp.where(keep, qb, zero)], axis=0)
            out.append((lax.dot_general(lhs, kh, _NT, preferred_element_type=jnp.float32), vh))
    return out


def _swa_bias():
    qi = np.arange(BLOCK)[:, None]
    ki = np.arange(2 * BLOCK)[None, :]
    dist = BLOCK + qi - ki
    window = (dist >= 0) & (dist < WINDOW)
    heads = np.arange(1, A_Q_HEADS + 1, dtype=np.float32)
    slopes = np.exp2(np.float32(-8.0) * heads / np.float32(A_Q_HEADS)).astype(np.float32)
    bias = np.where(window[None], -(slopes[:, None, None] * dist[None].astype(np.float32)),
                    -np.inf).astype(np.float32)
    first = np.where((ki >= BLOCK)[None], bias, -np.inf).astype(np.float32)
    return jnp.asarray(np.stack([first, bias]))


def _swa_finish(bias_ref, sink_ref, scores, o_ref):
    lane = lax.broadcasted_iota(jnp.int32, (BLOCK, LANES), 1)
    left = lane < HEAD_DIM
    for j in range(A_KV_HEADS):
        outs = []
        for side in range(2):
            s2, vh = scores[2 * j + side]
            ps, inv_den = [], []
            for t in range(2):
                h = A_GROUP * j + 2 * t + side
                s = s2[t * BLOCK:(t + 1) * BLOCK] + bias_ref[0, h]
                sink = sink_ref[h]
                m = jnp.maximum(jnp.max(s, axis=-1, keepdims=True), sink)
                p = jnp.exp(s - m)
                den = jnp.sum(p, axis=-1, keepdims=True) + jnp.exp(sink - m)
                ps.append(p.astype(jnp.bfloat16))
                inv_den.append(1.0 / den)
            o2 = jnp.dot(jnp.concatenate(ps, axis=0), vh, preferred_element_type=jnp.float32)
            outs.append((o2[:BLOCK] * inv_den[0], o2[BLOCK:] * inv_den[1]))
        for t in range(2):
            out = jnp.where(left, outs[0][t], outs[1][t])
            o_ref[:, (2 * j + t) * LANES:(2 * j + t + 1) * LANES] = out.astype(o_ref.dtype)


N_SWA_IN = 8


def _swa_in_specs(step_of):
    qa_blk = QA_OFF // A_Q_W
    ka_blk = KA_OFF // A_KV_W
    va_blk = VA_OFF // A_KV_W
    cur = lambda *g: step_of(*g)
    prev = lambda *g: jnp.maximum(step_of(*g) - 1, 0)
    return [
        pl.BlockSpec(memory_space=pltpu.SMEM),
        pl.BlockSpec((BLOCK, A_Q_W), lambda *g: (cur(*g), qa_blk)),
        pl.BlockSpec((BLOCK, A_KV_W), lambda *g: (prev(*g), ka_blk)),
        pl.BlockSpec((BLOCK, A_KV_W), lambda *g: (cur(*g), ka_blk)),
        pl.BlockSpec((BLOCK, A_KV_W), lambda *g: (prev(*g), va_blk)),
        pl.BlockSpec((BLOCK, A_KV_W), lambda *g: (cur(*g), va_blk)),
        pl.BlockSpec((A_KV_W, A_KV_W), lambda *g: (0, 0)),
        pl.BlockSpec((1, A_Q_HEADS, BLOCK, 2 * BLOCK),
                     lambda *g: (jnp.minimum(step_of(*g), 1), 0, 0, 0)),
    ]


STICK_FIRST_BLOCKS = 3
SOFTPLUS_CLAMP = 40.0
MASKED = 1e30


def _stick_constant():
    j = np.arange(BLOCK)[:, None]
    s = np.arange(BLOCK)[None, :]
    tri = (j >= s).astype(np.float32)
    zero = np.zeros_like(tri)
    return jnp.asarray(np.block([[tri, zero], [zero, tri]]), jnp.bfloat16)


def _row_total(sums):
    return jnp.broadcast_to(sums[:, :1], sums.shape)


def _softplus(z):
    return jnp.maximum(z, jnp.log(1.0 + jnp.exp(jnp.minimum(z, SOFTPLUS_CLAMP))))


def _stick_scores_clamped(lhs, k_ref, blk):
    nk = STICK_FIRST_BLOCKS
    first = jnp.maximum(blk - (nk - 1), 0)
    start = pl.multiple_of(first * BLOCK, BLOCK)
    z = lax.dot_general(lhs, k_ref[pl.ds(start, nk * BLOCK), :], _NT,
                        preferred_element_type=jnp.float32)
    return [z[:, n * BLOCK:(n + 1) * BLOCK] for n in range(nk)], first


def _stick_masked(z, blk, first, mask_all):
    nk = STICK_FIRST_BLOCKS
    row = lax.broadcasted_iota(jnp.int32, (2 * BLOCK, BLOCK), 0) & (BLOCK - 1)
    col = lax.broadcasted_iota(jnp.int32, (2 * BLOCK, BLOCK), 1)
    out = []
    for n in range(nk):
        zn = z[n]
        if mask_all:
            m = (first + n - blk) * BLOCK + col < row
        elif n == nk - 1:
            m = col < row
        else:
            m = None
        if m is not None:
            zn = jnp.where(m, zn, -MASKED)
        out.append((zn, _softplus(zn).astype(jnp.bfloat16)))
    return out


def _stick_weights(terms):
    tail = None
    a = [None] * len(terms)
    for n in reversed(range(len(terms))):
        zn, sums = terms[n]
        e = zn - sums
        if tail is not None:
            e = e - tail
        a[n] = jnp.exp(e).astype(jnp.bfloat16)
        tail = _row_total(sums) if tail is None else tail + _row_total(sums)
    return a, tail


def _stick_values_clamped(a, v_ref, first):
    start = pl.multiple_of(first * BLOCK, BLOCK)
    v0 = v_ref[pl.ds(start, STICK_FIRST_BLOCKS * BLOCK), :]
    return jnp.dot(jnp.concatenate(a, axis=1), v0, preferred_element_type=jnp.float32)


def _stick_live(tail):
    return (jnp.min(tail) < -F32_EXP_ZERO).astype(jnp.int32)


def _stick_rest(lhs, k_ref, v_ref, c, acc, tail, first, live0):
    live = _stick_live

    def cond(carry):
        kb, alive, _, _ = carry
        return jnp.logical_and(kb >= 0, alive > 0)

    def body(carry):
        kb, _, acc, tail = carry
        st = pl.multiple_of(kb * BLOCK, BLOCK)
        z = lax.dot_general(lhs, k_ref[pl.ds(st, BLOCK), :], _NT,
                            preferred_element_type=jnp.float32)
        sums = jnp.dot(_softplus(z).astype(jnp.bfloat16), c[:BLOCK, :BLOCK],
                       preferred_element_type=jnp.float32)
        a = jnp.exp(z - sums - tail)
        acc = acc + jnp.dot(a.astype(jnp.bfloat16), v_ref[pl.ds(st, BLOCK), :],
                            preferred_element_type=jnp.float32)
        tail = tail + _row_total(sums)
        return kb - 1, live(tail), acc, tail

    _, _, acc, _ = lax.while_loop(cond, body, (first - 1, live0, acc, tail))
    return acc


def _attn_kernel(*refs, q_blocks, n_cast):
    swa_in, refs = refs[:N_SWA_IN], refs[N_SWA_IN:]
    q_ref, k_ref, v_ref, c_ref = refs[:4]
    w_refs = refs[4:4 + n_cast]
    oa_ref, o_ref = refs[4 + n_cast:6 + n_cast]
    wo_refs = refs[6 + n_cast:]
    it = pl.program_id(1)
    lane = lax.broadcasted_iota(jnp.int32, (BLOCK, LANES), 1)
    left = lane < HEAD_DIM
    c = c_ref[...]

    def run(first_step):
        for w_ref, wo_ref in zip(w_refs, wo_refs):
            wo_ref[...] = w_ref[...].astype(wo_ref.dtype)
        blks = [it * q_blocks + g for g in range(q_blocks)]
        lhss = []
        for g in range(q_blocks):
            q = q_ref[g * BLOCK:(g + 1) * BLOCK, :] * jnp.bfloat16(SCALE)
            zero = jnp.zeros_like(q)
            lhss.append(jnp.concatenate([jnp.where(left, q, zero), jnp.where(left, zero, q)], axis=0))
        _swa_finish(swa_in[7], swa_in[0], _swa_scores(*swa_in[1:7]), oa_ref)
        nk = STICK_FIRST_BLOCKS
        clamped = [_stick_scores_clamped(lhss[g], k_ref, blks[g]) for g in range(q_blocks)]
        z = [zs for zs, _ in clamped]
        firsts = [first for _, first in clamped]
        masked = [_stick_masked(z[g], blks[g], firsts[g], first_step and g < nk - 1)
                  for g in range(q_blocks)]
        sps = [sp for m in masked for _, sp in m]
        sums = jnp.dot(
            jnp.concatenate([jnp.concatenate(sps[t:t + 2], axis=1) for t in range(0, len(sps), 2)],
                            axis=0), c, preferred_element_type=jnp.float32)
        rows = 2 * BLOCK

        def block_sums(t):
            return sums[(t // 2) * rows:(t // 2 + 1) * rows, (t % 2) * BLOCK:(t % 2 + 1) * BLOCK]

        terms = [[(masked[g][n][0], block_sums(g * nk + n)) for n in range(nk)]
                 for g in range(q_blocks)]
        weights = [_stick_weights(terms[g]) for g in range(q_blocks)]
        accs = [_stick_values_clamped(weights[g][0], v_ref, firsts[g]) for g in range(q_blocks)]
        lives = [_stick_live(tail) for _, tail in weights]
        for g in range(q_blocks):
            acc = _stick_rest(lhss[g], k_ref, v_ref, c, accs[g], weights[g][1], firsts[g], lives[g])
            o_ref[g * BLOCK:(g + 1) * BLOCK, :] = jnp.where(
                left, acc[:BLOCK], acc[BLOCK:]).astype(o_ref.dtype)

    assert q_blocks >= STICK_FIRST_BLOCKS - 1
    pl.when(it == 0)(lambda: run(True))
    pl.when(it > 0)(lambda: run(False))


def _cast_block_spec(shape, steps, n_inner):
    rows, cols = shape
    for n_cb in range(1, steps + 1):
        n_rb = steps // n_cb
        if steps % n_cb == 0 and rows % n_rb == 0 and cols % n_cb == 0 \
                and (rows // n_rb) % BF16_ROWS == 0 and (cols // n_cb) % LANES == 0:
            return pl.BlockSpec(
                (rows // n_rb, cols // n_cb),
                lambda b, i: ((b * n_inner + i) // n_cb, (b * n_inner + i) % n_cb))
    raise ValueError(f"cannot tile {shape} into {steps} blocks")


def _attention(proj, sinks, weights, *, q_blocks=8):
    s = proj.shape[0]
    tq = q_blocks * BLOCK
    n_q = s // tq
    steps = PAIRS_B * n_q
    assert steps == s // BLOCK
    c = _stick_constant()
    qb_blk = QB_OFF // LANES
    kb_blk = KB_OFF // LANES
    vb_blk = VB_OFF // LANES
    step_of = lambda b, i: b * n_q + i
    w_specs = [_cast_block_spec(w.shape, steps, n_q) for w in weights]
    outs = pl.pallas_call(
        functools.partial(_attn_kernel, q_blocks=q_blocks, n_cast=len(weights)),
        grid=(PAIRS_B, n_q),
        in_specs=[
            *_swa_in_specs(step_of),
            pl.BlockSpec((tq, LANES), lambda b, i: (i, qb_blk + b)),
            pl.BlockSpec((s, LANES), lambda b, i: (0, kb_blk + b)),
            pl.BlockSpec((s, LANES), lambda b, i: (0, vb_blk + b)),
            pl.BlockSpec(c.shape, lambda b, i: (0, 0)),
            *w_specs,
        ],
        out_specs=[pl.BlockSpec((BLOCK, A_Q_W), lambda b, i: (step_of(b, i), 0)),
                   pl.BlockSpec((tq, LANES), lambda b, i: (i, b)), *w_specs],
        out_shape=[jax.ShapeDtypeStruct((s, A_Q_W), jnp.bfloat16),
                   jax.ShapeDtypeStruct((s, B_W), jnp.bfloat16)]
        + [jax.ShapeDtypeStruct(w.shape, jnp.bfloat16) for w in weights],
        compiler_params=pltpu.CompilerParams(
            dimension_semantics=("arbitrary", "arbitrary"), vmem_limit_bytes=VMEM_LIMIT),
        name="attention",
    )(sinks, *([proj] * 5), _half_swap_matrix(A_KV_W), _swa_bias(), proj, proj, proj, c, *weights)
    return outs[0], outs[1], outs[2:]


def _merge_kernel(*refs, n_chunks):
    oa_ref, ob_ref = refs[0], refs[1]
    ga_refs = refs[2:2 + n_chunks]
    gb_refs = refs[2 + n_chunks:2 + 2 * n_chunks]
    x_ref, wa_ref, wb_ref, wo_ref, o_ref, m_ref = refs[2 + 2 * n_chunks:]
    tn = o_ref.shape[1] // n_chunks
    for c in range(n_chunks):
        sl = slice(c * tn, (c + 1) * tn)
        ya = jnp.dot(oa_ref[...], wa_ref[:, sl], preferred_element_type=jnp.float32)
        yb = jnp.dot(ob_ref[...], wb_ref[:, sl], preferred_element_type=jnp.float32)
        ga = jax.nn.sigmoid(ga_refs[c][...].astype(jnp.float32))
        gb = jax.nn.sigmoid(gb_refs[c][...].astype(jnp.float32))
        m_ref[:, sl] = (ga * ya + gb * yb).astype(m_ref.dtype)
    for c in range(n_chunks):
        sl = slice(c * tn, (c + 1) * tn)
        o_ref[:, sl] = x_ref[:, sl] + jnp.dot(m_ref[...], wo_ref[:, sl],
                                              preferred_element_type=jnp.float32)


def _merge(oa, ob, proj, x, wa, wb, wo, *, tm=512, tn=512):
    s, d = x.shape
    n_chunks = d // tn
    const = lambda i: (0, 0)
    gate_specs = [pl.BlockSpec((tm, tn), functools.partial(lambda i, blk: (i, blk), blk=off // tn + c))
                  for off in (GA_OFF, GB_OFF) for c in range(n_chunks)]
    return pl.pallas_call(
        functools.partial(_merge_kernel, n_chunks=n_chunks),
        grid=(s // tm,),
        in_specs=[
            pl.BlockSpec((tm, A_Q_W), lambda i: (i, 0)),
            pl.BlockSpec((tm, B_W), lambda i: (i, 0)),
            *gate_specs,
            pl.BlockSpec((tm, d), lambda i: (i, 0)),
            pl.BlockSpec(wa.shape, const),
            pl.BlockSpec(wb.shape, const),
            pl.BlockSpec(wo.shape, const),
        ],
        out_specs=pl.BlockSpec((tm, d), lambda i: (i, 0)),
        out_shape=jax.ShapeDtypeStruct((s, d), jnp.float32),
        scratch_shapes=[pltpu.VMEM((tm, d), jnp.bfloat16)],
        compiler_params=pltpu.CompilerParams(
            dimension_semantics=("arbitrary",), vmem_limit_bytes=VMEM_LIMIT),
        name="merge",
    )(oa, ob, *([proj] * (2 * n_chunks)), x, wa, wb, wo)


def _ffn_kernel(x_hbm, g_ref, wg_ref, wu_ref, wd_ref, gf_ref, o_ref, xbuf, sem, h_ref, *,
                row_chunk):
    f = pl.program_id(1)

    def start_row_block():
        _rmsnorm_rows(xbuf, h_ref, g_ref[...], row_chunk)
        o_ref[...] = xbuf[...]

    _consume_row_block(x_hbm, xbuf, sem, start_row_block)

    h = h_ref[...]
    gate = jnp.dot(h, wg_ref[...], preferred_element_type=jnp.float32)
    up = jnp.dot(h, wu_ref[...], preferred_element_type=jnp.float32)
    act = (gate * jax.nn.sigmoid(gate) * up).astype(jnp.bfloat16)
    o_ref[...] += jnp.dot(act, wd_ref[...], preferred_element_type=jnp.float32)

    @pl.when(f == pl.num_programs(1) - 1)
    def _():
        _rmsnorm_rows(o_ref, o_ref, gf_ref[...], row_chunk)


def _ffn(x1, g, w_in, w_down, gf, *, tm=1024, tf=512):
    s, d = x1.shape
    nf = D_FF // tf
    return pl.pallas_call(
        functools.partial(_ffn_kernel, row_chunk=NORM_ROWS),
        grid=(s // tm, nf),
        in_specs=[
            pl.BlockSpec(memory_space=pl.ANY),
            pl.BlockSpec((1, d), lambda i, f: (0, 0)),
            pl.BlockSpec((d, tf), lambda i, f: (0, f)),
            pl.BlockSpec((d, tf), lambda i, f: (0, f + nf)),
            pl.BlockSpec((tf, d), lambda i, f: (f, 0)),
            pl.BlockSpec((1, d), lambda i, f: (0, 0)),
        ],
        out_specs=pl.BlockSpec((tm, d), lambda i, f: (i, 0)),
        out_shape=jax.ShapeDtypeStruct((s, d), jnp.float32),
        scratch_shapes=[pltpu.VMEM((tm, d), x1.dtype), pltpu.SemaphoreType.DMA(()),
                        pltpu.VMEM((tm, d), jnp.bfloat16)],
        compiler_params=pltpu.CompilerParams(
            dimension_semantics=("arbitrary", "arbitrary"), vmem_limit_bytes=VMEM_LIMIT),
        name="ffn",
    )(x1, g, w_in, w_in, w_down, gf)


def kernel(x, norm_mix_g, w_in, sink_logits, w_branch_a, w_branch_b, w_out,
           norm_ffn_g, w_ffn_in, w_ffn_down, norm_final_g):
    b, s, d = x.shape
    assert (b, s, d) == (1, SEQ, D_MODEL) and w_in.shape[0] == 1
    x2 = x.reshape(s, d)
    proj = _in_proj(x2, norm_mix_g[0].reshape(1, d), w_in[0])
    f32_weights = [w_branch_a[0], w_branch_b[0], w_out[0], w_ffn_in[0], w_ffn_down[0]]
    oa, ob, (wa, wb, wo, wfi, wfd) = _attention(proj, sink_logits[0], f32_weights)
    x1 = _merge(oa, ob, proj, x2, wa, wb, wo)
    out = _ffn(x1, norm_ffn_g[0].reshape(1, d), wfi, wfd, norm_final_g.reshape(1, d))
    return out.reshape(b, s, d)
```

```python
import functools
import math

import numpy as np
import jax
import jax.numpy as jnp
from jax import lax
from jax.experimental import pallas as pl
from jax.experimental.pallas import tpu as pltpu

D_MODEL = 2048
SEQ = 8192
HEAD_DIM = 64
A_Q_HEADS = 16
A_KV_HEADS = 4
A_GROUP = A_Q_HEADS // A_KV_HEADS
WINDOW = 128
B_HEADS = 16
BLOCK = 128
D_FF = 5632
EPS = 1e-6
SCALE = 1.0 / math.sqrt(HEAD_DIM)

A_Q_W = A_Q_HEADS * HEAD_DIM
A_KV_W = A_KV_HEADS * HEAD_DIM
B_W = B_HEADS * HEAD_DIM
IN_WIDTH = A_Q_W + 2 * A_KV_W + 3 * B_W + 2 * D_MODEL

LANES = 128
BF16_ROWS = 16
PAIRS_B = B_HEADS // 2

QA_OFF = 0
KA_OFF = QA_OFF + A_Q_W
VA_OFF = KA_OFF + A_KV_W
QB_OFF = VA_OFF + A_KV_W
KB_OFF = QB_OFF + B_W
VB_OFF = KB_OFF + B_W
GA_OFF = VB_OFF + B_W
GB_OFF = GA_OFF + D_MODEL

F32_EXP_ZERO = -104.0

VMEM_LIMIT = 56 * 1024 * 1024
NORM_ROWS = 128

_NT = (((1,), (1,)), ((), ()))


def _rmsnorm_rows(src_ref, dst_ref, g, row_chunk):
    def body(c, _):
        r = pl.multiple_of(c * row_chunk, row_chunk)
        x = src_ref[pl.ds(r, row_chunk), :]
        inv = lax.rsqrt(jnp.mean(x * x, axis=-1, keepdims=True) + EPS)
        dst_ref[pl.ds(r, row_chunk), :] = (x * inv * g).astype(dst_ref.dtype)
        return 0

    lax.fori_loop(0, src_ref.shape[0] // row_chunk, body, 0)


def _consume_row_block(x_hbm, xbuf, sem, consume):
    i = pl.program_id(0)
    rows = xbuf.shape[0]

    def copy(blk):
        return pltpu.make_async_copy(x_hbm.at[pl.ds(blk * rows, rows), :], xbuf, sem)

    @pl.when(pl.program_id(1) == 0)
    def _():
        @pl.when(i == 0)
        def _():
            copy(0).start()

        copy(i).wait()
        consume()

        @pl.when(i + 1 < pl.num_programs(0))
        def _():
            copy(i + 1).start()


def _in_proj_kernel(x_hbm, g_ref, w_ref, o_ref, xbuf, sems, h_ref, *, row_block):
    chunk = xbuf.shape[1]
    n_chunks = h_ref.shape[0] // chunk

    def copy(c, slot):
        return pltpu.make_async_copy(x_hbm.at[pl.ds(c * chunk, chunk), :], xbuf.at[slot],
                                     sems.at[slot])

    @pl.when(pl.program_id(0) == 0)
    def _():
        copy(0, 0).start()

        def body(c, _):
            slot = c % 2

            @pl.when(c + 1 < n_chunks)
            def _():
                copy(c + 1, 1 - slot).start()

            copy(c, slot).wait()
            x = xbuf[slot]
            inv = lax.rsqrt(jnp.mean(x * x, axis=-1, keepdims=True) + EPS)
            r = pl.multiple_of(c * chunk, chunk)
            h_ref[pl.ds(r, chunk), :] = (x * inv * g_ref[...]).astype(h_ref.dtype)
            return 0

        lax.fori_loop(0, n_chunks, body, 0)

    w = w_ref[...].astype(jnp.bfloat16)
    for r in range(0, h_ref.shape[0], row_block):
        o_ref[r:r + row_block, :] = jnp.dot(
            h_ref[r:r + row_block, :], w, preferred_element_type=jnp.float32).astype(o_ref.dtype)


def _in_proj(x, g, w, *, tn=256, chunk=2 * NORM_ROWS, row_block=2048):
    s, d = x.shape
    n = w.shape[1]
    return pl.pallas_call(
        functools.partial(_in_proj_kernel, row_block=row_block),
        grid=(n // tn,),
        in_specs=[
            pl.BlockSpec(memory_space=pl.ANY),
            pl.BlockSpec((1, d), lambda j: (0, 0)),
            pl.BlockSpec((d, tn), lambda j: (0, j)),
        ],
        out_specs=pl.BlockSpec((s, tn), lambda j: (0, j)),
        out_shape=jax.ShapeDtypeStruct((s, n), jnp.bfloat16),
        scratch_shapes=[pltpu.VMEM((2, chunk, d), x.dtype), pltpu.SemaphoreType.DMA((2,)),
                        pltpu.VMEM((s, d), jnp.bfloat16)],
        compiler_params=pltpu.CompilerParams(
            dimension_semantics=("arbitrary",), vmem_limit_bytes=VMEM_LIMIT),
        name="in_proj",
    )(x, g, w)


def _half_swap_matrix(n):
    c = np.arange(n)
    p = np.zeros((n, n), np.float32)
    p[c, c ^ HEAD_DIM] = 1.0
    return jnp.asarray(p, jnp.bfloat16)


def _swa_scores(q_ref, kp_ref, kc_ref, vp_ref, vc_ref, p_ref):
    k = jnp.concatenate([kp_ref[...], kc_ref[...]], axis=0)
    v = jnp.concatenate([vp_ref[...], vc_ref[...]], axis=0)
    k_sw = jnp.dot(k, p_ref[...], preferred_element_type=jnp.float32).astype(k.dtype)
    v_sw = jnp.dot(v, p_ref[...], preferred_element_type=jnp.float32).astype(v.dtype)
    lane = lax.broadcasted_iota(jnp.int32, (BLOCK, LANES), 1)
    left = lane < HEAD_DIM
    out = []
    for j in range(A_KV_HEADS):
        tile = slice((j // 2) * LANES, (j // 2 + 1) * LANES)
        qa = q_ref[:, (2 * j) * LANES:(2 * j + 1) * LANES] * jnp.bfloat16(SCALE)
        qb = q_ref[:, (2 * j + 1) * LANES:(2 * j + 2) * LANES] * jnp.bfloat16(SCALE)
        zero = jnp.zeros_like(qa)
        for side in range(2):
            keep = left if side == 0 else jnp.logical_not(left)
            aligned = (j % 2) == side
            kh = (k if aligned else k_sw)[:, tile]
            vh = (v if aligned else v_sw)[:, tile]
            lhs = jnp.concatenate([jnp.where(keep, qa, zero), jnp.where(keep, qb, zero)], axis=0)
            out.append((lax.dot_general(lhs, kh, _NT, preferred_element_type=jnp.float32), vh))
    return out


def _swa_bias():
    qi = np.arange(BLOCK)[:, None]
    ki = np.arange(2 * BLOCK)[None, :]
    dist = BLOCK + qi - ki
    window = (dist >= 0) & (dist < WINDOW)
    heads = np.arange(1, A_Q_HEADS + 1, dtype=np.float32)
    slopes = np.exp2(np.float32(-8.0) * heads / np.float32(A_Q_HEADS)).astype(np.float32)
    bias = np.where(window[None], -(slopes[:, None, None] * dist[None].astype(np.float32)),
                    -np.inf).astype(np.float32)
    first = np.where((ki >= BLOCK)[None], bias, -np.inf).astype(np.float32)
    return jnp.asarray(np.stack([first, bias]))


def _swa_finish(bias_ref, sink_ref, scores, o_ref):
    lane = lax.broadcasted_iota(jnp.int32, (BLOCK, LANES), 1)
    left = lane < HEAD_DIM
    for j in range(A_KV_HEADS):
        outs = []
        for side in range(2):
            s2, vh = scores[2 * j + side]
            ps, inv_den = [], []
            for t in range(2):
                h = A_GROUP * j + 2 * t + side
                s = s2[t * BLOCK:(t + 1) * BLOCK] + bias_ref[0, h]
                sink = sink_ref[h]
                m = jnp.maximum(jnp.max(s, axis=-1, keepdims=True), sink)
                p = jnp.exp(s - m)
                den = jnp.sum(p, axis=-1, keepdims=True) + jnp.exp(sink - m)
                ps.append(p.astype(jnp.bfloat16))
                inv_den.append(1.0 / den)
            o2 = jnp.dot(jnp.concatenate(ps, axis=0), vh, preferred_element_type=jnp.float32)
            outs.append((o2[:BLOCK] * inv_den[0], o2[BLOCK:] * inv_den[1]))
        for t in range(2):
            out = jnp.where(left, outs[0][t], outs[1][t])
            o_ref[:, (2 * j + t) * LANES:(2 * j + t + 1) * LANES] = out.astype(o_ref.dtype)


N_SWA_IN = 8


def _swa_in_specs(step_of):
    qa_blk = QA_OFF // A_Q_W
    ka_blk = KA_OFF // A_KV_W
    va_blk = VA_OFF // A_KV_W
    cur = lambda *g: step_of(*g)
    prev = lambda *g: jnp.maximum(step_of(*g) - 1, 0)
    return [
        pl.BlockSpec(memory_space=pltpu.SMEM),
        pl.BlockSpec((BLOCK, A_Q_W), lambda *g: (cur(*g), qa_blk)),
        pl.BlockSpec((BLOCK, A_KV_W), lambda *g: (prev(*g), ka_blk)),
        pl.BlockSpec((BLOCK, A_KV_W), lambda *g: (cur(*g), ka_blk)),
        pl.BlockSpec((BLOCK, A_KV_W), lambda *g: (prev(*g), va_blk)),
        pl.BlockSpec((BLOCK, A_KV_W), lambda *g: (cur(*g), va_blk)),
        pl.BlockSpec((A_KV_W, A_KV_W), lambda *g: (0, 0)),
        pl.BlockSpec((1, A_Q_HEADS, BLOCK, 2 * BLOCK),
                     lambda *g: (jnp.minimum(step_of(*g), 1), 0, 0, 0)),
    ]


STICK_FIRST_BLOCKS = 3
STICK_TOP_ROWS = 32
SOFTPLUS_CLAMP = 40.0
MASKED = 1e30


def _stick_constant():
    j = np.arange(BLOCK)[:, None]
    s = np.arange(BLOCK)[None, :]
    tri = (j >= s).astype(np.float32)
    zero = np.zeros_like(tri)
    return jnp.asarray(np.block([[tri, zero], [zero, tri]]), jnp.bfloat16)


def _row_total(sums):
    return jnp.broadcast_to(sums[:, :1], sums.shape)


def _softplus(z):
    return jnp.maximum(z, jnp.log(1.0 + jnp.exp(jnp.minimum(z, SOFTPLUS_CLAMP))))


def _stick_scores_clamped(lhs, k_ref, blk):
    nk = STICK_FIRST_BLOCKS
    first = jnp.maximum(blk - (nk - 1), 0)
    start = pl.multiple_of(first * BLOCK, BLOCK)
    z = lax.dot_general(lhs, k_ref[pl.ds(start, nk * BLOCK), :], _NT,
                        preferred_element_type=jnp.float32)
    return [z[:, n * BLOCK:(n + 1) * BLOCK] for n in range(nk)], first


def _stick_masked(z, blk, first, mask_all):
    nk = STICK_FIRST_BLOCKS
    row = lax.broadcasted_iota(jnp.int32, (2 * BLOCK, BLOCK), 0) & (BLOCK - 1)
    col = lax.broadcasted_iota(jnp.int32, (2 * BLOCK, BLOCK), 1)
    out = []
    for n in range(nk):
        zn = z[n]
        if mask_all:
            m = (first + n - blk) * BLOCK + col < row
        elif n == nk - 1:
            m = col < row
        else:
            m = None
        if m is not None:
            zn = jnp.where(m, zn, -MASKED)
        out.append((zn, _softplus(zn).astype(jnp.bfloat16)))
    return out


def _stick_weights(terms):
    tail = None
    a = [None] * len(terms)
    for n in reversed(range(len(terms))):
        zn, sums = terms[n]
        e = zn - sums
        if tail is not None:
            e = e - tail
        a[n] = jnp.exp(e).astype(jnp.bfloat16)
        tail = _row_total(sums) if tail is None else tail + _row_total(sums)
    return a, tail


def _stick_values_clamped(a, v_ref, first):
    start = pl.multiple_of(first * BLOCK, BLOCK)
    v0 = v_ref[pl.ds(start, STICK_FIRST_BLOCKS * BLOCK), :]
    return jnp.dot(jnp.concatenate(a, axis=1), v0, preferred_element_type=jnp.float32)


def _stick_live(tail):
    return (jnp.min(tail) < -F32_EXP_ZERO).astype(jnp.int32)


def _top_rows(x):
    return jnp.concatenate([x[:STICK_TOP_ROWS], x[BLOCK:BLOCK + STICK_TOP_ROWS]], axis=0)


def _add_to_top_rows(x, top):
    t = STICK_TOP_ROWS
    return jnp.concatenate([x[:t] + top[:t], x[t:BLOCK], x[BLOCK:BLOCK + t] + top[t:],
                            x[BLOCK + t:]], axis=0)


def _stick_near(lhs, k_ref, v_ref, blk):
    start = pl.multiple_of((blk - 1) * BLOCK, BLOCK)
    far = pl.multiple_of((blk - 2) * BLOCK, BLOCK)
    z = lax.dot_general(lhs, k_ref[pl.ds(start, 2 * BLOCK), :], _NT,
                        preferred_element_type=jnp.float32)
    zt = lax.dot_general(_top_rows(lhs), k_ref[pl.ds(far, BLOCK), :], _NT,
                         preferred_element_type=jnp.float32)
    row = lax.broadcasted_iota(jnp.int32, (2 * BLOCK, BLOCK), 0) & (BLOCK - 1)
    col = lax.broadcasted_iota(jnp.int32, (2 * BLOCK, BLOCK), 1)
    z_prev = z[:, :BLOCK]
    z_diag = jnp.where(col < row, z[:, BLOCK:], -MASKED)
    return (z_prev, z_diag, zt), (v_ref[pl.ds(start, 2 * BLOCK), :], v_ref[pl.ds(far, BLOCK), :])


def _stick_near_values(zs, sums, sums_top, vs):
    z_prev, z_diag, zt = zs
    s_prev, s_diag = sums[:, :BLOCK], sums[:, BLOCK:]
    a_diag = jnp.exp(z_diag - s_diag)
    tail = _row_total(s_diag)
    a_prev = jnp.exp(z_prev - s_prev - tail)
    tail = tail + _row_total(s_prev)
    a_top = jnp.exp(zt - sums_top - _top_rows(tail))
    acc = jnp.dot(jnp.concatenate([a_prev, a_diag], axis=1).astype(jnp.bfloat16), vs[0],
                  preferred_element_type=jnp.float32)
    acc_top = jnp.dot(a_top.astype(jnp.bfloat16), vs[1], preferred_element_type=jnp.float32)
    return _add_to_top_rows(acc, acc_top), _add_to_top_rows(tail, _row_total(sums_top))


def _stick_rest(lhs, k_ref, v_ref, c, acc, tail, first, live0, top_done=None):
    live = _stick_live

    def cond(carry):
        kb, alive, _, _ = carry
        return jnp.logical_and(kb >= 0, alive > 0)

    def body(carry):
        kb, _, acc, tail = carry
        st = pl.multiple_of(kb * BLOCK, BLOCK)
        z = lax.dot_general(lhs, k_ref[pl.ds(st, BLOCK), :], _NT,
                            preferred_element_type=jnp.float32)
        if top_done is not None:
            row = lax.broadcasted_iota(jnp.int32, z.shape, 0) & (BLOCK - 1)
            z = jnp.where(jnp.logical_and(row < STICK_TOP_ROWS, kb == top_done), -MASKED, z)
        sums = jnp.dot(_softplus(z).astype(jnp.bfloat16), c[:BLOCK, :BLOCK],
                       preferred_element_type=jnp.float32)
        a = jnp.exp(z - sums - tail)
        acc = acc + jnp.dot(a.astype(jnp.bfloat16), v_ref[pl.ds(st, BLOCK), :],
                            preferred_element_type=jnp.float32)
        tail = tail + _row_total(sums)
        return kb - 1, live(tail), acc, tail

    _, _, acc, _ = lax.while_loop(cond, body, (first - 1, live0, acc, tail))
    return acc


def _attn_kernel(*refs, q_blocks, n_cast):
    swa_in, refs = refs[:N_SWA_IN], refs[N_SWA_IN:]
    q_ref, k_ref, v_ref, c_ref = refs[:4]
    w_refs = refs[4:4 + n_cast]
    oa_ref, o_ref = refs[4 + n_cast:6 + n_cast]
    wo_refs = refs[6 + n_cast:]
    it = pl.program_id(1)
    lane = lax.broadcasted_iota(jnp.int32, (BLOCK, LANES), 1)
    left = lane < HEAD_DIM
    c = c_ref[...]

    def run(first_step):
        for w_ref, wo_ref in zip(w_refs, wo_refs):
            wo_ref[...] = w_ref[...].astype(wo_ref.dtype)
        blks = [it * q_blocks + g for g in range(q_blocks)]
        lhss = []
        for g in range(q_blocks):
            q = q_ref[g * BLOCK:(g + 1) * BLOCK, :] * jnp.bfloat16(SCALE)
            zero = jnp.zeros_like(q)
            lhss.append(jnp.concatenate([jnp.where(left, q, zero), jnp.where(left, zero, q)], axis=0))
        _swa_finish(swa_in[7], swa_in[0], _swa_scores(*swa_in[1:7]), oa_ref)
        nk = STICK_FIRST_BLOCKS
        rows = 2 * BLOCK
        if first_step:
            clamped = [_stick_scores_clamped(lhss[g], k_ref, blks[g]) for g in range(q_blocks)]
            z = [zs for zs, _ in clamped]
            firsts = [first for _, first in clamped]
            masked = [_stick_masked(z[g], blks[g], firsts[g], g < nk - 1) for g in range(q_blocks)]
            sps = [sp for m in masked for _, sp in m]
            sums = jnp.dot(
                jnp.concatenate([jnp.concatenate(sps[t:t + 2], axis=1)
                                 for t in range(0, len(sps), 2)], axis=0),
                c, preferred_element_type=jnp.float32)

            def block_sums(t):
                return sums[(t // 2) * rows:(t // 2 + 1) * rows,
                            (t % 2) * BLOCK:(t % 2 + 1) * BLOCK]

            terms = [[(masked[g][n][0], block_sums(g * nk + n)) for n in range(nk)]
                     for g in range(q_blocks)]
            weights = [_stick_weights(terms[g]) for g in range(q_blocks)]
            accs = [_stick_values_clamped(weights[g][0], v_ref, firsts[g])
                    for g in range(q_blocks)]
            tails = [tail for _, tail in weights]
            top_done = [None] * q_blocks
        else:
            near = [_stick_near(lhss[g], k_ref, v_ref, blks[g]) for g in range(q_blocks)]
            sp = [[_softplus(zn).astype(jnp.bfloat16) for zn in zs] for zs, _ in near]
            sums = jnp.dot(jnp.concatenate([jnp.concatenate(s[:2], axis=1) for s in sp], axis=0),
                           c, preferred_element_type=jnp.float32)
            top = 2 * STICK_TOP_ROWS
            sums_top = jnp.dot(jnp.concatenate([s[2] for s in sp], axis=0), c[:BLOCK, :BLOCK],
                               preferred_element_type=jnp.float32)
            vals = [_stick_near_values(near[g][0], sums[g * rows:(g + 1) * rows],
                                       sums_top[g * top:(g + 1) * top], near[g][1])
                    for g in range(q_blocks)]
            accs = [acc for acc, _ in vals]
            tails = [tail for _, tail in vals]
            firsts = [blk - 1 for blk in blks]
            top_done = [blk - 2 for blk in blks]
        lives = [_stick_live(tail) for tail in tails]
        for g in range(q_blocks):
            acc = _stick_rest(lhss[g], k_ref, v_ref, c, accs[g], tails[g], firsts[g], lives[g],
                              top_done[g])
            o_ref[g * BLOCK:(g + 1) * BLOCK, :] = jnp.where(
                left, acc[:BLOCK], acc[BLOCK:]).astype(o_ref.dtype)

    assert q_blocks >= STICK_FIRST_BLOCKS - 1
    pl.when(it == 0)(lambda: run(True))
    pl.when(it > 0)(lambda: run(False))


def _cast_block_spec(shape, steps, n_inner):
    rows, cols = shape
    for n_cb in range(1, steps + 1):
        n_rb = steps // n_cb
        if steps % n_cb == 0 and rows % n_rb == 0 and cols % n_cb == 0 \
                and (rows // n_rb) % BF16_ROWS == 0 and (cols // n_cb) % LANES == 0:
            return pl.BlockSpec(
                (rows // n_rb, cols // n_cb),
                lambda b, i: ((b * n_inner + i) // n_cb, (b * n_inner + i) % n_cb))
    raise ValueError(f"cannot tile {shape} into {steps} blocks")


def _attention(proj, sinks, weights, *, q_blocks=8):
    s = proj.shape[0]
    tq = q_blocks * BLOCK
    n_q = s // tq
    steps = PAIRS_B * n_q
    assert steps == s // BLOCK
    c = _stick_constant()
    qb_blk = QB_OFF // LANES
    kb_blk = KB_OFF // LANES
    vb_blk = VB_OFF // LANES
    step_of = lambda b, i: b * n_q + i
    w_specs = [_cast_block_spec(w.shape, steps, n_q) for w in weights]
    outs = pl.pallas_call(
        functools.partial(_attn_kernel, q_blocks=q_blocks, n_cast=len(weights)),
        grid=(PAIRS_B, n_q),
        in_specs=[
            *_swa_in_specs(step_of),
            pl.BlockSpec((tq, LANES), lambda b, i: (i, qb_blk + b)),
            pl.BlockSpec((s, LANES), lambda b, i: (0, kb_blk + b)),
            pl.BlockSpec((s, LANES), lambda b, i: (0, vb_blk + b)),
            pl.BlockSpec(c.shape, lambda b, i: (0, 0)),
            *w_specs,
        ],
        out_specs=[pl.BlockSpec((BLOCK, A_Q_W), lambda b, i: (step_of(b, i), 0)),
                   pl.BlockSpec((tq, LANES), lambda b, i: (i, b)), *w_specs],
        out_shape=[jax.ShapeDtypeStruct((s, A_Q_W), jnp.bfloat16),
                   jax.ShapeDtypeStruct((s, B_W), jnp.bfloat16)]
        + [jax.ShapeDtypeStruct(w.shape, jnp.bfloat16) for w in weights],
        compiler_params=pltpu.CompilerParams(
            dimension_semantics=("arbitrary", "arbitrary"), vmem_limit_bytes=VMEM_LIMIT),
        name="attention",
    )(sinks, *([proj] * 5), _half_swap_matrix(A_KV_W), _swa_bias(), proj, proj, proj, c, *weights)
    return outs[0], outs[1], outs[2:]


def _merge_kernel(*refs, n_chunks):
    oa_ref, ob_ref = refs[0], refs[1]
    ga_refs = refs[2:2 + n_chunks]
    gb_refs = refs[2 + n_chunks:2 + 2 * n_chunks]
    x_ref, wa_ref, wb_ref, wo_ref, o_ref, m_ref = refs[2 + 2 * n_chunks:]
    tn = o_ref.shape[1] // n_chunks
    for c in range(n_chunks):
        sl = slice(c * tn, (c + 1) * tn)
        ya = jnp.dot(oa_ref[...], wa_ref[:, sl], preferred_element_type=jnp.float32)
        yb = jnp.dot(ob_ref[...], wb_ref[:, sl], preferred_element_type=jnp.float32)
        ga = jax.nn.sigmoid(ga_refs[c][...].astype(jnp.float32))
        gb = jax.nn.sigmoid(gb_refs[c][...].astype(jnp.float32))
        m_ref[:, sl] = (ga * ya + gb * yb).astype(m_ref.dtype)
    for c in range(n_chunks):
        sl = slice(c * tn, (c + 1) * tn)
        o_ref[:, sl] = x_ref[:, sl] + jnp.dot(m_ref[...], wo_ref[:, sl],
                                              preferred_element_type=jnp.float32)


def _merge(oa, ob, proj, x, wa, wb, wo, *, tm=512, tn=512):
    s, d = x.shape
    n_chunks = d // tn
    const = lambda i: (0, 0)
    gate_specs = [pl.BlockSpec((tm, tn), functools.partial(lambda i, blk: (i, blk), blk=off // tn + c))
                  for off in (GA_OFF, GB_OFF) for c in range(n_chunks)]
    return pl.pallas_call(
        functools.partial(_merge_kernel, n_chunks=n_chunks),
        grid=(s // tm,),
        in_specs=[
            pl.BlockSpec((tm, A_Q_W), lambda i: (i, 0)),
            pl.BlockSpec((tm, B_W), lambda i: (i, 0)),
            *gate_specs,
            pl.BlockSpec((tm, d), lambda i: (i, 0)),
            pl.BlockSpec(wa.shape, const),
            pl.BlockSpec(wb.shape, const),
            pl.BlockSpec(wo.shape, const),
        ],
        out_specs=pl.BlockSpec((tm, d), lambda i: (i, 0)),
        out_shape=jax.ShapeDtypeStruct((s, d), jnp.float32),
        scratch_shapes=[pltpu.VMEM((tm, d), jnp.bfloat16)],
        compiler_params=pltpu.CompilerParams(
            dimension_semantics=("arbitrary",), vmem_limit_bytes=VMEM_LIMIT),
        name="merge",
    )(oa, ob, *([proj] * (2 * n_chunks)), x, wa, wb, wo)


def _ffn_kernel(x_hbm, g_ref, wg_ref, wu_ref, wd_ref, gf_ref, o_ref, xbuf, sem, h_ref, *,
                row_chunk):
    f = pl.program_id(1)

    def start_row_block():
        _rmsnorm_rows(xbuf, h_ref, g_ref[...], row_chunk)
        o_ref[...] = xbuf[...]

    _consume_row_block(x_hbm, xbuf, sem, start_row_block)

    h = h_ref[...]
    gate = jnp.dot(h, wg_ref[...], preferred_element_type=jnp.float32)
    up = jnp.dot(h, wu_ref[...], preferred_element_type=jnp.float32)
    act = (gate * jax.nn.sigmoid(gate) * up).astype(jnp.bfloat16)
    o_ref[...] += jnp.dot(act, wd_ref[...], preferred_element_type=jnp.float32)

    @pl.when(f == pl.num_programs(1) - 1)
    def _():
        _rmsnorm_rows(o_ref, o_ref, gf_ref[...], row_chunk)


def _ffn(x1, g, w_in, w_down, gf, *, tm=1024, tf=512):
    s, d = x1.shape
    nf = D_FF // tf
    return pl.pallas_call(
        functools.partial(_ffn_kernel, row_chunk=NORM_ROWS),
        grid=(s // tm, nf),
        in_specs=[
            pl.BlockSpec(memory_space=pl.ANY),
            pl.BlockSpec((1, d), lambda i, f: (0, 0)),
            pl.BlockSpec((d, tf), lambda i, f: (0, f)),
            pl.BlockSpec((d, tf), lambda i, f: (0, f + nf)),
            pl.BlockSpec((tf, d), lambda i, f: (f, 0)),
            pl.BlockSpec((1, d), lambda i, f: (0, 0)),
        ],
        out_specs=pl.BlockSpec((tm, d), lambda i, f: (i, 0)),
        out_shape=jax.ShapeDtypeStruct((s, d), jnp.float32),
        scratch_shapes=[pltpu.VMEM((tm, d), x1.dtype), pltpu.SemaphoreType.DMA(()),
                        pltpu.VMEM((tm, d), jnp.bfloat16)],
        compiler_params=pltpu.CompilerParams(
            dimension_semantics=("arbitrary", "arbitrary"), vmem_limit_bytes=VMEM_LIMIT),
        name="ffn",
    )(x1, g, w_in, w_in, w_down, gf)


def kernel(x, norm_mix_g, w_in, sink_logits, w_branch_a, w_branch_b, w_out,
           norm_ffn_g, w_ffn_in, w_ffn_down, norm_final_g):
    b, s, d = x.shape
    assert (b, s, d) == (1, SEQ, D_MODEL) and w_in.shape[0] == 1
    x2 = x.reshape(s, d)
    proj = _in_proj(x2, norm_mix_g[0].reshape(1, d), w_in[0])
    f32_weights = [w_branch_a[0], w_branch_b[0], w_out[0], w_ffn_in[0], w_ffn_down[0]]
    oa, ob, (wa, wb, wo, wfi, wfd) = _attention(proj, sink_logits[0], f32_weights)
    x1 = _merge(oa, ob, proj, x2, wa, wb, wo)
    out = _ffn(x1, norm_ffn_g[0].reshape(1, d), wfi, wfd, norm_final_g.reshape(1, d))
    return out.reshape(b, s, d)
```

```python
import functools
import math

import numpy as np
import jax
import jax.numpy as jnp
from jax import lax
from jax.experimental import pallas as pl
from jax.experimental.pallas import tpu as pltpu

D_MODEL = 2048
SEQ = 8192
HEAD_DIM = 64
A_Q_HEADS = 16
A_KV_HEADS = 4
A_GROUP = A_Q_HEADS // A_KV_HEADS
WINDOW = 128
B_HEADS = 16
BLOCK = 128
D_FF = 5632
EPS = 1e-6
SCALE = 1.0 / math.sqrt(HEAD_DIM)

A_Q_W = A_Q_HEADS * HEAD_DIM
A_KV_W = A_KV_HEADS * HEAD_DIM
B_W = B_HEADS * HEAD_DIM
IN_WIDTH = A_Q_W + 2 * A_KV_W + 3 * B_W + 2 * D_MODEL

LANES = 128
BF16_ROWS = 16
PAIRS_B = B_HEADS // 2

QA_OFF = 0
KA_OFF = QA_OFF + A_Q_W
VA_OFF = KA_OFF + A_KV_W
QB_OFF = VA_OFF + A_KV_W
KB_OFF = QB_OFF + B_W
VB_OFF = KB_OFF + B_W
GA_OFF = VB_OFF + B_W
GB_OFF = GA_OFF + D_MODEL

F32_EXP_ZERO = -104.0

VMEM_LIMIT = 56 * 1024 * 1024
NORM_ROWS = 128

_NT = (((1,), (1,)), ((), ()))


def _rmsnorm_rows(src_ref, dst_ref, g, row_chunk):
    def body(c, _):
        r = pl.multiple_of(c * row_chunk, row_chunk)
        x = src_ref[pl.ds(r, row_chunk), :]
        inv = lax.rsqrt(jnp.mean(x * x, axis=-1, keepdims=True) + EPS)
        dst_ref[pl.ds(r, row_chunk), :] = (x * inv * g).astype(dst_ref.dtype)
        return 0

    lax.fori_loop(0, src_ref.shape[0] // row_chunk, body, 0)


def _consume_row_block(x_hbm, xbuf, sem, consume):
    i = pl.program_id(0)
    rows = xbuf.shape[0]

    def copy(blk):
        return pltpu.make_async_copy(x_hbm.at[pl.ds(blk * rows, rows), :], xbuf, sem)

    @pl.when(pl.program_id(1) == 0)
    def _():
        @pl.when(i == 0)
        def _():
            copy(0).start()

        copy(i).wait()
        consume()

        @pl.when(i + 1 < pl.num_programs(0))
        def _():
            copy(i + 1).start()


def _in_proj_kernel(x_hbm, g_ref, w_ref, o_ref, xbuf, sems, h_ref, *, row_block):
    chunk = xbuf.shape[1]
    n_chunks = h_ref.shape[0] // chunk

    def copy(c, slot):
        return pltpu.make_async_copy(x_hbm.at[pl.ds(c * chunk, chunk), :], xbuf.at[slot],
                                     sems.at[slot])

    @pl.when(pl.program_id(0) == 0)
    def _():
        copy(0, 0).start()

        def body(c, _):
            slot = c % 2

            @pl.when(c + 1 < n_chunks)
            def _():
                copy(c + 1, 1 - slot).start()

            copy(c, slot).wait()
            x = xbuf[slot]
            inv = lax.rsqrt(jnp.mean(x * x, axis=-1, keepdims=True) + EPS)
            r = pl.multiple_of(c * chunk, chunk)
            h_ref[pl.ds(r, chunk), :] = (x * inv * g_ref[...]).astype(h_ref.dtype)
            return 0

        lax.fori_loop(0, n_chunks, body, 0)

    w = w_ref[...].astype(jnp.bfloat16)
    for r in range(0, h_ref.shape[0], row_block):
        o_ref[r:r + row_block, :] = jnp.dot(
            h_ref[r:r + row_block, :], w, preferred_element_type=jnp.float32).astype(o_ref.dtype)


def _in_proj(x, g, w, *, tn=256, chunk=2 * NORM_ROWS, row_block=2048):
    s, d = x.shape
    n = w.shape[1]
    return pl.pallas_call(
        functools.partial(_in_proj_kernel, row_block=row_block),
        grid=(n // tn,),
        in_specs=[
            pl.BlockSpec(memory_space=pl.ANY),
            pl.BlockSpec((1, d), lambda j: (0, 0)),
            pl.BlockSpec((d, tn), lambda j: (0, j)),
        ],
        out_specs=pl.BlockSpec((s, tn), lambda j: (0, j)),
        out_shape=jax.ShapeDtypeStruct((s, n), jnp.bfloat16),
        scratch_shapes=[pltpu.VMEM((2, chunk, d), x.dtype), pltpu.SemaphoreType.DMA((2,)),
                        pltpu.VMEM((s, d), jnp.bfloat16)],
        compiler_params=pltpu.CompilerParams(
            dimension_semantics=("arbitrary",), vmem_limit_bytes=VMEM_LIMIT),
        name="in_proj",
    )(x, g, w)


def _half_swap_matrix(n):
    c = np.arange(n)
    p = np.zeros((n, n), np.float32)
    p[c, c ^ HEAD_DIM] = 1.0
    return jnp.asarray(p, jnp.bfloat16)


def _swa_scores(q_ref, kp_ref, kc_ref, vp_ref, vc_ref, p_ref):
    k = jnp.concatenate([kp_ref[...], kc_ref[...]], axis=0)
    v = jnp.concatenate([vp_ref[...], vc_ref[...]], axis=0)
    k_sw = jnp.dot(k, p_ref[...], preferred_element_type=jnp.float32).astype(k.dtype)
    v_sw = jnp.dot(v, p_ref[...], preferred_element_type=jnp.float32).astype(v.dtype)
    lane = lax.broadcasted_iota(jnp.int32, (BLOCK, LANES), 1)
    left = lane < HEAD_DIM
    out = []
    for j in range(A_KV_HEADS):
        tile = slice((j // 2) * LANES, (j // 2 + 1) * LANES)
        qa = q_ref[:, (2 * j) * LANES:(2 * j + 1) * LANES] * jnp.bfloat16(SCALE)
        qb = q_ref[:, (2 * j + 1) * LANES:(2 * j + 2) * LANES] * jnp.bfloat16(SCALE)
        zero = jnp.zeros_like(qa)
        for side in range(2):
            keep = left if side == 0 else jnp.logical_not(left)
            aligned = (j % 2) == side
            kh = (k if aligned else k_sw)[:, tile]
            vh = (v if aligned else v_sw)[:, tile]
            lhs = jnp.concatenate([jnp.where(keep, qa, zero), jnp.where(keep, qb, zero)], axis=0)
            out.append((lax.dot_general(lhs, kh, _NT, preferred_element_type=jnp.float32), vh))
    return out


def _swa_bias():
    qi = np.arange(BLOCK)[:, None]
    ki = np.arange(2 * BLOCK)[None, :]
    dist = BLOCK + qi - ki
    window = (dist >= 0) & (dist < WINDOW)
    heads = np.arange(1, A_Q_HEADS + 1, dtype=np.float32)
    slopes = np.exp2(np.float32(-8.0) * heads / np.float32(A_Q_HEADS)).astype(np.float32)
    bias = np.where(window[None], -(slopes[:, None, None] * dist[None].astype(np.float32)),
                    -np.inf).astype(np.float32)
    first = np.where((ki >= BLOCK)[None], bias, -np.inf).astype(np.float32)
    both = np.stack([first, bias])
    prev_half = (np.arange(BLOCK)[None, :] > qi)[None, None]
    return jnp.asarray(np.where(prev_half, both[..., :BLOCK], both[..., BLOCK:]))


def _swa_finish(bias_ref, sink_ref, scores, o_ref):
    lane = lax.broadcasted_iota(jnp.int32, (BLOCK, LANES), 1)
    left = lane < HEAD_DIM
    prev_half = lane > lax.broadcasted_iota(jnp.int32, (BLOCK, BLOCK), 0)
    for j in range(A_KV_HEADS):
        outs = []
        for side in range(2):
            s2, vh = scores[2 * j + side]
            ps, inv_den = [], []
            for t in range(2):
                h = A_GROUP * j + 2 * t + side
                st = s2[t * BLOCK:(t + 1) * BLOCK]
                s = jnp.where(prev_half, st[:, :BLOCK], st[:, BLOCK:]) + bias_ref[0, h]
                sink = sink_ref[h]
                m = jnp.maximum(jnp.max(s, axis=-1, keepdims=True), sink)
                p = jnp.exp(s - m)
                den = jnp.sum(p, axis=-1, keepdims=True) + jnp.exp(sink - m)
                pb = p.astype(jnp.bfloat16)
                zero = jnp.zeros_like(pb)
                ps.append(jnp.concatenate([jnp.where(prev_half, pb, zero),
                                           jnp.where(prev_half, zero, pb)], axis=1))
                inv_den.append(1.0 / den)
            o2 = jnp.dot(jnp.concatenate(ps, axis=0), vh, preferred_element_type=jnp.float32)
            outs.append((o2[:BLOCK] * inv_den[0], o2[BLOCK:] * inv_den[1]))
        for t in range(2):
            out = jnp.where(left, outs[0][t], outs[1][t])
            o_ref[:, (2 * j + t) * LANES:(2 * j + t + 1) * LANES] = out.astype(o_ref.dtype)


N_SWA_IN = 8


def _swa_in_specs(step_of):
    qa_blk = QA_OFF // A_Q_W
    ka_blk = KA_OFF // A_KV_W
    va_blk = VA_OFF // A_KV_W
    cur = lambda *g: step_of(*g)
    prev = lambda *g: jnp.maximum(step_of(*g) - 1, 0)
    return [
        pl.BlockSpec(memory_space=pltpu.SMEM),
        pl.BlockSpec((BLOCK, A_Q_W), lambda *g: (cur(*g), qa_blk)),
        pl.BlockSpec((BLOCK, A_KV_W), lambda *g: (prev(*g), ka_blk)),
        pl.BlockSpec((BLOCK, A_KV_W), lambda *g: (cur(*g), ka_blk)),
        pl.BlockSpec((BLOCK, A_KV_W), lambda *g: (prev(*g), va_blk)),
        pl.BlockSpec((BLOCK, A_KV_W), lambda *g: (cur(*g), va_blk)),
        pl.BlockSpec((A_KV_W, A_KV_W), lambda *g: (0, 0)),
        pl.BlockSpec((1, A_Q_HEADS, BLOCK, BLOCK),
                     lambda *g: (jnp.minimum(step_of(*g), 1), 0, 0, 0)),
    ]


STICK_FIRST_BLOCKS = 3
STICK_TOP_ROWS = 32
SOFTPLUS_CLAMP = 40.0
MASKED = 1e30


def _stick_constant():
    j = np.arange(BLOCK)[:, None]
    s = np.arange(BLOCK)[None, :]
    tri = (j >= s).astype(np.float32)
    zero = np.zeros_like(tri)
    return jnp.asarray(np.block([[tri, zero], [zero, tri]]), jnp.bfloat16)


def _row_total(sums):
    return jnp.broadcast_to(sums[:, :1], sums.shape)


def _softplus(z):
    return jnp.maximum(z, jnp.log(1.0 + jnp.exp(jnp.minimum(z, SOFTPLUS_CLAMP))))


def _stick_scores_clamped(lhs, k_ref, blk):
    nk = STICK_FIRST_BLOCKS
    first = jnp.maximum(blk - (nk - 1), 0)
    start = pl.multiple_of(first * BLOCK, BLOCK)
    z = lax.dot_general(lhs, k_ref[pl.ds(start, nk * BLOCK), :], _NT,
                        preferred_element_type=jnp.float32)
    return [z[:, n * BLOCK:(n + 1) * BLOCK] for n in range(nk)], first


def _stick_masked(z, blk, first, mask_all):
    nk = STICK_FIRST_BLOCKS
    row = lax.broadcasted_iota(jnp.int32, (2 * BLOCK, BLOCK), 0) & (BLOCK - 1)
    col = lax.broadcasted_iota(jnp.int32, (2 * BLOCK, BLOCK), 1)
    out = []
    for n in range(nk):
        zn = z[n]
        if mask_all:
            m = (first + n - blk) * BLOCK + col < row
        elif n == nk - 1:
            m = col < row
        else:
            m = None
        if m is not None:
            zn = jnp.where(m, zn, -MASKED)
        out.append((zn, _softplus(zn).astype(jnp.bfloat16)))
    return out


def _stick_weights(terms):
    tail = None
    a = [None] * len(terms)
    for n in reversed(range(len(terms))):
        zn, sums = terms[n]
        e = zn - sums
        if tail is not None:
            e = e - tail
        a[n] = jnp.exp(e).astype(jnp.bfloat16)
        tail = _row_total(sums) if tail is None else tail + _row_total(sums)
    return a, tail


def _stick_values_clamped(a, v_ref, first):
    start = pl.multiple_of(first * BLOCK, BLOCK)
    v0 = v_ref[pl.ds(start, STICK_FIRST_BLOCKS * BLOCK), :]
    return jnp.dot(jnp.concatenate(a, axis=1), v0, preferred_element_type=jnp.float32)


def _stick_live(tail):
    return (jnp.min(tail) < -F32_EXP_ZERO).astype(jnp.int32)


def _top_rows(x):
    return jnp.concatenate([x[:STICK_TOP_ROWS], x[BLOCK:BLOCK + STICK_TOP_ROWS]], axis=0)


def _add_to_top_rows(x, top):
    t = STICK_TOP_ROWS
    return jnp.concatenate([x[:t] + top[:t], x[t:BLOCK], x[BLOCK:BLOCK + t] + top[t:],
                            x[BLOCK + t:]], axis=0)


def _stick_near(lhs, k_ref, v_ref, blk):
    start = pl.multiple_of((blk - 1) * BLOCK, BLOCK)
    far = pl.multiple_of((blk - 2) * BLOCK, BLOCK)
    z = lax.dot_general(lhs, k_ref[pl.ds(start, 2 * BLOCK), :], _NT,
                        preferred_element_type=jnp.float32)
    zt = lax.dot_general(_top_rows(lhs), k_ref[pl.ds(far, BLOCK), :], _NT,
                         preferred_element_type=jnp.float32)
    row = lax.broadcasted_iota(jnp.int32, (2 * BLOCK, BLOCK), 0) & (BLOCK - 1)
    col = lax.broadcasted_iota(jnp.int32, (2 * BLOCK, BLOCK), 1)
    z_prev = z[:, :BLOCK]
    z_diag = jnp.where(col < row, z[:, BLOCK:], -MASKED)
    return (z_prev, z_diag, zt), (v_ref[pl.ds(start, 2 * BLOCK), :], v_ref[pl.ds(far, BLOCK), :])


def _stick_near_values(zs, sums, sums_top, vs):
    z_prev, z_diag, zt = zs
    s_prev, s_diag = sums[:, :BLOCK], sums[:, BLOCK:]
    a_diag = jnp.exp(z_diag - s_diag)
    tail = _row_total(s_diag)
    a_prev = jnp.exp(z_prev - s_prev - tail)
    tail = tail + _row_total(s_prev)
    a_top = jnp.exp(zt - sums_top - _top_rows(tail))
    acc = jnp.dot(jnp.concatenate([a_prev, a_diag], axis=1).astype(jnp.bfloat16), vs[0],
                  preferred_element_type=jnp.float32)
    acc_top = jnp.dot(a_top.astype(jnp.bfloat16), vs[1], preferred_element_type=jnp.float32)
    return _add_to_top_rows(acc, acc_top), _add_to_top_rows(tail, _row_total(sums_top))


def _stick_rest(lhs, k_ref, v_ref, c, acc, tail, first, live0, top_done=None):
    live = _stick_live

    def cond(carry):
        kb, alive, _, _ = carry
        return jnp.logical_and(kb >= 0, alive > 0)

    def body(carry):
        kb, _, acc, tail = carry
        st = pl.multiple_of(kb * BLOCK, BLOCK)
        z = lax.dot_general(lhs, k_ref[pl.ds(st, BLOCK), :], _NT,
                            preferred_element_type=jnp.float32)
        if top_done is not None:
            row = lax.broadcasted_iota(jnp.int32, z.shape, 0) & (BLOCK - 1)
            z = jnp.where(jnp.logical_and(row < STICK_TOP_ROWS, kb == top_done), -MASKED, z)
        sums = jnp.dot(_softplus(z).astype(jnp.bfloat16), c[:BLOCK, :BLOCK],
                       preferred_element_type=jnp.float32)
        a = jnp.exp(z - sums - tail)
        acc = acc + jnp.dot(a.astype(jnp.bfloat16), v_ref[pl.ds(st, BLOCK), :],
                            preferred_element_type=jnp.float32)
        tail = tail + _row_total(sums)
        return kb - 1, live(tail), acc, tail

    _, _, acc, _ = lax.while_loop(cond, body, (first - 1, live0, acc, tail))
    return acc


def _attn_kernel(*refs, q_blocks, n_cast):
    swa_in, refs = refs[:N_SWA_IN], refs[N_SWA_IN:]
    q_ref, k_ref, v_ref, c_ref = refs[:4]
    w_refs = refs[4:4 + n_cast]
    oa_ref, o_ref = refs[4 + n_cast:6 + n_cast]
    wo_refs = refs[6 + n_cast:]
    it = pl.program_id(1)
    lane = lax.broadcasted_iota(jnp.int32, (BLOCK, LANES), 1)
    left = lane < HEAD_DIM
    c = c_ref[...]

    def run(first_step):
        for w_ref, wo_ref in zip(w_refs, wo_refs):
            wo_ref[...] = w_ref[...].astype(wo_ref.dtype)
        blks = [it * q_blocks + g for g in range(q_blocks)]
        lhss = []
        for g in range(q_blocks):
            q = q_ref[g * BLOCK:(g + 1) * BLOCK, :] * jnp.bfloat16(SCALE)
            zero = jnp.zeros_like(q)
            lhss.append(jnp.concatenate([jnp.where(left, q, zero), jnp.where(left, zero, q)], axis=0))
        _swa_finish(swa_in[7], swa_in[0], _swa_scores(*swa_in[1:7]), oa_ref)
        nk = STICK_FIRST_BLOCKS
        rows = 2 * BLOCK
        if first_step:
            clamped = [_stick_scores_clamped(lhss[g], k_ref, blks[g]) for g in range(q_blocks)]
            z = [zs for zs, _ in clamped]
            firsts = [first for _, first in clamped]
            masked = [_stick_masked(z[g], blks[g], firsts[g], g < nk - 1) for g in range(q_blocks)]
            sps = [sp for m in masked for _, sp in m]
            sums = jnp.dot(
                jnp.concatenate([jnp.concatenate(sps[t:t + 2], axis=1)
                                 for t in range(0, len(sps), 2)], axis=0),
                c, preferred_element_type=jnp.float32)

            def block_sums(t):
                return sums[(t // 2) * rows:(t // 2 + 1) * rows,
                            (t % 2) * BLOCK:(t % 2 + 1) * BLOCK]

            terms = [[(masked[g][n][0], block_sums(g * nk + n)) for n in range(nk)]
                     for g in range(q_blocks)]
            weights = [_stick_weights(terms[g]) for g in range(q_blocks)]
            accs = [_stick_values_clamped(weights[g][0], v_ref, firsts[g])
                    for g in range(q_blocks)]
            tails = [tail for _, tail in weights]
            top_done = [None] * q_blocks
        else:
            near = [_stick_near(lhss[g], k_ref, v_ref, blks[g]) for g in range(q_blocks)]
            sp = [[_softplus(zn).astype(jnp.bfloat16) for zn in zs] for zs, _ in near]
            sums = jnp.dot(jnp.concatenate([jnp.concatenate(s[:2], axis=1) for s in sp], axis=0),
                           c, preferred_element_type=jnp.float32)
            top = 2 * STICK_TOP_ROWS
            sums_top = jnp.dot(jnp.concatenate([s[2] for s in sp], axis=0), c[:BLOCK, :BLOCK],
                               preferred_element_type=jnp.float32)
            vals = [_stick_near_values(near[g][0], sums[g * rows:(g + 1) * rows],
                                       sums_top[g * top:(g + 1) * top], near[g][1])
                    for g in range(q_blocks)]
            accs = [acc for acc, _ in vals]
            tails = [tail for _, tail in vals]
            firsts = [blk - 1 for blk in blks]
            top_done = [blk - 2 for blk in blks]
        lives = [_stick_live(tail) for tail in tails]
        for g in range(q_blocks):
            acc = _stick_rest(lhss[g], k_ref, v_ref, c, accs[g], tails[g], firsts[g], lives[g],
                              top_done[g])
            o_ref[g * BLOCK:(g + 1) * BLOCK, :] = jnp.where(
                left, acc[:BLOCK], acc[BLOCK:]).astype(o_ref.dtype)

    assert q_blocks >= STICK_FIRST_BLOCKS - 1
    pl.when(it == 0)(lambda: run(True))
    pl.when(it > 0)(lambda: run(False))


def _cast_block_spec(shape, steps, n_inner):
    rows, cols = shape
    for n_cb in range(1, steps + 1):
        n_rb = steps // n_cb
        if steps % n_cb == 0 and rows % n_rb == 0 and cols % n_cb == 0 \
                and (rows // n_rb) % BF16_ROWS == 0 and (cols // n_cb) % LANES == 0:
            return pl.BlockSpec(
                (rows // n_rb, cols // n_cb),
                lambda b, i: ((b * n_inner + i) // n_cb, (b * n_inner + i) % n_cb))
    raise ValueError(f"cannot tile {shape} into {steps} blocks")


def _attention(proj, sinks, weights, *, q_blocks=8):
    s = proj.shape[0]
    tq = q_blocks * BLOCK
    n_q = s // tq
    steps = PAIRS_B * n_q
    assert steps == s // BLOCK
    c = _stick_constant()
    qb_blk = QB_OFF // LANES
    kb_blk = KB_OFF // LANES
    vb_blk = VB_OFF // LANES
    step_of = lambda b, i: b * n_q + i
    w_specs = [_cast_block_spec(w.shape, steps, n_q) for w in weights]
    outs = pl.pallas_call(
        functools.partial(_attn_kernel, q_blocks=q_blocks, n_cast=len(weights)),
        grid=(PAIRS_B, n_q),
        in_specs=[
            *_swa_in_specs(step_of),
            pl.BlockSpec((tq, LANES), lambda b, i: (i, qb_blk + b)),
            pl.BlockSpec((s, LANES), lambda b, i: (0, kb_blk + b)),
            pl.BlockSpec((s, LANES), lambda b, i: (0, vb_blk + b)),
            pl.BlockSpec(c.shape, lambda b, i: (0, 0)),
            *w_specs,
        ],
        out_specs=[pl.BlockSpec((BLOCK, A_Q_W), lambda b, i: (step_of(b, i), 0)),
                   pl.BlockSpec((tq, LANES), lambda b, i: (i, b)), *w_specs],
        out_shape=[jax.ShapeDtypeStruct((s, A_Q_W), jnp.bfloat16),
                   jax.ShapeDtypeStruct((s, B_W), jnp.bfloat16)]
        + [jax.ShapeDtypeStruct(w.shape, jnp.bfloat16) for w in weights],
        compiler_params=pltpu.CompilerParams(
            dimension_semantics=("arbitrary", "arbitrary"), vmem_limit_bytes=VMEM_LIMIT),
        name="attention",
    )(sinks, *([proj] * 5), _half_swap_matrix(A_KV_W), _swa_bias(), proj, proj, proj, c, *weights)
    return outs[0], outs[1], outs[2:]


def _merge_kernel(*refs, n_chunks):
    oa_ref, ob_ref = refs[0], refs[1]
    ga_refs = refs[2:2 + n_chunks]
    gb_refs = refs[2 + n_chunks:2 + 2 * n_chunks]
    x_ref, wa_ref, wb_ref, wo_ref, o_ref, m_ref = refs[2 + 2 * n_chunks:]
    tn = o_ref.shape[1] // n_chunks
    for c in range(n_chunks):
        sl = slice(c * tn, (c + 1) * tn)
        ya = jnp.dot(oa_ref[...], wa_ref[:, sl], preferred_element_type=jnp.float32)
        yb = jnp.dot(ob_ref[...], wb_ref[:, sl], preferred_element_type=jnp.float32)
        ga = jax.nn.sigmoid(ga_refs[c][...].astype(jnp.float32))
        gb = jax.nn.sigmoid(gb_refs[c][...].astype(jnp.float32))
        m_ref[:, sl] = (ga * ya + gb * yb).astype(m_ref.dtype)
    for c in range(n_chunks):
        sl = slice(c * tn, (c + 1) * tn)
        o_ref[:, sl] = x_ref[:, sl] + jnp.dot(m_ref[...], wo_ref[:, sl],
                                              preferred_element_type=jnp.float32)


def _merge(oa, ob, proj, x, wa, wb, wo, *, tm=512, tn=512):
    s, d = x.shape
    n_chunks = d // tn
    const = lambda i: (0, 0)
    gate_specs = [pl.BlockSpec((tm, tn), functools.partial(lambda i, blk: (i, blk), blk=off // tn + c))
                  for off in (GA_OFF, GB_OFF) for c in range(n_chunks)]
    return pl.pallas_call(
        functools.partial(_merge_kernel, n_chunks=n_chunks),
        grid=(s // tm,),
        in_specs=[
            pl.BlockSpec((tm, A_Q_W), lambda i: (i, 0)),
            pl.BlockSpec((tm, B_W), lambda i: (i, 0)),
            *gate_specs,
            pl.BlockSpec((tm, d), lambda i: (i, 0)),
            pl.BlockSpec(wa.shape, const),
            pl.BlockSpec(wb.shape, const),
            pl.BlockSpec(wo.shape, const),
        ],
        out_specs=pl.BlockSpec((tm, d), lambda i: (i, 0)),
        out_shape=jax.ShapeDtypeStruct((s, d), jnp.float32),
        scratch_shapes=[pltpu.VMEM((tm, d), jnp.bfloat16)],
        compiler_params=pltpu.CompilerParams(
            dimension_semantics=("arbitrary",), vmem_limit_bytes=VMEM_LIMIT),
        name="merge",
    )(oa, ob, *([proj] * (2 * n_chunks)), x, wa, wb, wo)


def _ffn_kernel(x_hbm, g_ref, wg_ref, wu_ref, wd_ref, gf_ref, o_ref, xbuf, sem, h_ref, *,
                row_chunk):
    f = pl.program_id(1)

    def start_row_block():
        _rmsnorm_rows(xbuf, h_ref, g_ref[...], row_chunk)
        o_ref[...] = xbuf[...]

    _consume_row_block(x_hbm, xbuf, sem, start_row_block)

    h = h_ref[...]
    gate = jnp.dot(h, wg_ref[...], preferred_element_type=jnp.float32)
    up = jnp.dot(h, wu_ref[...], preferred_element_type=jnp.float32)
    act = (gate * jax.nn.sigmoid(gate) * up).astype(jnp.bfloat16)
    o_ref[...] += jnp.dot(act, wd_ref[...], preferred_element_type=jnp.float32)

    @pl.when(f == pl.num_programs(1) - 1)
    def _():
        _rmsnorm_rows(o_ref, o_ref, gf_ref[...], row_chunk)


def _ffn(x1, g, w_in, w_down, gf, *, tm=1024, tf=512):
    s, d = x1.shape
    nf = D_FF // tf
    return pl.pallas_call(
        functools.partial(_ffn_kernel, row_chunk=NORM_ROWS),
        grid=(s // tm, nf),
        in_specs=[
            pl.BlockSpec(memory_space=pl.ANY),
            pl.BlockSpec((1, d), lambda i, f: (0, 0)),
            pl.BlockSpec((d, tf), lambda i, f: (0, f)),
            pl.BlockSpec((d, tf), lambda i, f: (0, f + nf)),
            pl.BlockSpec((tf, d), lambda i, f: (f, 0)),
            pl.BlockSpec((1, d), lambda i, f: (0, 0)),
        ],
        out_specs=pl.BlockSpec((tm, d), lambda i, f: (i, 0)),
        out_shape=jax.ShapeDtypeStruct((s, d), jnp.float32),
        scratch_shapes=[pltpu.VMEM((tm, d), x1.dtype), pltpu.SemaphoreType.DMA(()),
                        pltpu.VMEM((tm, d), jnp.bfloat16)],
        compiler_params=pltpu.CompilerParams(
            dimension_semantics=("arbitrary", "arbitrary"), vmem_limit_bytes=VMEM_LIMIT),
        name="ffn",
    )(x1, g, w_in, w_in, w_down, gf)


def kernel(x, norm_mix_g, w_in, sink_logits, w_branch_a, w_branch_b, w_out,
           norm_ffn_g, w_ffn_in, w_ffn_down, norm_final_g):
    b, s, d = x.shape
    assert (b, s, d) == (1, SEQ, D_MODEL) and w_in.shape[0] == 1
    x2 = x.reshape(s, d)
    proj = _in_proj(x2, norm_mix_g[0].reshape(1, d), w_in[0])
    f32_weights = [w_branch_a[0], w_branch_b[0], w_out[0], w_ffn_in[0], w_ffn_down[0]]
    oa, ob, (wa, wb, wo, wfi, wfd) = _attention(proj, sink_logits[0], f32_weights)
    x1 = _merge(oa, ob, proj, x2, wa, wb, wo)
    out = _ffn(x1, norm_ffn_g[0].reshape(1, d), wfi, wfd, norm_final_g.reshape(1, d))
    return out.reshape(b, s, d)
```

```python
import functools
import math

import numpy as np
import jax
import jax.numpy as jnp
from jax import lax
from jax.experimental import pallas as pl
from jax.experimental.pallas import tpu as pltpu

D_MODEL = 2048
SEQ = 8192
HEAD_DIM = 64
A_Q_HEADS = 16
A_KV_HEADS = 4
A_GROUP = A_Q_HEADS // A_KV_HEADS
WINDOW = 128
B_HEADS = 16
BLOCK = 128
D_FF = 5632
EPS = 1e-6
SCALE = 1.0 / math.sqrt(HEAD_DIM)

A_Q_W = A_Q_HEADS * HEAD_DIM
A_KV_W = A_KV_HEADS * HEAD_DIM
B_W = B_HEADS * HEAD_DIM
IN_WIDTH = A_Q_W + 2 * A_KV_W + 3 * B_W + 2 * D_MODEL

LANES = 128
BF16_ROWS = 16
PAIRS_B = B_HEADS // 2

QA_OFF = 0
KA_OFF = QA_OFF + A_Q_W
VA_OFF = KA_OFF + A_KV_W
QB_OFF = VA_OFF + A_KV_W
KB_OFF = QB_OFF + B_W
VB_OFF = KB_OFF + B_W
GA_OFF = VB_OFF + B_W
GB_OFF = GA_OFF + D_MODEL

F32_EXP_ZERO = -104.0

VMEM_LIMIT = 56 * 1024 * 1024
NORM_ROWS = 128

_NT = (((1,), (1,)), ((), ()))


def _rmsnorm_rows(src_ref, dst_ref, g, row_chunk, copy_ref=None):
    def body(c, _):
        r = pl.multiple_of(c * row_chunk, row_chunk)
        x = src_ref[pl.ds(r, row_chunk), :]
        if copy_ref is not None:
            copy_ref[pl.ds(r, row_chunk), :] = x
        inv = lax.rsqrt(jnp.mean(x * x, axis=-1, keepdims=True) + EPS)
        dst_ref[pl.ds(r, row_chunk), :] = (x * inv * g).astype(dst_ref.dtype)
        return 0

    lax.fori_loop(0, src_ref.shape[0] // row_chunk, body, 0)


def _consume_row_block(x_hbm, xbuf, sem, consume):
    i = pl.program_id(0)
    rows = xbuf.shape[0]

    def copy(blk):
        return pltpu.make_async_copy(x_hbm.at[pl.ds(blk * rows, rows), :], xbuf, sem)

    @pl.when(pl.program_id(1) == 0)
    def _():
        @pl.when(i == 0)
        def _():
            copy(0).start()

        copy(i).wait()
        consume()

        @pl.when(i + 1 < pl.num_programs(0))
        def _():
            copy(i + 1).start()


def _in_proj_kernel(x_hbm, g_ref, w_ref, o_ref, xbuf, sems, h_ref, *, row_block):
    chunk = xbuf.shape[1]
    n_chunks = h_ref.shape[0] // chunk

    def copy(c, slot):
        return pltpu.make_async_copy(x_hbm.at[pl.ds(c * chunk, chunk), :], xbuf.at[slot],
                                     sems.at[slot])

    @pl.when(pl.program_id(0) == 0)
    def _():
        copy(0, 0).start()

        def body(c, _):
            slot = c % 2

            @pl.when(c + 1 < n_chunks)
            def _():
                copy(c + 1, 1 - slot).start()

            copy(c, slot).wait()
            x = xbuf[slot]
            inv = lax.rsqrt(jnp.mean(x * x, axis=-1, keepdims=True) + EPS)
            r = pl.multiple_of(c * chunk, chunk)
            h_ref[pl.ds(r, chunk), :] = (x * inv * g_ref[...]).astype(h_ref.dtype)
            return 0

        lax.fori_loop(0, n_chunks, body, 0)

    w = w_ref[...].astype(jnp.bfloat16)
    for r in range(0, h_ref.shape[0], row_block):
        o_ref[r:r + row_block, :] = jnp.dot(
            h_ref[r:r + row_block, :], w, preferred_element_type=jnp.float32).astype(o_ref.dtype)


def _in_proj(x, g, w, *, tn=256, chunk=2 * NORM_ROWS, row_block=2048):
    s, d = x.shape
    n = w.shape[1]
    return pl.pallas_call(
        functools.partial(_in_proj_kernel, row_block=row_block),
        grid=(n // tn,),
        in_specs=[
            pl.BlockSpec(memory_space=pl.ANY),
            pl.BlockSpec((1, d), lambda j: (0, 0)),
            pl.BlockSpec((d, tn), lambda j: (0, j)),
        ],
        out_specs=pl.BlockSpec((s, tn), lambda j: (0, j)),
        out_shape=jax.ShapeDtypeStruct((s, n), jnp.bfloat16),
        scratch_shapes=[pltpu.VMEM((2, chunk, d), x.dtype), pltpu.SemaphoreType.DMA((2,)),
                        pltpu.VMEM((s, d), jnp.bfloat16)],
        compiler_params=pltpu.CompilerParams(
            dimension_semantics=("arbitrary",), vmem_limit_bytes=VMEM_LIMIT),
        name="in_proj",
    )(x, g, w)


def _half_swap_matrix(n):
    c = np.arange(n)
    p = np.zeros((n, n), np.float32)
    p[c, c ^ HEAD_DIM] = 1.0
    return jnp.asarray(p, jnp.bfloat16)


def _swa_scores(q_ref, kp_ref, kc_ref, vp_ref, vc_ref, p_ref):
    k = jnp.concatenate([kp_ref[...], kc_ref[...]], axis=0)
    v = jnp.concatenate([vp_ref[...], vc_ref[...]], axis=0)
    k_sw = jnp.dot(k, p_ref[...], preferred_element_type=jnp.float32).astype(k.dtype)
    v_sw = jnp.dot(v, p_ref[...], preferred_element_type=jnp.float32).astype(v.dtype)
    lane = lax.broadcasted_iota(jnp.int32, (BLOCK, LANES), 1)
    left = lane < HEAD_DIM
    out = []
    for j in range(A_KV_HEADS):
        tile = slice((j // 2) * LANES, (j // 2 + 1) * LANES)
        qa = q_ref[:, (2 * j) * LANES:(2 * j + 1) * LANES] * jnp.bfloat16(SCALE)
        qb = q_ref[:, (2 * j + 1) * LANES:(2 * j + 2) * LANES] * jnp.bfloat16(SCALE)
        zero = jnp.zeros_like(qa)
        for side in range(2):
            keep = left if side == 0 else jnp.logical_not(left)
            aligned = (j % 2) == side
            kh = (k if aligned else k_sw)[:, tile]
            vh = (v if aligned else v_sw)[:, tile]
            lhs = jnp.concatenate([jnp.where(keep, qa, zero), jnp.where(keep, qb, zero)], axis=0)
            out.append((lax.dot_general(lhs, kh, _NT, preferred_element_type=jnp.float32), vh))
    return out


def _swa_bias():
    qi = np.arange(BLOCK)[:, None]
    ki = np.arange(2 * BLOCK)[None, :]
    dist = BLOCK + qi - ki
    window = (dist >= 0) & (dist < WINDOW)
    heads = np.arange(1, A_Q_HEADS + 1, dtype=np.float32)
    slopes = np.exp2(np.float32(-8.0) * heads / np.float32(A_Q_HEADS)).astype(np.float32)
    bias = np.where(window[None], -(slopes[:, None, None] * dist[None].astype(np.float32)),
                    -np.inf).astype(np.float32)
    first = np.where((ki >= BLOCK)[None], bias, -np.inf).astype(np.float32)
    both = np.stack([first, bias])
    prev_half = (np.arange(BLOCK)[None, :] > qi)[None, None]
    return jnp.asarray(np.where(prev_half, both[..., :BLOCK], both[..., BLOCK:]))


def _swa_finish(bias_ref, sink_ref, scores, o_ref):
    lane = lax.broadcasted_iota(jnp.int32, (BLOCK, LANES), 1)
    left = lane < HEAD_DIM
    prev_half = lane > lax.broadcasted_iota(jnp.int32, (BLOCK, BLOCK), 0)
    for j in range(A_KV_HEADS):
        outs = []
        for side in range(2):
            s2, vh = scores[2 * j + side]
            ps, inv_den = [], []
            for t in range(2):
                h = A_GROUP * j + 2 * t + side
                st = s2[t * BLOCK:(t + 1) * BLOCK]
                s = jnp.where(prev_half, st[:, :BLOCK], st[:, BLOCK:]) + bias_ref[0, h]
                sink = sink_ref[h]
                m = jnp.maximum(jnp.max(s, axis=-1, keepdims=True), sink)
                p = jnp.exp(s - m)
                den = jnp.sum(p, axis=-1, keepdims=True) + jnp.exp(sink - m)
                pb = p.astype(jnp.bfloat16)
                zero = jnp.zeros_like(pb)
                ps.append(jnp.concatenate([jnp.where(prev_half, pb, zero),
                                           jnp.where(prev_half, zero, pb)], axis=1))
                inv_den.append(1.0 / den)
            o2 = jnp.dot(jnp.concatenate(ps, axis=0), vh, preferred_element_type=jnp.float32)
            outs.append((o2[:BLOCK] * inv_den[0], o2[BLOCK:] * inv_den[1]))
        for t in range(2):
            out = jnp.where(left, outs[0][t], outs[1][t])
            o_ref[:, (2 * j + t) * LANES:(2 * j + t + 1) * LANES] = out.astype(o_ref.dtype)


N_SWA_IN = 8


def _swa_in_specs(step_of):
    qa_blk = QA_OFF // A_Q_W
    ka_blk = KA_OFF // A_KV_W
    va_blk = VA_OFF // A_KV_W
    cur = lambda *g: step_of(*g)
    prev = lambda *g: jnp.maximum(step_of(*g) - 1, 0)
    return [
        pl.BlockSpec(memory_space=pltpu.SMEM),
        pl.BlockSpec((BLOCK, A_Q_W), lambda *g: (cur(*g), qa_blk)),
        pl.BlockSpec((BLOCK, A_KV_W), lambda *g: (prev(*g), ka_blk)),
        pl.BlockSpec((BLOCK, A_KV_W), lambda *g: (cur(*g), ka_blk)),
        pl.BlockSpec((BLOCK, A_KV_W), lambda *g: (prev(*g), va_blk)),
        pl.BlockSpec((BLOCK, A_KV_W), lambda *g: (cur(*g), va_blk)),
        pl.BlockSpec((A_KV_W, A_KV_W), lambda *g: (0, 0)),
        pl.BlockSpec((1, A_Q_HEADS, BLOCK, BLOCK),
                     lambda *g: (jnp.minimum(step_of(*g), 1), 0, 0, 0)),
    ]


STICK_FIRST_BLOCKS = 3
STICK_TOP_ROWS = 32
SOFTPLUS_CLAMP = 40.0
MASKED = 1e30


def _stick_constant():
    j = np.arange(BLOCK)[:, None]
    s = np.arange(BLOCK)[None, :]
    tri = (j >= s).astype(np.float32)
    zero = np.zeros_like(tri)
    return jnp.asarray(np.block([[tri, zero], [zero, tri]]), jnp.bfloat16)


def _row_total(sums):
    return jnp.broadcast_to(sums[:, :1], sums.shape)


def _softplus(z):
    return jnp.maximum(z, jnp.log(1.0 + jnp.exp(jnp.minimum(z, SOFTPLUS_CLAMP))))


def _stick_scores_clamped(lhs, k_ref, blk):
    nk = STICK_FIRST_BLOCKS
    first = jnp.maximum(blk - (nk - 1), 0)
    start = pl.multiple_of(first * BLOCK, BLOCK)
    z = lax.dot_general(lhs, k_ref[pl.ds(start, nk * BLOCK), :], _NT,
                        preferred_element_type=jnp.float32)
    return [z[:, n * BLOCK:(n + 1) * BLOCK] for n in range(nk)], first


def _stick_masked(z, blk, first, mask_all):
    nk = STICK_FIRST_BLOCKS
    row = lax.broadcasted_iota(jnp.int32, (2 * BLOCK, BLOCK), 0) & (BLOCK - 1)
    col = lax.broadcasted_iota(jnp.int32, (2 * BLOCK, BLOCK), 1)
    out = []
    for n in range(nk):
        zn = z[n]
        if mask_all:
            m = (first + n - blk) * BLOCK + col < row
        elif n == nk - 1:
            m = col < row
        else:
            m = None
        if m is not None:
            zn = jnp.where(m, zn, -MASKED)
        out.append((zn, _softplus(zn).astype(jnp.bfloat16)))
    return out


def _stick_weights(terms):
    tail = None
    a = [None] * len(terms)
    for n in reversed(range(len(terms))):
        zn, sums = terms[n]
        e = zn - sums
        if tail is not None:
            e = e - tail
        a[n] = jnp.exp(e).astype(jnp.bfloat16)
        tail = _row_total(sums) if tail is None else tail + _row_total(sums)
    return a, tail


def _stick_values_clamped(a, v_ref, first):
    start = pl.multiple_of(first * BLOCK, BLOCK)
    v0 = v_ref[pl.ds(start, STICK_FIRST_BLOCKS * BLOCK), :]
    return jnp.dot(jnp.concatenate(a, axis=1), v0, preferred_element_type=jnp.float32)


def _stick_live(tail):
    return (jnp.min(tail) < -F32_EXP_ZERO).astype(jnp.int32)


def _top_rows(x):
    return jnp.concatenate([x[:STICK_TOP_ROWS], x[BLOCK:BLOCK + STICK_TOP_ROWS]], axis=0)


def _add_to_top_rows(x, top):
    t = STICK_TOP_ROWS
    return jnp.concatenate([x[:t] + top[:t], x[t:BLOCK], x[BLOCK:BLOCK + t] + top[t:],
                            x[BLOCK + t:]], axis=0)


def _stick_near(lhs, k_ref, v_ref, blk):
    start = pl.multiple_of((blk - 1) * BLOCK, BLOCK)
    far = pl.multiple_of((blk - 2) * BLOCK, BLOCK)
    z = lax.dot_general(lhs, k_ref[pl.ds(start, 2 * BLOCK), :], _NT,
                        preferred_element_type=jnp.float32)
    zt = lax.dot_general(_top_rows(lhs), k_ref[pl.ds(far, BLOCK), :], _NT,
                         preferred_element_type=jnp.float32)
    row = lax.broadcasted_iota(jnp.int32, (2 * BLOCK, BLOCK), 0) & (BLOCK - 1)
    col = lax.broadcasted_iota(jnp.int32, (2 * BLOCK, BLOCK), 1)
    z_prev = z[:, :BLOCK]
    z_diag = jnp.where(col < row, z[:, BLOCK:], -MASKED)
    return (z_prev, z_diag, zt), (v_ref[pl.ds(start, 2 * BLOCK), :], v_ref[pl.ds(far, BLOCK), :])


def _stick_near_values(zs, sums, sums_top, vs):
    z_prev, z_diag, zt = zs
    s_prev, s_diag = sums[:, :BLOCK], sums[:, BLOCK:]
    a_diag = jnp.exp(z_diag - s_diag)
    tail = _row_total(s_diag)
    a_prev = jnp.exp(z_prev - s_prev - tail)
    tail = tail + _row_total(s_prev)
    a_top = jnp.exp(zt - sums_top - _top_rows(tail))
    acc = jnp.dot(jnp.concatenate([a_prev, a_diag], axis=1).astype(jnp.bfloat16), vs[0],
                  preferred_element_type=jnp.float32)
    acc_top = jnp.dot(a_top.astype(jnp.bfloat16), vs[1], preferred_element_type=jnp.float32)
    return _add_to_top_rows(acc, acc_top), _add_to_top_rows(tail, _row_total(sums_top))


def _stick_rest(lhs, k_ref, v_ref, c, acc, tail, first, live0, top_done=None):
    live = _stick_live

    def cond(carry):
        kb, alive, _, _ = carry
        return jnp.logical_and(kb >= 0, alive > 0)

    def body(carry):
        kb, _, acc, tail = carry
        st = pl.multiple_of(kb * BLOCK, BLOCK)
        z = lax.dot_general(lhs, k_ref[pl.ds(st, BLOCK), :], _NT,
                            preferred_element_type=jnp.float32)
        if top_done is not None:
            row = lax.broadcasted_iota(jnp.int32, z.shape, 0) & (BLOCK - 1)
            z = jnp.where(jnp.logical_and(row < STICK_TOP_ROWS, kb == top_done), -MASKED, z)
        sums = jnp.dot(_softplus(z).astype(jnp.bfloat16), c[:BLOCK, :BLOCK],
                       preferred_element_type=jnp.float32)
        a = jnp.exp(z - sums - tail)
        acc = acc + jnp.dot(a.astype(jnp.bfloat16), v_ref[pl.ds(st, BLOCK), :],
                            preferred_element_type=jnp.float32)
        tail = tail + _row_total(sums)
        return kb - 1, live(tail), acc, tail

    _, _, acc, _ = lax.while_loop(cond, body, (first - 1, live0, acc, tail))
    return acc


def _attn_kernel(*refs, q_blocks, n_cast):
    swa_in, refs = refs[:N_SWA_IN], refs[N_SWA_IN:]
    q_ref, k_ref, v_ref, c_ref = refs[:4]
    w_refs = refs[4:4 + n_cast]
    oa_ref, o_ref = refs[4 + n_cast:6 + n_cast]
    wo_refs = refs[6 + n_cast:]
    it = pl.program_id(1)
    lane = lax.broadcasted_iota(jnp.int32, (BLOCK, LANES), 1)
    left = lane < HEAD_DIM
    c = c_ref[...]

    def run(first_step):
        for w_ref, wo_ref in zip(w_refs, wo_refs):
            wo_ref[...] = w_ref[...].astype(wo_ref.dtype)
        blks = [it * q_blocks + g for g in range(q_blocks)]
        lhss = []
        for g in range(q_blocks):
            q = q_ref[g * BLOCK:(g + 1) * BLOCK, :] * jnp.bfloat16(SCALE)
            zero = jnp.zeros_like(q)
            lhss.append(jnp.concatenate([jnp.where(left, q, zero), jnp.where(left, zero, q)], axis=0))
        _swa_finish(swa_in[7], swa_in[0], _swa_scores(*swa_in[1:7]), oa_ref)
        nk = STICK_FIRST_BLOCKS
        rows = 2 * BLOCK
        if first_step:
            clamped = [_stick_scores_clamped(lhss[g], k_ref, blks[g]) for g in range(q_blocks)]
            z = [zs for zs, _ in clamped]
            firsts = [first for _, first in clamped]
            masked = [_stick_masked(z[g], blks[g], firsts[g], g < nk - 1) for g in range(q_blocks)]
            sps = [sp for m in masked for _, sp in m]
            sums = jnp.dot(
                jnp.concatenate([jnp.concatenate(sps[t:t + 2], axis=1)
                                 for t in range(0, len(sps), 2)], axis=0),
                c, preferred_element_type=jnp.float32)

            def block_sums(t):
                return sums[(t // 2) * rows:(t // 2 + 1) * rows,
                            (t % 2) * BLOCK:(t % 2 + 1) * BLOCK]

            terms = [[(masked[g][n][0], block_sums(g * nk + n)) for n in range(nk)]
                     for g in range(q_blocks)]
            weights = [_stick_weights(terms[g]) for g in range(q_blocks)]
            accs = [_stick_values_clamped(weights[g][0], v_ref, firsts[g])
                    for g in range(q_blocks)]
            tails = [tail for _, tail in weights]
            top_done = [None] * q_blocks
        else:
            near = [_stick_near(lhss[g], k_ref, v_ref, blks[g]) for g in range(q_blocks)]
            sp = [[_softplus(zn).astype(jnp.bfloat16) for zn in zs] for zs, _ in near]
            sums = jnp.dot(jnp.concatenate([jnp.concatenate(s[:2], axis=1) for s in sp], axis=0),
                           c, preferred_element_type=jnp.float32)
            top = 2 * STICK_TOP_ROWS
            sums_top = jnp.dot(jnp.concatenate([s[2] for s in sp], axis=0), c[:BLOCK, :BLOCK],
                               preferred_element_type=jnp.float32)
            vals = [_stick_near_values(near[g][0], sums[g * rows:(g + 1) * rows],
                                       sums_top[g * top:(g + 1) * top], near[g][1])
                    for g in range(q_blocks)]
            accs = [acc for acc, _ in vals]
            tails = [tail for _, tail in vals]
            firsts = [blk - 1 for blk in blks]
            top_done = [blk - 2 for blk in blks]
        lives = [_stick_live(tail) for tail in tails]
        for g in range(q_blocks):
            acc = _stick_rest(lhss[g], k_ref, v_ref, c, accs[g], tails[g], firsts[g], lives[g],
                              top_done[g])
            o_ref[g * BLOCK:(g + 1) * BLOCK, :] = jnp.where(
                left, acc[:BLOCK], acc[BLOCK:]).astype(o_ref.dtype)

    assert q_blocks >= STICK_FIRST_BLOCKS - 1
    pl.when(it == 0)(lambda: run(True))
    pl.when(it > 0)(lambda: run(False))


def _cast_block_spec(shape, steps, n_inner):
    rows, cols = shape
    for n_cb in range(1, steps + 1):
        n_rb = steps // n_cb
        if steps % n_cb == 0 and rows % n_rb == 0 and cols % n_cb == 0 \
                and (rows // n_rb) % BF16_ROWS == 0 and (cols // n_cb) % LANES == 0:
            return pl.BlockSpec(
                (rows // n_rb, cols // n_cb),
                lambda b, i: ((b * n_inner + i) // n_cb, (b * n_inner + i) % n_cb))
    raise ValueError(f"cannot tile {shape} into {steps} blocks")


def _attention(proj, sinks, weights, *, q_blocks=8):
    s = proj.shape[0]
    tq = q_blocks * BLOCK
    n_q = s // tq
    steps = PAIRS_B * n_q
    assert steps == s // BLOCK
    c = _stick_constant()
    qb_blk = QB_OFF // LANES
    kb_blk = KB_OFF // LANES
    vb_blk = VB_OFF // LANES
    step_of = lambda b, i: b * n_q + i
    w_specs = [_cast_block_spec(w.shape, steps, n_q) for w in weights]
    outs = pl.pallas_call(
        functools.partial(_attn_kernel, q_blocks=q_blocks, n_cast=len(weights)),
        grid=(PAIRS_B, n_q),
        in_specs=[
            *_swa_in_specs(step_of),
            pl.BlockSpec((tq, LANES), lambda b, i: (i, qb_blk + b)),
            pl.BlockSpec((s, LANES), lambda b, i: (0, kb_blk + b)),
            pl.BlockSpec((s, LANES), lambda b, i: (0, vb_blk + b)),
            pl.BlockSpec(c.shape, lambda b, i: (0, 0)),
            *w_specs,
        ],
        out_specs=[pl.BlockSpec((BLOCK, A_Q_W), lambda b, i: (step_of(b, i), 0)),
                   pl.BlockSpec((tq, LANES), lambda b, i: (i, b)), *w_specs],
        out_shape=[jax.ShapeDtypeStruct((s, A_Q_W), jnp.bfloat16),
                   jax.ShapeDtypeStruct((s, B_W), jnp.bfloat16)]
        + [jax.ShapeDtypeStruct(w.shape, jnp.bfloat16) for w in weights],
        compiler_params=pltpu.CompilerParams(
            dimension_semantics=("arbitrary", "arbitrary"), vmem_limit_bytes=VMEM_LIMIT),
        name="attention",
    )(sinks, *([proj] * 5), _half_swap_matrix(A_KV_W), _swa_bias(), proj, proj, proj, c, *weights)
    return outs[0], outs[1], outs[2:]


def _merge_kernel(*refs, n_chunks):
    oa_ref, ob_ref = refs[0], refs[1]
    ga_refs = refs[2:2 + n_chunks]
    gb_refs = refs[2 + n_chunks:2 + 2 * n_chunks]
    x_ref, wa_ref, wb_ref, wo_ref, o_ref, m_ref = refs[2 + 2 * n_chunks:]
    tn = o_ref.shape[1] // n_chunks
    for c in range(n_chunks):
        sl = slice(c * tn, (c + 1) * tn)
        ya = jnp.dot(oa_ref[...], wa_ref[:, sl], preferred_element_type=jnp.float32)
        yb = jnp.dot(ob_ref[...], wb_ref[:, sl], preferred_element_type=jnp.float32)
        ga = jax.nn.sigmoid(ga_refs[c][...].astype(jnp.float32))
        gb = jax.nn.sigmoid(gb_refs[c][...].astype(jnp.float32))
        m_ref[:, sl] = (ga * ya + gb * yb).astype(m_ref.dtype)
    for c in range(n_chunks):
        sl = slice(c * tn, (c + 1) * tn)
        o_ref[:, sl] = x_ref[:, sl] + jnp.dot(m_ref[...], wo_ref[:, sl],
                                              preferred_element_type=jnp.float32)


def _merge(oa, ob, proj, x, wa, wb, wo, *, tm=512, tn=512):
    s, d = x.shape
    n_chunks = d // tn
    const = lambda i: (0, 0)
    gate_specs = [pl.BlockSpec((tm, tn), functools.partial(lambda i, blk: (i, blk), blk=off // tn + c))
                  for off in (GA_OFF, GB_OFF) for c in range(n_chunks)]
    return pl.pallas_call(
        functools.partial(_merge_kernel, n_chunks=n_chunks),
        grid=(s // tm,),
        in_specs=[
            pl.BlockSpec((tm, A_Q_W), lambda i: (i, 0)),
            pl.BlockSpec((tm, B_W), lambda i: (i, 0)),
            *gate_specs,
            pl.BlockSpec((tm, d), lambda i: (i, 0)),
            pl.BlockSpec(wa.shape, const),
            pl.BlockSpec(wb.shape, const),
            pl.BlockSpec(wo.shape, const),
        ],
        out_specs=pl.BlockSpec((tm, d), lambda i: (i, 0)),
        out_shape=jax.ShapeDtypeStruct((s, d), jnp.float32),
        scratch_shapes=[pltpu.VMEM((tm, d), jnp.bfloat16)],
        compiler_params=pltpu.CompilerParams(
            dimension_semantics=("arbitrary",), vmem_limit_bytes=VMEM_LIMIT),
        name="merge",
    )(oa, ob, *([proj] * (2 * n_chunks)), x, wa, wb, wo)


def _ffn_kernel(x_hbm, g_ref, wg_ref, wu_ref, wd_ref, gf_ref, o_ref, xbuf, sem, h_ref, *,
                row_chunk):
    f = pl.program_id(1)

    def start_row_block():
        _rmsnorm_rows(xbuf, h_ref, g_ref[...], row_chunk, copy_ref=o_ref)

    _consume_row_block(x_hbm, xbuf, sem, start_row_block)

    h = h_ref[...]
    gate = jnp.dot(h, wg_ref[...], preferred_element_type=jnp.float32)
    up = jnp.dot(h, wu_ref[...], preferred_element_type=jnp.float32)
    act = (gate * jax.nn.sigmoid(gate) * up).astype(jnp.bfloat16)
    o_ref[...] += jnp.dot(act, wd_ref[...], preferred_element_type=jnp.float32)

    @pl.when(f == pl.num_programs(1) - 1)
    def _():
        _rmsnorm_rows(o_ref, o_ref, gf_ref[...], row_chunk)


def _ffn(x1, g, w_in, w_down, gf, *, tm=1024, tf=512):
    s, d = x1.shape
    nf = D_FF // tf
    return pl.pallas_call(
        functools.partial(_ffn_kernel, row_chunk=NORM_ROWS),
        grid=(s // tm, nf),
        in_specs=[
            pl.BlockSpec(memory_space=pl.ANY),
            pl.BlockSpec((1, d), lambda i, f: (0, 0)),
            pl.BlockSpec((d, tf), lambda i, f: (0, f)),
            pl.BlockSpec((d, tf), lambda i, f: (0, f + nf)),
            pl.BlockSpec((tf, d), lambda i, f: (f, 0)),
            pl.BlockSpec((1, d), lambda i, f: (0, 0)),
        ],
        out_specs=pl.BlockSpec((tm, d), lambda i, f: (i, 0)),
        out_shape=jax.ShapeDtypeStruct((s, d), jnp.float32),
        scratch_shapes=[pltpu.VMEM((tm, d), x1.dtype), pltpu.SemaphoreType.DMA(()),
                        pltpu.VMEM((tm, d), jnp.bfloat16)],
        compiler_params=pltpu.CompilerParams(
            dimension_semantics=("arbitrary", "arbitrary"), vmem_limit_bytes=VMEM_LIMIT),
        name="ffn",
    )(x1, g, w_in, w_in, w_down, gf)


def kernel(x, norm_mix_g, w_in, sink_logits, w_branch_a, w_branch_b, w_out,
           norm_ffn_g, w_ffn_in, w_ffn_down, norm_final_g):
    b, s, d = x.shape
    assert (b, s, d) == (1, SEQ, D_MODEL) and w_in.shape[0] == 1
    x2 = x.reshape(s, d)
    proj = _in_proj(x2, norm_mix_g[0].reshape(1, d), w_in[0])
    f32_weights = [w_branch_a[0], w_branch_b[0], w_out[0], w_ffn_in[0], w_ffn_down[0]]
    oa, ob, (wa, wb, wo, wfi, wfd) = _attention(proj, sink_logits[0], f32_weights)
    x1 = _merge(oa, ob, proj, x2, wa, wb, wo)
    out = _ffn(x1, norm_ffn_g[0].reshape(1, d), wfi, wfd, norm_final_g.reshape(1, d))
    return out.reshape(b, s, d)
```
